```python
import math
import jax
import jax.numpy as jnp
from jax import lax
import numpy as np


D_MODEL = 1024
BATCH = 2
SEQ = 16384
DEPTH = 4

HEAD_DIM = 64
N_HEADS = D_MODEL // HEAD_DIM
N_MIXERS = 4
A_HEADS = N_HEADS // 2
B_PAIRS = ((128, 1), (512, 4), (2048, 16))
B_GROUPS = len(B_PAIRS)
NSA_HEADS = N_HEADS
NSA_KV = 4
NSA_GROUP = NSA_HEADS // NSA_KV
NSA_CMP_LEN = 32
NSA_CMP_STRIDE = 16
NSA_CMP_HID = 256
NSA_SLC_LEN = 64
NSA_TOP_N = 16
NSA_WINDOW = 512
NSA_IN = NSA_HEADS * HEAD_DIM + 6 * NSA_KV * HEAD_DIM + 3 * NSA_HEADS
NUM_BUCKETS = 32
MAX_EXACT = NUM_BUCKETS // 2
REL_MAX_DIST = 2048
D_FF = 2816
N_EXPERTS = 8
TOP_K = 2
D_FF_EXPERT = 512
Q_BLOCK = 128
CAUSAL_CHUNK = 512
EPS = 1e-6
SCALE = HEAD_DIM ** -0.5
NEG_BIG = -1e30
FORCE_BONUS = 1e4

kernel_name = 'hybrid_interleaved_mixers_decoder'


def rms_norm(x, g):
    xf = x.astype(jnp.float32)
    y = xf * lax.rsqrt(jnp.mean(xf * xf, axis=-1, keepdims=True) + EPS)
    return (y * g.astype(jnp.float32)).astype(x.dtype)


def t5_bucket(dist):
    n = jnp.maximum(dist, 0)
    log_ratio = jnp.log(jnp.maximum(n, 1).astype(jnp.float32) / MAX_EXACT) / math.log(REL_MAX_DIST / MAX_EXACT)
    large = jnp.minimum(MAX_EXACT + (log_ratio * (NUM_BUCKETS - MAX_EXACT)).astype(jnp.int32), NUM_BUCKETS - 1)
    return jnp.where(n < MAX_EXACT, n, large)


def causal_sweep(block_fn, seq):
    chunk = math.gcd(seq, CAUSAL_CHUNK)
    n_blk = chunk // Q_BLOCK
    outs = []
    for c in range(seq // chunk):
        key_len = (c + 1) * chunk
        starts = c * chunk + Q_BLOCK * jnp.arange(n_blk)
        outs.append(lax.map(lambda t0, kl=key_len: block_fn(t0, kl), starts))
    return jnp.concatenate(outs, axis=0)


def swiglu(h, wg, wu, wd):
    return (jax.nn.silu(h @ wg) * (h @ wu)) @ wd


def moe_ffn(h, router, wg, wu, wd):
    logits = (h @ router).astype(jnp.float32)
    top_v, top_i = lax.top_k(logits, TOP_K)
    gate = jax.nn.softmax(top_v, axis=-1)
    combine = jnp.einsum('bske,bsk->bse', jax.nn.one_hot(top_i, N_EXPERTS, dtype=jnp.float32), gate).astype(h.dtype)
    y = jnp.zeros_like(h)
    for e in range(N_EXPERTS):
        y = y + combine[..., e:e + 1] * swiglu(h, wg[e], wu[e], wd[e])
    return y


def diff_attention(h, w_in, lam, subln_g, w_out, rel_bias, layer_idx):
    bsz, seq, _ = h.shape
    q, k, v = jnp.split(h @ w_in, 3, axis=-1)
    q = q.reshape(bsz, seq, A_HEADS, 2, HEAD_DIM).transpose(0, 2, 3, 1, 4)
    k = k.reshape(bsz, seq, A_HEADS, 2, HEAD_DIM).transpose(0, 2, 3, 1, 4)
    v = v.reshape(bsz, seq, A_HEADS, 2 * HEAD_DIM).transpose(0, 2, 1, 3)
    lam_init = 0.8 - 0.6 * math.exp(-0.3 * layer_idx)
    lf = lam.astype(jnp.float32)
    lam_full = jnp.exp(jnp.sum(lf[0] * lf[1])) - jnp.exp(jnp.sum(lf[2] * lf[3])) + lam_init
    tab = rel_bias.T.reshape(A_HEADS, 2, NUM_BUCKETS).astype(jnp.float32)

    def block(t0, key_len):
        qb = lax.dynamic_slice_in_dim(q, t0, Q_BLOCK, axis=3)
        dist = (t0 + jnp.arange(Q_BLOCK))[:, None] - jnp.arange(key_len)[None, :]
        s = jnp.einsum('bhjqd,bhjkd->bhjqk', qb, k[:, :, :, :key_len]).astype(jnp.float32) * SCALE + tab[:, :, t5_bucket(dist)]
        p = jax.nn.softmax(jnp.where(dist >= 0, s, -jnp.inf), axis=-1)
        a = p[:, :, 0] - lam_full * p[:, :, 1]
        return jnp.einsum('bhqk,bhkd->bhqd', a.astype(v.dtype), v[:, :, :key_len])

    o = causal_sweep(block, seq)
    o = rms_norm(o, subln_g) * (1.0 - lam_init)
    return o.transpose(1, 0, 3, 2, 4).reshape(bsz, seq, D_MODEL) @ w_out


def _dilated_group(q, k, v, window, dilation, rel_bias):
    bsz, nh, seq, hd = q.shape
    w = window // dilation
    chunk = w * dilation
    seq_pad = -(-seq // chunk) * chunk
    m_len = seq_pad // dilation
    nb = m_len // w

    def to_blocks(t):
        t = jnp.pad(t, ((0, 0), (0, 0), (0, seq_pad - seq), (0, 0)))
        t = t.reshape(bsz, nh, m_len, dilation, hd).transpose(0, 1, 3, 2, 4)
        return t.reshape(bsz, nh, dilation, nb, w, hd)

    qb, kb, vb = to_blocks(q), to_blocks(k), to_blocks(v)

    def with_prev(t):
        prev = jnp.pad(t[:, :, :, :-1], ((0, 0), (0, 0), (0, 0), (1, 0), (0, 0), (0, 0)))
        return jnp.concatenate([prev, t], axis=4)

    kc, vc = with_prev(kb), with_prev(vb)
    steps = w + jnp.arange(w)[:, None] - jnp.arange(2 * w)[None, :]
    valid = ((steps >= 0) & (steps <= w))[None] & ~((jnp.arange(nb) == 0)[:, None, None] & (jnp.arange(2 * w) < w)[None, None, :])
    bias = rel_bias[t5_bucket(dilation * steps)].transpose(2, 0, 1).astype(jnp.float32)
    s = jnp.einsum('bhrnqd,bhrnkd->bhrnqk', qb, kc).astype(jnp.float32) * SCALE + bias[:, None, None]
    s = jnp.where(valid, s, -jnp.inf)
    m = jnp.max(s, axis=-1, keepdims=True)
    e = jnp.exp(s - m)
    den = jnp.sum(e, axis=-1, keepdims=True)
    o = jnp.einsum('bhrnqk,bhrnkd->bhrnqd', (e / den).astype(v.dtype), vc)
    lse = m + jnp.log(den)

    def from_blocks(t):
        t = t.reshape(bsz, nh, dilation, m_len, t.shape[-1]).transpose(0, 1, 3, 2, 4)
        return t.reshape(bsz, nh, seq_pad, t.shape[-1])[:, :, :seq]

    return from_blocks(o), from_blocks(lse)[..., 0]


def dilated_attention(h, w_in, w_out, rel_bias):
    bsz, seq, _ = h.shape
    qkv = (h @ w_in).reshape(bsz, seq, 3, N_HEADS, HEAD_DIM).transpose(2, 0, 3, 1, 4)
    outs, lses = [], []
    for window, dil in B_PAIRS:
        o, lse = _dilated_group(qkv[0], qkv[1], qkv[2], window, dil, rel_bias)
        outs.append(o)
        lses.append(lse)
    wts = jax.nn.softmax(jnp.stack(lses), axis=0)
    o = jnp.einsum('gbhs,gbhsd->bhsd', wts.astype(h.dtype), jnp.stack(outs))
    return o.transpose(0, 2, 1, 3).reshape(bsz, seq, D_MODEL) @ w_out


def stick_breaking_attention(h, w_in, w_out):
    bsz, seq, _ = h.shape
    q, k, v = [t.reshape(bsz, seq, N_HEADS, HEAD_DIM).transpose(0, 2, 1, 3) for t in jnp.split(h @ w_in, 3, axis=-1)]

    def block(t0, key_len):
        qb = lax.dynamic_slice_in_dim(q, t0, Q_BLOCK, axis=2)
        z = jnp.einsum('bhqd,bhkd->bhqk', qb, k[:, :, :key_len]).astype(jnp.float32) * SCALE
        before = jnp.arange(key_len)[None, :] < (t0 + jnp.arange(Q_BLOCK))[:, None]
        log_keep = jnp.where(before, jax.nn.log_sigmoid(-z), 0.0)
        later = lax.cumsum(log_keep, axis=3, reverse=True) - log_keep
        a = jnp.where(before, jnp.exp(z + log_keep + later), 0.0)
        return jnp.einsum('bhqk,bhkd->bhqd', a.astype(v.dtype), v[:, :, :key_len])

    o = causal_sweep(block, seq)
    return o.transpose(1, 0, 3, 2, 4).reshape(bsz, seq, D_MODEL) @ w_out


def native_sparse_attention(h, w_in, cmp_pos, cmp_w1, cmp_w2, w_out, rel_bias):
    bsz, seq, _ = h.shape
    z = h @ w_in
    nqc = NSA_HEADS * HEAD_DIM
    nkc = 6 * NSA_KV * HEAD_DIM
    q = z[..., :nqc].reshape(bsz, seq, NSA_KV, NSA_GROUP, HEAD_DIM).transpose(0, 2, 3, 1, 4)
    kv = z[..., nqc:nqc + nkc].reshape(bsz, seq, 6, NSA_KV, HEAD_DIM).transpose(2, 0, 3, 1, 4)
    k_cmp, v_cmp, k_slc, v_slc, k_win, v_win = kv
    gates = jax.nn.sigmoid(z[..., nqc + nkc:].reshape(bsz, seq, NSA_KV, NSA_GROUP, 3)).transpose(0, 2, 3, 1, 4)

    n_cmp = (seq - NSA_CMP_LEN) // NSA_CMP_STRIDE + 1
    cmp_idx = NSA_CMP_STRIDE * jnp.arange(n_cmp)[:, None] + jnp.arange(NSA_CMP_LEN)[None, :]
    cmp_end = NSA_CMP_STRIDE * jnp.arange(n_cmp) + NSA_CMP_LEN - 1

    def compress(t, j):
        blk = (t[:, :, cmp_idx] + cmp_pos[j]).reshape(bsz, NSA_KV, n_cmp, NSA_CMP_LEN * HEAD_DIM)
        return jax.nn.gelu(blk @ cmp_w1[j]) @ cmp_w2[j]

    kc, vc = compress(k_cmp, 0), compress(v_cmp, 1)

    n_slc = seq // NSA_SLC_LEN
    n_sel = min(NSA_TOP_N, n_slc)
    ks_blk = k_slc.reshape(bsz, NSA_KV, n_slc, NSA_SLC_LEN, HEAD_DIM)
    vs_blk = v_slc.reshape(bsz, NSA_KV, n_slc, NSA_SLC_LEN, HEAD_DIM)
    ratio = NSA_SLC_LEN // NSA_CMP_STRIDE
    span = NSA_CMP_LEN // NSA_CMP_STRIDE
    offs = ratio * jnp.arange(n_slc)[:, None] - jnp.arange(ratio + span - 1)[None, :]
    coef = jnp.asarray(np.convolve(np.ones(ratio), np.ones(span)), jnp.float32)
    coef_mat = jnp.where(offs >= 0, coef[None, :], 0.0)
    offs_c = jnp.clip(offs, 0, n_cmp - 1)

    kw_pad = jnp.pad(k_win, ((0, 0), (0, 0), (NSA_WINDOW, 0), (0, 0)))
    vw_pad = jnp.pad(v_win, ((0, 0), (0, 0), (NSA_WINDOW, 0), (0, 0)))

    tab_hg = rel_bias.T.reshape(NSA_KV, NSA_GROUP, NUM_BUCKETS).astype(jnp.float32)
    bi = jnp.arange(bsz)[:, None, None, None]
    hi = jnp.arange(NSA_KV)[None, :, None, None]
    hi6 = jnp.arange(NSA_KV)[None, :, None, None, None, None]
    gi6 = jnp.arange(NSA_GROUP)[None, None, :, None, None, None]
    blk_ids = jnp.arange(n_slc)[None, :]

    def block(i):
        t0 = i * Q_BLOCK
        qpos = t0 + jnp.arange(Q_BLOCK)
        qb = lax.dynamic_slice_in_dim(q, t0, Q_BLOCK, axis=3)
        gb = lax.dynamic_slice_in_dim(gates, t0, Q_BLOCK, axis=3)
        sc = jnp.einsum('bhgqd,bhcd->bhgqc', qb, kc).astype(jnp.float32) * SCALE
        seen = cmp_end[None, :] <= qpos[:, None]
        pc = jax.nn.softmax(jnp.where(seen, sc, NEG_BIG), axis=-1) * (qpos >= NSA_CMP_LEN - 1)[:, None]
        o_cmp = jnp.einsum('bhgqc,bhcd->bhgqd', pc.astype(vc.dtype), vc)
        imp = jnp.sum(pc, axis=2)
        imp_slc = jnp.einsum('bhqjo,jo->bhqj', imp[..., offs_c], coef_mat)
        cur = (qpos // NSA_SLC_LEN)[:, None]
        forced = (blk_ids == 0) | (blk_ids == cur) | (blk_ids == cur - 1)
        score = jnp.where(blk_ids * NSA_SLC_LEN <= qpos[:, None], imp_slc + FORCE_BONUS * forced, -1.0)
        _, sel = lax.top_k(score, n_sel)
        ks = ks_blk[bi, hi, sel]
        vs = vs_blk[bi, hi, sel]
        kpos = sel[..., None] * NSA_SLC_LEN + jnp.arange(NSA_SLC_LEN)
        dist = qpos[None, None, :, None, None] - kpos
        ss = jnp.einsum('bhgqd,bhqnkd->bhgqnk', qb, ks).astype(jnp.float32) * SCALE + tab_hg[hi6, gi6, t5_bucket(dist)[:, :, None]]
        ps = jax.nn.softmax(jnp.where(dist[:, :, None] >= 0, ss, -jnp.inf), axis=(-2, -1))
        o_slc = jnp.einsum('bhgqnk,bhqnkd->bhgqd', ps.astype(vs.dtype), vs)
        kw = lax.dynamic_slice_in_dim(kw_pad, t0, Q_BLOCK + NSA_WINDOW, axis=2)
        vw = lax.dynamic_slice_in_dim(vw_pad, t0, Q_BLOCK + NSA_WINDOW, axis=2)
        kpos_w = t0 - NSA_WINDOW + jnp.arange(Q_BLOCK + NSA_WINDOW)
        dist_w = qpos[:, None] - kpos_w[None, :]
        inside = (dist_w >= 0) & (dist_w < NSA_WINDOW) & (kpos_w >= 0)[None, :]
        sw = jnp.einsum('bhgqd,bhkd->bhgqk', qb, kw).astype(jnp.float32) * SCALE + tab_hg[:, :, t5_bucket(dist_w)]
        pw = jax.nn.softmax(jnp.where(inside, sw, -jnp.inf), axis=-1)
        o_win = jnp.einsum('bhgqk,bhkd->bhgqd', pw.astype(vw.dtype), vw)
        return gb[..., 0:1] * o_cmp + gb[..., 1:2] * o_slc + gb[..., 2:3] * o_win

    o = lax.map(block, jnp.arange(seq // Q_BLOCK))
    return o.transpose(1, 0, 4, 2, 3, 5).reshape(bsz, seq, D_MODEL) @ w_out


def setup_inputs(seed: int = 0) -> dict:
    key = jax.random.key(seed)
    keys = iter(jax.random.split(key, 32))

    def nrm(shape, scale):
        return scale * jax.random.normal(next(keys), shape, jnp.float32)

    d = D_MODEL
    n_rep = -(-DEPTH // N_MIXERS)
    n_dense = (DEPTH + 1) // 2
    n_moe = DEPTH // 2
    flat = NSA_CMP_LEN * HEAD_DIM
    return {
        'x': nrm((BATCH, SEQ, d), 1.0),
        'rel_bias': nrm((NUM_BUCKETS, N_HEADS), 0.2),
        'norm_gains': 1.0 + nrm((DEPTH, 2, d), 0.01),
        'final_gain': 1.0 + nrm((d,), 0.01),
        'a_w_in': nrm((n_rep, d, 3 * d), d ** -0.5),
        'a_lambda': nrm((n_rep, 4, HEAD_DIM), 0.1),
        'a_subln': 1.0 + nrm((n_rep, 2 * HEAD_DIM), 0.01),
        'a_w_out': nrm((n_rep, d, d), d ** -0.5),
        'b_w_in': nrm((n_rep, d, 3 * N_HEADS * HEAD_DIM), d ** -0.5),
        'b_w_out': nrm((n_rep, d, d), d ** -0.5),
        'c_w_in': nrm((n_rep, d, 3 * d), d ** -0.5),
        'c_w_out': nrm((n_rep, d, d), d ** -0.5),
        'd_w_in': nrm((n_rep, d, NSA_IN), d ** -0.5),
        'd_cmp_pos': nrm((n_rep, 2, NSA_CMP_LEN, HEAD_DIM), 0.1),
        'd_cmp_w1': nrm((n_rep, 2, flat, NSA_CMP_HID), flat ** -0.5),
        'd_cmp_w2': nrm((n_rep, 2, NSA_CMP_HID, HEAD_DIM), NSA_CMP_HID ** -0.5),
        'd_w_out': nrm((n_rep, NSA_HEADS * HEAD_DIM, d), d ** -0.5),
        'ffn_w_gate': nrm((n_dense, d, D_FF), d ** -0.5),
        'ffn_w_up': nrm((n_dense, d, D_FF), d ** -0.5),
        'ffn_w_down': nrm((n_dense, D_FF, d), D_FF ** -0.5),
        'moe_router': nrm((n_moe, d, N_EXPERTS), d ** -0.5),
        'moe_w_gate': nrm((n_moe, N_EXPERTS, d, D_FF_EXPERT), d ** -0.5),
        'moe_w_up': nrm((n_moe, N_EXPERTS, d, D_FF_EXPERT), d ** -0.5),
        'moe_w_down': nrm((n_moe, N_EXPERTS, D_FF_EXPERT, d), D_FF_EXPERT ** -0.5),
    }


def reference(x, rel_bias, norm_gains, final_gain, a_w_in, a_lambda, a_subln, a_w_out,
              b_w_in, b_w_out, c_w_in, c_w_out, d_w_in, d_cmp_pos, d_cmp_w1, d_cmp_w2, d_w_out,
              ffn_w_gate, ffn_w_up, ffn_w_down, moe_router, moe_w_gate, moe_w_up, moe_w_down):
    h = x
    for i in range(DEPTH):
        r, mix = divmod(i, N_MIXERS)
        hn = rms_norm(h, norm_gains[i, 0])
        if mix == 0:
            y = diff_attention(hn, a_w_in[r], a_lambda[r], a_subln[r], a_w_out[r], rel_bias, i)
        elif mix == 1:
            y = dilated_attention(hn, b_w_in[r], b_w_out[r], rel_bias)
        elif mix == 2:
            y = stick_breaking_attention(hn, c_w_in[r], c_w_out[r])
        else:
            y = native_sparse_attention(hn, d_w_in[r], d_cmp_pos[r], d_cmp_w1[r], d_cmp_w2[r], d_w_out[r], rel_bias)
        h = h + y
        hn = rms_norm(h, norm_gains[i, 1])
        j = i // 2
        if i % 2 == 0:
            y = swiglu(hn, ffn_w_gate[j], ffn_w_up[j], ffn_w_down[j])
        else:
            y = moe_ffn(hn, moe_router[j], moe_w_gate[j], moe_w_up[j], moe_w_down[j])
        h = h + y
    return rms_norm(h, final_gain)
```

```python
import functools
import math

import numpy as np
import jax
import jax.numpy as jnp
from jax import lax
from jax.experimental import pallas as pl
from jax.experimental.pallas import tpu as pltpu

F32 = jnp.float32
BF16 = jnp.bfloat16

D_MODEL = 1024
HEAD_DIM = 64
N_HEADS = 16
LANES = 128
B_PAIRS = ((128, 1), (512, 4), (2048, 16))
DIL_W = 128
NSA_KV = 4
NSA_GROUP = 4
NSA_CMP_LEN = 32
NSA_CMP_STRIDE = 16
NSA_CMP_HID = 256
NSA_SLC_LEN = 64
NSA_TOP_N = 16
NSA_WINDOW = 512
NUM_BUCKETS = 32
MAX_EXACT = 16
REL_MAX_DIST = 2048
N_EXPERTS = 8
EPS = 1e-6
SCALE = HEAD_DIM ** -0.5
NEG = -1e30
FORCE_BONUS = 1e4
MASKED_BUCKET = NUM_BUCKETS

ATT_T = 256
N_NEAR = 7
SB_T = 128
SB_CUTOFF = -104.0
VMEM_LIMIT = 56 << 20


def _cparams(sem):
    return pltpu.CompilerParams(dimension_semantics=sem, vmem_limit_bytes=VMEM_LIMIT)


def _nt_dot(a, b):
    return lax.dot_general(a, b, (((1,), (1,)), ((), ())), preferred_element_type=F32)


def _dot(a, b):
    return jnp.dot(a, b, preferred_element_type=F32)


def _split2(x):
    hi = x.astype(BF16)
    lo = (x - hi.astype(F32)).astype(BF16)
    return hi, lo


def _rms(x, g):
    ms = jnp.mean(x * x, axis=-1, keepdims=True)
    return x * lax.rsqrt(ms + EPS) * g


def _sigmoid(x):
    return 1.0 / (1.0 + jnp.exp(-x))


def _t5_bucket_np(dist):
    n = np.maximum(dist, 0)
    ratio = np.log(np.maximum(n, 1).astype(np.float32) / np.float32(MAX_EXACT)) / np.float32(
        math.log(REL_MAX_DIST / MAX_EXACT))
    large = np.minimum(MAX_EXACT + (ratio * np.float32(NUM_BUCKETS - MAX_EXACT)).astype(np.int32),
                       NUM_BUCKETS - 1)
    return np.where(n < MAX_EXACT, n, large).astype(np.int32)


def _causal_bucket_tiles(t, n_tiles, max_dist=None):
    i = np.arange(t)[None, :, None]
    j = np.arange(t)[None, None, :]
    dist = np.arange(n_tiles)[:, None, None] * t + i - j
    ok = dist >= 0
    if max_dist is not None:
        ok = ok & (dist < max_dist)
    return np.where(ok, _t5_bucket_np(dist), MASKED_BUCKET).astype(np.int32)


def _dilated_bucket_tiles(dilation):
    w = DIL_W
    steps = w + np.arange(w)[:, None] - np.arange(2 * w)[None, :]
    ok = (steps >= 0) & (steps <= w)
    bkt = np.where(ok, _t5_bucket_np(dilation * steps), MASKED_BUCKET)
    first = np.where(np.arange(2 * w)[None, :] < w, MASKED_BUCKET, bkt)
    return np.stack([first, bkt]).astype(np.int32)


def _bias_build_kernel(tab_ref, bkt_ref, o_ref):
    c = pl.program_id(0)
    bkt = bkt_ref[0]
    out = jnp.full(bkt.shape, NEG, F32)
    for b in range(NUM_BUCKETS):
        out = jnp.where(bkt == b, tab_ref[c, b], out)
    o_ref[0, 0] = out


def _build_bias_tiles(tab, buckets):
    n, r, c = buckets.shape
    return pl.pallas_call(
        _bias_build_kernel,
        out_shape=jax.ShapeDtypeStruct((N_HEADS, n, r, c), F32),
        grid=(N_HEADS, n),
        in_specs=[pl.BlockSpec(memory_space=pltpu.SMEM),
                  pl.BlockSpec((1, r, c), lambda h, i: (i, 0, 0))],
        out_specs=pl.BlockSpec((1, 1, r, c), lambda h, i: (h, i, 0, 0)),
        compiler_params=_cparams(("arbitrary", "arbitrary")),
        name="bias_tiles",
    )(tab, jnp.asarray(buckets))


def _norm_matmul_kernel(x_ref, g_ref, w_ref, o_ref, *, chunk):
    xn = _rms(x_ref[...], g_ref[...]).astype(BF16)
    nout = o_ref.shape[1]
    for c0 in range(0, nout, chunk):
        c1 = min(c0 + chunk, nout)
        o_ref[:, c0:c1] = _dot(xn, w_ref[:, c0:c1]).astype(o_ref.dtype)


def _norm_matmul(x, g, w, tm=512):
    n, d = x.shape
    nout = w.shape[1]
    return pl.pallas_call(
        functools.partial(_norm_matmul_kernel, chunk=512),
        out_shape=jax.ShapeDtypeStruct((n, nout), BF16),
        grid=(n // tm,),
        in_specs=[pl.BlockSpec((tm, d), lambda i: (i, 0)),
                  pl.BlockSpec((1, d), lambda i: (0, 0)),
                  pl.BlockSpec((d, nout), lambda i: (0, 0))],
        out_specs=pl.BlockSpec((tm, nout), lambda i: (i, 0)),
        compiler_params=_cparams(("parallel",)),
        name="norm_matmul",
    )(x, g.reshape(1, d), w)


def _proj_res_kernel(a_ref, w_ref, h_ref, o_ref):
    o_ref[...] = h_ref[...] + _dot(a_ref[...], w_ref[...])


def _proj_residual(a, w, h, tm=512):
    n, d = h.shape
    k = a.shape[1]
    return pl.pallas_call(
        _proj_res_kernel,
        out_shape=jax.ShapeDtypeStruct((n, d), F32),
        grid=(n // tm,),
        in_specs=[pl.BlockSpec((tm, k), lambda i: (i, 0)),
                  pl.BlockSpec((k, d), lambda i: (0, 0)),
                  pl.BlockSpec((tm, d), lambda i: (i, 0))],
        out_specs=pl.BlockSpec((tm, d), lambda i: (i, 0)),
        compiler_params=_cparams(("parallel",)),
        name="proj_residual",
    )(a, w, h)


def _dil_combine_proj_kernel(o1, o2, o3, l1, l2, l3, w_ref, h_ref, o_ref):
    ls = [l1[...], l2[...], l3[...]]
    m = jnp.maximum(jnp.maximum(ls[0], ls[1]), ls[2])
    es = [jnp.exp(l - m) for l in ls]
    den = es[0] + es[1] + es[2]
    a = (es[0] * o1[...].astype(F32) + es[1] * o2[...].astype(F32) + es[2] * o3[...].astype(F32)) / den
    o_ref[...] = h_ref[...] + _dot(a.astype(BF16), w_ref[...])


def _dil_combine_proj(outs, lses, w, h, tm=512):
    n, d = h.shape
    row = pl.BlockSpec((tm, d), lambda i: (i, 0))
    return pl.pallas_call(
        _dil_combine_proj_kernel,
        out_shape=jax.ShapeDtypeStruct((n, d), F32),
        grid=(n // tm,),
        in_specs=[row] * 6 + [pl.BlockSpec((d, d), lambda i: (0, 0)), row],
        out_specs=row,
        compiler_params=_cparams(("parallel",)),
        name="dilated_combine_proj",
    )(*outs, *lses, w, h)


def _ffn_kernel(h_ref, g_ref, wg_ref, wu_ref, wd_ref, o_ref, xn_ref, acc_ref):
    f = pl.program_id(1)

    @pl.when(f == 0)
    def _():
        xn_ref[...] = _rms(h_ref[...], g_ref[...]).astype(BF16)
        acc_ref[...] = jnp.zeros_like(acc_ref)

    xn = xn_ref[...]
    gate = _dot(xn, wg_ref[...])
    up = _dot(xn, wu_ref[...])
    act = (gate * _sigmoid(gate) * up).astype(BF16)
    acc_ref[...] += _dot(act, wd_ref[...])

    @pl.when(f == pl.num_programs(1) - 1)
    def _():
        o_ref[...] = h_ref[...] + acc_ref[...]


def _ffn(h, g, wg, wu, wd, tm=512, n_f=2):
    n, d = h.shape
    dff = wg.shape[1]
    tf = dff // n_f
    return pl.pallas_call(
        _ffn_kernel,
        out_shape=jax.ShapeDtypeStruct((n, d), F32),
        grid=(n // tm, n_f),
        in_specs=[pl.BlockSpec((tm, d), lambda i, f: (i, 0)),
                  pl.BlockSpec((1, d), lambda i, f: (0, 0)),
                  pl.BlockSpec((d, tf), lambda i, f: (0, f)),
                  pl.BlockSpec((d, tf), lambda i, f: (0, f)),
                  pl.BlockSpec((tf, d), lambda i, f: (f, 0))],
        out_specs=pl.BlockSpec((tm, d), lambda i, f: (i, 0)),
        scratch_shapes=[pltpu.VMEM((tm, d), BF16), pltpu.VMEM((tm, d), F32)],
        compiler_params=_cparams(("parallel", "arbitrary")),
        name="ffn",
    )(h, g.reshape(1, d), wg, wu, wd)


def _moe_kernel(h_ref, g_ref, rh_ref, rl_ref, wg_ref, wu_ref, wd_ref, fg_ref, o_ref,
                xn_ref, comb_ref, acc_ref, *, final_norm):
    e = pl.program_id(1)
    col = lax.broadcasted_iota(jnp.int32, comb_ref.shape, 1)

    @pl.when(e == 0)
    def _():
        xn = _rms(h_ref[...], g_ref[...])
        xh, xl = _split2(xn)
        xn_ref[...] = xh
        logits = _dot(xh, rh_ref[...]) + _dot(xh, rl_ref[...]) + _dot(xl, rh_ref[...])
        colf = col.astype(F32)
        lg = jnp.where(col < N_EXPERTS, logits, NEG)
        m1 = jnp.max(lg, axis=-1, keepdims=True)
        i1 = jnp.min(jnp.where(lg == m1, colf, float(LANES)), axis=-1, keepdims=True)
        lg2 = jnp.where(colf == i1, NEG, lg)
        m2 = jnp.max(lg2, axis=-1, keepdims=True)
        i2 = jnp.min(jnp.where(lg2 == m2, colf, float(LANES)), axis=-1, keepdims=True)
        e2 = jnp.exp(m2 - m1)
        g1 = 1.0 / (1.0 + e2)
        g2 = e2 / (1.0 + e2)
        comb_ref[...] = jnp.where(colf == i1, g1, 0.0) + jnp.where(colf == i2, g2, 0.0)
        acc_ref[...] = jnp.zeros_like(acc_ref)

    xn = xn_ref[...]
    gate = _dot(xn, wg_ref[0])
    up = _dot(xn, wu_ref[0])
    ce = jnp.sum(jnp.where(col == e, comb_ref[...], 0.0), axis=-1, keepdims=True)
    act = (gate * _sigmoid(gate) * up * ce).astype(BF16)
    acc_ref[...] += _dot(act, wd_ref[0])

    @pl.when(e == pl.num_programs(1) - 1)
    def _():
        y = h_ref[...] + acc_ref[...]
        if final_norm:
            y = _rms(y, fg_ref[...])
        o_ref[...] = y


def _moe(h, g, router, wg, wu, wd, final_gain, final_norm, tm=512):
    n, d = h.shape
    ne, _, dfe = wg.shape
    rpad = jnp.zeros((d, LANES), F32).at[:, :ne].set(router)
    rh, rl = _split2(rpad)
    return pl.pallas_call(
        functools.partial(_moe_kernel, final_norm=final_norm),
        out_shape=jax.ShapeDtypeStruct((n, d), F32),
        grid=(n // tm, ne),
        in_specs=[pl.BlockSpec((tm, d), lambda i, e: (i, 0)),
                  pl.BlockSpec((1, d), lambda i, e: (0, 0)),
                  pl.BlockSpec((d, LANES), lambda i, e: (0, 0)),
                  pl.BlockSpec((d, LANES), lambda i, e: (0, 0)),
                  pl.BlockSpec((1, d, dfe), lambda i, e: (e, 0, 0)),
                  pl.BlockSpec((1, d, dfe), lambda i, e: (e, 0, 0)),
                  pl.BlockSpec((1, dfe, d), lambda i, e: (e, 0, 0)),
                  pl.BlockSpec((1, d), lambda i, e: (0, 0))],
        out_specs=pl.BlockSpec((tm, d), lambda i, e: (i, 0)),
        scratch_shapes=[pltpu.VMEM((tm, d), BF16), pltpu.VMEM((tm, LANES), F32),
                        pltpu.VMEM((tm, d), F32)],
        compiler_params=_cparams(("parallel", "arbitrary")),
        name="moe",
    )(h, g.reshape(1, d), rh, rl, wg, wu, wd, final_gain.reshape(1, d))


def _flash_step(s, vb, m_ref, l_ref, acc_ref, idx):
    m_old = m_ref[idx]
    m_new = jnp.maximum(m_old, jnp.max(s, axis=-1, keepdims=True))
    p = jnp.exp(s - m_new)
    alpha = jnp.exp(m_old - m_new)
    l_ref[idx] = alpha * l_ref[idx] + jnp.sum(p, axis=-1, keepdims=True)
    acc_ref[idx] = alpha * acc_ref[idx] + _dot(p.astype(BF16), vb)
    m_ref[idx] = m_new


def _flash_init(m_ref, l_ref, acc_ref):
    m_ref[...] = jnp.full(m_ref.shape, NEG, F32)
    l_ref[...] = jnp.zeros_like(l_ref)
    acc_ref[...] = jnp.zeros_like(acc_ref)


def _diff_attn_kernel(lam_ref, q_ref, k_ref, v_ref, bias_ref, g_ref, o_ref, m_ref, l_ref, acc_ref,
                      *, t, lam_init):
    qi = pl.program_id(2)
    q = q_ref[...]
    lane = lax.broadcasted_iota(jnp.int32, q.shape, 1)
    zero = jnp.zeros_like(q)
    qs = (jnp.where(lane < HEAD_DIM, q, zero), jnp.where(lane >= HEAD_DIM, q, zero))
    _flash_init(m_ref, l_ref, acc_ref)

    def body(j, carry):
        start = pl.multiple_of(j * t, t)
        kb = k_ref[pl.ds(start, t), :]
        vb = v_ref[pl.ds(start, t), :]
        d = jnp.minimum(qi - j, N_NEAR)
        for mi in range(2):
            s = _nt_dot(qs[mi], kb) + bias_ref[mi, d]
            _flash_step(s, vb, m_ref, l_ref, acc_ref, mi)
        return carry

    lax.fori_loop(0, qi + 1, body, 0)

    lam = lam_ref[...]
    lam_full = (jnp.exp(jnp.sum(lam[0:1] * lam[1:2], axis=-1, keepdims=True))
                - jnp.exp(jnp.sum(lam[2:3] * lam[3:4], axis=-1, keepdims=True)) + lam_init)
    o = acc_ref[0] / l_ref[0] - lam_full * (acc_ref[1] / l_ref[1])
    o_ref[...] = (_rms(o, g_ref[...]) * (1.0 - lam_init)).astype(o_ref.dtype)


def _diff_attention(z, lam, subln, bias_tiles, lam_init):
    b, s, _ = z.shape
    t = ATT_T
    nh = N_HEADS // 2
    n_tiles = bias_tiles.shape[1]
    return pl.pallas_call(
        functools.partial(_diff_attn_kernel, t=t, lam_init=lam_init),
        out_shape=jax.ShapeDtypeStruct((b, s, D_MODEL), BF16),
        grid=(b, nh, s // t),
        in_specs=[pl.BlockSpec((4, HEAD_DIM), lambda bi, h, qi: (0, 0)),
                  pl.BlockSpec((None, t, LANES), lambda bi, h, qi: (bi, qi, h)),
                  pl.BlockSpec((None, s, LANES), lambda bi, h, qi: (bi, 0, nh + h)),
                  pl.BlockSpec((None, s, LANES), lambda bi, h, qi: (bi, 0, 2 * nh + h)),
                  pl.BlockSpec((2, n_tiles, t, t), lambda bi, h, qi: (h, 0, 0, 0)),
                  pl.BlockSpec((1, LANES), lambda bi, h, qi: (0, 0))],
        out_specs=pl.BlockSpec((None, t, LANES), lambda bi, h, qi: (bi, qi, h)),
        scratch_shapes=[pltpu.VMEM((2, t, 1), F32), pltpu.VMEM((2, t, 1), F32),
                        pltpu.VMEM((2, t, LANES), F32)],
        compiler_params=_cparams(("parallel", "parallel", "arbitrary")),
        name="diff_attention",
    )(lam, z, z, z, bias_tiles, subln.reshape(1, LANES))


def _dilated_kernel(q_ref, kp_ref, kc_ref, vp_ref, vc_ref, bias_ref, o_ref, lse_ref):
    n = pl.program_id(2)
    variant = jnp.minimum(n, 1)
    w = q_ref.shape[0]
    lane = lax.broadcasted_iota(jnp.int32, (w, LANES), 1)
    low = lane < HEAD_DIM

    def pair(hp, carry):
        cols = pl.ds(pl.multiple_of(hp * LANES, LANES), LANES)
        q = q_ref[:, cols]
        zero = jnp.zeros_like(q)
        kcat = jnp.concatenate([kp_ref[:, cols], kc_ref[:, cols]], axis=0)
        vcat = jnp.concatenate([vp_ref[:, cols], vc_ref[:, cols]], axis=0)
        outs, lses = [], []
        for sub in range(2):
            qm = jnp.where(low if sub == 0 else jnp.logical_not(low), q, zero)
            s = _nt_dot(qm, kcat) + bias_ref[2 * hp + sub, variant]
            m = jnp.max(s, axis=-1, keepdims=True)
            e = jnp.exp(s - m)
            den = jnp.sum(e, axis=-1, keepdims=True)
            outs.append(_dot((e / den).astype(BF16), vcat))
            lses.append(jnp.broadcast_to(m + jnp.log(den), (w, LANES)))
        o_ref[:, cols] = jnp.where(low, outs[0], outs[1]).astype(o_ref.dtype)
        lse_ref[:, cols] = jnp.where(low, lses[0], lses[1])
        return carry

    lax.fori_loop(0, N_HEADS // 2, pair, 0)


def _dilated_group(z, bias, dilation):
    b, s, c3 = z.shape
    w = DIL_W
    m_len = s // dilation
    nb = m_len // w
    zv = z.reshape(b, m_len, dilation * c3)
    blk = lambda sect, prev: pl.BlockSpec(
        (None, w, D_MODEL),
        (lambda bi, r, n: (bi, jnp.maximum(n - 1, 0), 3 * r + sect)) if prev
        else (lambda bi, r, n: (bi, n, 3 * r + sect)))
    out_blk = pl.BlockSpec((None, w, D_MODEL), lambda bi, r, n: (bi, n, r))
    o, lse = pl.pallas_call(
        _dilated_kernel,
        out_shape=(jax.ShapeDtypeStruct((b, m_len, dilation * D_MODEL), BF16),
                   jax.ShapeDtypeStruct((b, m_len, dilation * D_MODEL), F32)),
        grid=(b, dilation, nb),
        in_specs=[blk(0, False), blk(1, True), blk(1, False), blk(2, True), blk(2, False),
                  pl.BlockSpec((N_HEADS, 2, w, 2 * w), lambda bi, r, n: (0, 0, 0, 0))],
        out_specs=(out_blk, out_blk),
        compiler_params=_cparams(("parallel", "parallel", "arbitrary")),
        name=f"dilated_d{dilation}",
    )(zv, zv, zv, zv, zv, bias)
    return o.reshape(b * s, D_MODEL), lse.reshape(b * s, D_MODEL)


def _stick_kernel(q_ref, k_ref, v_ref, o_ref, acc_ref, carry_ref, *, t):
    qi = pl.program_id(2)
    q = q_ref[...]
    lane = lax.broadcasted_iota(jnp.int32, q.shape, 1)
    zero = jnp.zeros_like(q)
    row = lax.broadcasted_iota(jnp.int32, (t, t), 0)
    col = lax.broadcasted_iota(jnp.int32, (t, t), 1)
    suffix = jnp.where(row > col, 1.0, 0.0).astype(BF16)
    before = col < row
    outs = []
    for sub in range(2):
        qm = jnp.where((lane < HEAD_DIM) if sub == 0 else (lane >= HEAD_DIM), q, zero)
        acc_ref[...] = jnp.zeros_like(acc_ref)
        carry_ref[...] = jnp.zeros_like(carry_ref)

        def block(j, masked, qm=qm):
            start = pl.multiple_of(j * t, t)
            kb = k_ref[pl.ds(start, t), :]
            vb = v_ref[pl.ds(start, t), :]
            z = _nt_dot(qm, kb)
            log_keep = -(jnp.maximum(z, 0.0) + jnp.log(1.0 + jnp.exp(-jnp.abs(z))))
            if masked:
                log_keep = jnp.where(before, log_keep, 0.0)
            hi, lo = _split2(log_keep)
            later = _dot(hi, suffix) + _dot(lo, suffix) + carry_ref[...]
            a = jnp.exp(z + log_keep + later)
            if masked:
                a = jnp.where(before, a, 0.0)
            acc_ref[...] += _dot(a.astype(BF16), vb)
            carry_ref[...] += jnp.sum(log_keep, axis=-1, keepdims=True)

        block(qi, True)

        def cond(c):
            return jnp.logical_and(c[0] >= 0, c[1] >= SB_CUTOFF)

        def body(c, block=block):
            block(c[0], False)
            return c[0] - 1, jnp.max(carry_ref[...])

        lax.while_loop(cond, body, (qi - 1, jnp.max(carry_ref[...])))
        outs.append(acc_ref[...])
    o_ref[...] = jnp.where(lane < HEAD_DIM, outs[0], outs[1]).astype(o_ref.dtype)


def _stick_attention(z):
    b, s, _ = z.shape
    t = SB_T
    nh = N_HEADS // 2
    return pl.pallas_call(
        functools.partial(_stick_kernel, t=t),
        out_shape=jax.ShapeDtypeStruct((b, s, D_MODEL), BF16),
        grid=(b, nh, s // t),
        in_specs=[pl.BlockSpec((None, t, LANES), lambda bi, h, qi: (bi, qi, h)),
                  pl.BlockSpec((None, s, LANES), lambda bi, h, qi: (bi, 0, nh + h)),
                  pl.BlockSpec((None, s, LANES), lambda bi, h, qi: (bi, 0, 2 * nh + h))],
        out_specs=pl.BlockSpec((None, t, LANES), lambda bi, h, qi: (bi, qi, h)),
        scratch_shapes=[pltpu.VMEM((t, LANES), F32), pltpu.VMEM((t, 1), F32)],
        compiler_params=_cparams(("parallel", "parallel", "arbitrary")),
        name="stick_breaking",
    )(z, z, z)


def _compress_kernel(a_ref, as_ref, w1_ref, pos_ref, w2_ref, o_ref):
    half = a_ref.shape[1]
    w1 = w1_ref[...]
    const = _dot(jnp.broadcast_to(pos_ref[...], (8, 2 * half)), w1)[0:1]
    pre = _dot(a_ref[...], w1[:half]) + _dot(as_ref[...], w1[half:]) + const
    hid = 0.5 * pre * (1.0 + jnp.tanh(math.sqrt(2.0 / math.pi) * (pre + 0.044715 * pre * pre * pre)))
    o_ref[...] = _dot(hid.astype(BF16), w2_ref[...]).astype(o_ref.dtype)


def _compress(blocks, blocks_next, w1, pos, w2, tr=256):
    _, nkv, b, r, half = blocks.shape
    tr = min(tr, r)
    spec = pl.BlockSpec((None, None, None, tr, half), lambda j, hk, bi, ri: (j, hk, bi, ri, 0))
    return pl.pallas_call(
        _compress_kernel,
        out_shape=jax.ShapeDtypeStruct((2, nkv, b, r, HEAD_DIM), BF16),
        grid=(2, nkv, b, r // tr),
        in_specs=[spec, spec,
                  pl.BlockSpec((None, 2 * half, NSA_CMP_HID), lambda j, hk, bi, ri: (j, 0, 0)),
                  pl.BlockSpec((None, 1, 2 * half), lambda j, hk, bi, ri: (j, 0, 0)),
                  pl.BlockSpec((None, NSA_CMP_HID, HEAD_DIM), lambda j, hk, bi, ri: (j, 0, 0))],
        out_specs=pl.BlockSpec((None, None, None, tr, HEAD_DIM), lambda j, hk, bi, ri: (j, hk, bi, ri, 0)),
        compiler_params=_cparams(("parallel", "parallel", "parallel", "arbitrary")),
        name="nsa_compress",
    )(blocks, blocks_next, w1, pos, w2)


def _nsa_select_kernel(q_ref, kv_ref, cmat_ref, o_ref, sel_ref, *, t, n_sel):
    qi = pl.program_id(2)
    kv = kv_ref[...]
    r = kv.shape[0]
    n_slc = sel_ref.shape[1]
    tpos = qi * t + lax.broadcasted_iota(jnp.int32, (t, 1), 0)
    cidx = lax.broadcasted_iota(jnp.int32, (t, r), 1)
    seen = NSA_CMP_STRIDE * cidx + (NSA_CMP_LEN - 1) <= tpos
    live = (tpos >= NSA_CMP_LEN - 1).astype(F32)
    imp = jnp.zeros((t, r), F32)
    for g in range(NSA_GROUP):
        qg = q_ref[:, g * LANES:(g + 1) * LANES]
        sc = jnp.where(seen, _nt_dot(qg, kv), NEG)
        e = jnp.exp(sc - jnp.max(sc, axis=-1, keepdims=True))
        pc = e / jnp.sum(e, axis=-1, keepdims=True) * live
        o_ref[:, g * LANES:(g + 1) * LANES] = _dot(pc.astype(BF16), kv).astype(o_ref.dtype)
        imp = imp + pc
    hi = imp.astype(BF16)
    rem = imp - hi.astype(F32)
    mid = rem.astype(BF16)
    lo = (rem - mid.astype(F32)).astype(BF16)
    cmat = cmat_ref[...]
    imp_slc = _dot(hi, cmat) + _dot(mid, cmat) + _dot(lo, cmat)
    blk = lax.broadcasted_iota(jnp.int32, (t, n_slc), 1)
    cur = tpos // NSA_SLC_LEN
    forced = (blk == 0) | (blk == cur) | (blk == cur - 1)
    valid = blk * NSA_SLC_LEN <= tpos
    score = jnp.where(valid, imp_slc + jnp.where(forced, FORCE_BONUS, 0.0), -1.0)
    sel = jnp.zeros((t, n_slc), F32)
    blkf = blk.astype(F32)
    for _ in range(n_sel):
        top = jnp.max(score, axis=-1, keepdims=True)
        first = jnp.min(jnp.where(score == top, blkf, float(n_slc)), axis=-1, keepdims=True)
        hit = blkf == first
        sel = jnp.where(hit, 1.0, sel)
        score = jnp.where(hit, -jnp.inf, score)
    sel_ref[...] = jnp.where(valid, sel, 0.0).astype(sel_ref.dtype)


def _nsa_select(z, kv_cmp, cmat, n_sel):
    b, s, _ = z.shape
    t = ATT_T
    r = kv_cmp.shape[2]
    n_slc = cmat.shape[1]
    gw = NSA_GROUP * LANES
    return pl.pallas_call(
        functools.partial(_nsa_select_kernel, t=t, n_sel=n_sel),
        out_shape=(jax.ShapeDtypeStruct((b, s, N_HEADS * LANES), BF16),
                   jax.ShapeDtypeStruct((b, NSA_KV, s, n_slc), BF16)),
        grid=(b, NSA_KV, s // t),
        in_specs=[pl.BlockSpec((None, t, gw), lambda bi, hk, qi: (bi, qi, hk)),
                  pl.BlockSpec((None, None, r, LANES), lambda bi, hk, qi: (bi, hk, 0, 0)),
                  pl.BlockSpec((r, n_slc), lambda bi, hk, qi: (0, 0))],
        out_specs=(pl.BlockSpec((None, t, gw), lambda bi, hk, qi: (bi, qi, hk)),
                   pl.BlockSpec((None, None, t, n_slc), lambda bi, hk, qi: (bi, hk, qi, 0))),
        compiler_params=_cparams(("parallel", "parallel", "arbitrary")),
        name="nsa_select",
    )(z, kv_cmp, cmat)


def _nsa_attend_kernel(q_ref, ks_ref, kw_ref, sel_ref, ocmp_ref, zg_ref, eg_ref, bs_ref, bw_ref, o_ref,
                       m_ref, l_ref, acc_ref, res_ref, *, t, n_win):
    qi = pl.program_id(2)
    n_slc = sel_ref.shape[1]
    per_tile = t // NSA_SLC_LEN
    sel = sel_ref[...]
    delta = (lax.broadcasted_iota(jnp.int32, (n_slc, t), 0)
             - lax.broadcasted_iota(jnp.int32, (n_slc, t), 1) // NSA_SLC_LEN)

    _flash_init(m_ref, l_ref, acc_ref)

    def slc_body(j, carry):
        start = pl.multiple_of(j * t, t)
        kb = ks_ref[pl.ds(start, t), :]
        expand = jnp.where(delta == j * per_tile, 1.0, 0.0).astype(BF16)
        keep = _dot(sel, expand) > 0.5
        d = jnp.minimum(qi - j, N_NEAR)
        for g in range(NSA_GROUP):
            s = _nt_dot(q_ref[:, g * LANES:(g + 1) * LANES], kb) + bs_ref[g, d]
            _flash_step(jnp.where(keep, s, NEG), kb, m_ref, l_ref, acc_ref, g)
        return carry

    lax.fori_loop(0, qi + 1, slc_body, 0)
    for g in range(NSA_GROUP):
        res_ref[g] = acc_ref[g] / l_ref[g]

    _flash_init(m_ref, l_ref, acc_ref)
    for d in range(n_win):
        @pl.when(qi >= d)
        def _(d=d):
            start = pl.multiple_of((qi - d) * t, t)
            kb = kw_ref[pl.ds(start, t), :]
            for g in range(NSA_GROUP):
                s = _nt_dot(q_ref[:, g * LANES:(g + 1) * LANES], kb) + bw_ref[g, d]
                _flash_step(s, kb, m_ref, l_ref, acc_ref, g)

    gates = _sigmoid(zg_ref[...].astype(F32))
    ghi, glo = _split2(gates)
    lane = lax.broadcasted_iota(jnp.int32, (t, LANES), 1)
    mixed = []
    for g in range(NSA_GROUP):
        ge = []
        for c in range(3):
            e = eg_ref[:, (3 * g + c) * LANES:(3 * g + c + 1) * LANES]
            ge.append(_dot(ghi, e) + _dot(glo, e))
        o_win = acc_ref[g] / l_ref[g]
        o_cmp = ocmp_ref[:, g * LANES:(g + 1) * LANES].astype(F32)
        mixed.append(ge[0] * o_cmp + ge[1] * res_ref[g] + ge[2] * o_win)
    for p in range(NSA_GROUP // 2):
        even = pltpu.roll(mixed[2 * p], HEAD_DIM, 1)
        o_ref[:, p * LANES:(p + 1) * LANES] = jnp.where(lane < HEAD_DIM, even, mixed[2 * p + 1]).astype(o_ref.dtype)


def _nsa_attend(z, sel, o_cmp, gate_expand, bias_slc, bias_win, col_slc, col_win, col_gate):
    b, s, _ = z.shape
    t = ATT_T
    n_slc = sel.shape[3]
    gw = NSA_GROUP * LANES
    n_win = bias_win.shape[1]
    n_tiles = bias_slc.shape[1]
    return pl.pallas_call(
        functools.partial(_nsa_attend_kernel, t=t, n_win=n_win),
        out_shape=jax.ShapeDtypeStruct((b, s, D_MODEL), BF16),
        grid=(b, NSA_KV, s // t),
        in_specs=[pl.BlockSpec((None, t, gw), lambda bi, hk, qi: (bi, qi, hk)),
                  pl.BlockSpec((None, s, LANES), lambda bi, hk, qi: (bi, 0, col_slc + hk)),
                  pl.BlockSpec((None, s, LANES), lambda bi, hk, qi: (bi, 0, col_win + hk)),
                  pl.BlockSpec((None, None, t, n_slc), lambda bi, hk, qi: (bi, hk, qi, 0)),
                  pl.BlockSpec((None, t, gw), lambda bi, hk, qi: (bi, qi, hk)),
                  pl.BlockSpec((None, t, LANES), lambda bi, hk, qi: (bi, qi, col_gate)),
                  pl.BlockSpec((None, LANES, 3 * NSA_GROUP * LANES), lambda bi, hk, qi: (hk, 0, 0)),
                  pl.BlockSpec((NSA_GROUP, n_tiles, t, t), lambda bi, hk, qi: (hk, 0, 0, 0)),
                  pl.BlockSpec((NSA_GROUP, n_win, t, t), lambda bi, hk, qi: (hk, 0, 0, 0))],
        out_specs=pl.BlockSpec((None, t, NSA_GROUP * HEAD_DIM), lambda bi, hk, qi: (bi, qi, hk)),
        scratch_shapes=[pltpu.VMEM((NSA_GROUP, t, 1), F32), pltpu.VMEM((NSA_GROUP, t, 1), F32),
                        pltpu.VMEM((NSA_GROUP, t, LANES), F32), pltpu.VMEM((NSA_GROUP, t, LANES), F32)],
        compiler_params=_cparams(("parallel", "parallel", "arbitrary")),
        name="nsa_attend",
    )(z, z, z, sel, o_cmp, z, gate_expand, bias_slc, bias_win)


def _nsa_layout(w_in):
    d = w_in.shape[0]
    nq = N_HEADS * HEAD_DIM
    kvw = NSA_KV * HEAD_DIM
    wq = (w_in[:, :nq] * SCALE).reshape(d, N_HEADS, HEAD_DIM)
    wq = jnp.concatenate([wq, jnp.zeros_like(wq)], axis=-1).reshape(d, N_HEADS * LANES)
    kv = [w_in[:, nq + j * kvw: nq + (j + 1) * kvw].reshape(d, NSA_KV, HEAD_DIM) for j in range(6)]
    pack = lambda k, v: jnp.concatenate([k, v], axis=-1).reshape(d, NSA_KV * LANES)
    wg = w_in[:, nq + 6 * kvw:]
    wg = jnp.concatenate([wg, jnp.zeros((d, LANES - wg.shape[1]), w_in.dtype)], axis=-1)
    w = jnp.concatenate([wq, pack(kv[2], kv[3]), pack(kv[4], kv[5]),
                         kv[0].reshape(d, kvw), kv[1].reshape(d, kvw), wg], axis=-1)
    cols = {"slc": N_HEADS, "win": N_HEADS + NSA_KV,
            "kcmp": (N_HEADS + 2 * NSA_KV) * LANES, "vcmp": (N_HEADS + 2 * NSA_KV) * LANES + kvw,
            "gate": N_HEADS + 2 * NSA_KV + 2 * kvw // LANES}
    return w.astype(BF16), cols


def _gate_expand_np():
    e = np.zeros((NSA_KV, LANES, 3 * NSA_GROUP * LANES), np.float32)
    for hk in range(NSA_KV):
        for g in range(NSA_GROUP):
            for c in range(3):
                e[hk, 3 * (NSA_GROUP * hk + g) + c, (3 * g + c) * LANES:(3 * g + c + 1) * LANES] = 1.0
    return e


def _importance_matrix_np(r, n_slc):
    ratio = NSA_SLC_LEN // NSA_CMP_STRIDE
    span = NSA_CMP_LEN // NSA_CMP_STRIDE
    coef = np.convolve(np.ones(ratio), np.ones(span))
    cmat = np.zeros((r, n_slc), np.float32)
    for j in range(n_slc):
        for o, cf in enumerate(coef):
            c = ratio * j - o
            if c >= 0:
                cmat[c, j] = cf
    return cmat


def _nsa_mixer(hn2d, b, s, w_in, cmp_pos, cmp_w1, cmp_w2, tab, bias_causal):
    w, cols = _nsa_layout(w_in)
    z = _norm_matmul(hn2d[0], hn2d[1], w).reshape(b, s, -1)
    kvw = NSA_KV * HEAD_DIM
    r = s // NSA_CMP_STRIDE

    def rows_of_16(x):
        x = x.reshape(b, -1, NSA_KV, HEAD_DIM).transpose(2, 0, 1, 3)
        return x.reshape(NSA_KV, b, -1, NSA_CMP_STRIDE * HEAD_DIM)

    kv_c = jnp.stack([z[..., cols["kcmp"]:cols["kcmp"] + kvw], z[..., cols["vcmp"]:cols["vcmp"] + kvw]])
    nxt = jnp.concatenate([kv_c[:, :, NSA_CMP_STRIDE:], jnp.zeros_like(kv_c[:, :, :NSA_CMP_STRIDE])], axis=2)
    blocks = jnp.stack([rows_of_16(kv_c[0]), rows_of_16(kv_c[1])])
    blocks_next = jnp.stack([rows_of_16(nxt[0]), rows_of_16(nxt[1])])
    flat = NSA_CMP_LEN * HEAD_DIM
    cmp = _compress(blocks, blocks_next, cmp_w1.astype(BF16), cmp_pos.reshape(2, 1, flat).astype(BF16),
                    cmp_w2.astype(BF16))
    kv_cmp = jnp.concatenate([cmp[0], cmp[1]], axis=-1).transpose(1, 0, 2, 3)

    n_slc = s // NSA_SLC_LEN
    cmat = jnp.asarray(_importance_matrix_np(r, n_slc), BF16)
    o_cmp, sel = _nsa_select(z, kv_cmp, cmat, min(NSA_TOP_N, n_slc))
    bias_win = _build_bias_tiles(tab, _causal_bucket_tiles(ATT_T, -(-NSA_WINDOW // ATT_T) + 1, NSA_WINDOW))
    return _nsa_attend(z, sel, o_cmp, jnp.asarray(_gate_expand_np(), BF16), bias_causal, bias_win,
                       cols["slc"], cols["win"], cols["gate"])


def _scaled_qkv(w_in):
    return jnp.concatenate([w_in[:, :D_MODEL] * SCALE, w_in[:, D_MODEL:]], axis=1).astype(BF16)


def kernel(x, rel_bias, norm_gains, final_gain, a_w_in, a_lambda, a_subln, a_w_out, b_w_in, b_w_out,
           c_w_in, c_w_out, d_w_in, d_cmp_pos, d_cmp_w1, d_cmp_w2, d_w_out, ffn_w_gate, ffn_w_up,
           ffn_w_down, moe_router, moe_w_gate, moe_w_up, moe_w_down):
    b, s, d = x.shape
    assert d == D_MODEL and s % (B_PAIRS[-1][0]) == 0 and s >= (N_NEAR + 1) * ATT_T
    n = b * s
    h = x.reshape(n, d)
    tab = rel_bias.T.astype(F32)
    bias_causal = _build_bias_tiles(tab, _causal_bucket_tiles(ATT_T, N_NEAR + 1))

    z = _norm_matmul(h, norm_gains[0, 0], _scaled_qkv(a_w_in[0])).reshape(b, s, -1)
    lam_init = 0.8 - 0.6 * math.exp(-0.3 * 0)
    o = _diff_attention(z, a_lambda[0].astype(F32), a_subln[0].astype(F32), bias_causal, lam_init)
    h = _proj_residual(o.reshape(n, d), a_w_out[0].astype(BF16), h)
    h = _ffn(h, norm_gains[0, 1], ffn_w_gate[0].astype(BF16), ffn_w_up[0].astype(BF16),
             ffn_w_down[0].astype(BF16))

    z = _norm_matmul(h, norm_gains[1, 0], _scaled_qkv(b_w_in[0])).reshape(b, s, -1)
    outs, lses = [], []
    for _, dil in B_PAIRS:
        bias = _build_bias_tiles(tab, _dilated_bucket_tiles(dil))
        o, lse = _dilated_group(z, bias, dil)
        outs.append(o)
        lses.append(lse)
    h = _dil_combine_proj(outs, lses, b_w_out[0].astype(BF16), h)
    h = _moe(h, norm_gains[1, 1], moe_router[0], moe_w_gate[0].astype(BF16), moe_w_up[0].astype(BF16),
             moe_w_down[0].astype(BF16), final_gain, False)

    z = _norm_matmul(h, norm_gains[2, 0], _scaled_qkv(c_w_in[0])).reshape(b, s, -1)
    o = _stick_attention(z)
    h = _proj_residual(o.reshape(n, d), c_w_out[0].astype(BF16), h)
    h = _ffn(h, norm_gains[2, 1], ffn_w_gate[1].astype(BF16), ffn_w_up[1].astype(BF16),
             ffn_w_down[1].astype(BF16))

    o = _nsa_mixer((h, norm_gains[3, 0]), b, s, d_w_in[0], d_cmp_pos[0], d_cmp_w1[0], d_cmp_w2[0], tab,
                   bias_causal)
    h = _proj_residual(o.reshape(n, d), d_w_out[0].astype(BF16), h)
    h = _moe(h, norm_gains[3, 1], moe_router[1], moe_w_gate[1].astype(BF16), moe_w_up[1].astype(BF16),
             moe_w_down[1].astype(BF16), final_gain, True)
    return h.reshape(b, s, d)
```

```python
import functools
import math

import numpy as np
import jax
import jax.numpy as jnp
from jax import lax
from jax.experimental import pallas as pl
from jax.experimental.pallas import tpu as pltpu

F32 = jnp.float32
BF16 = jnp.bfloat16

D_MODEL = 1024
HEAD_DIM = 64
N_HEADS = 16
LANES = 128
B_PAIRS = ((128, 1), (512, 4), (2048, 16))
DIL_W = 128
NSA_KV = 4
NSA_GROUP = 4
NSA_CMP_LEN = 32
NSA_CMP_STRIDE = 16
NSA_CMP_HID = 256
NSA_SLC_LEN = 64
NSA_TOP_N = 16
NSA_WINDOW = 512
NUM_BUCKETS = 32
MAX_EXACT = 16
REL_MAX_DIST = 2048
N_EXPERTS = 8
EPS = 1e-6
SCALE = HEAD_DIM ** -0.5
NEG = -1e30
FORCE_BONUS = 1e4
MASKED_BUCKET = NUM_BUCKETS

ATT_T = 256
N_NEAR = 7
SB_T = 128
SB_CUTOFF = -104.0
VMEM_LIMIT = 56 << 20


def _cparams(sem):
    return pltpu.CompilerParams(dimension_semantics=sem, vmem_limit_bytes=VMEM_LIMIT)


def _nt_dot(a, b):
    return lax.dot_general(a, b, (((1,), (1,)), ((), ())), preferred_element_type=F32)


def _dot(a, b):
    return jnp.dot(a, b, preferred_element_type=F32)


def _split2(x):
    hi = x.astype(BF16)
    lo = (x - hi.astype(F32)).astype(BF16)
    return hi, lo


def _split3(x):
    hi = x.astype(BF16)
    rem = x - hi.astype(F32)
    mid = rem.astype(BF16)
    lo = (rem - mid.astype(F32)).astype(BF16)
    return hi, mid, lo


def _rms(x, g):
    ms = jnp.mean(x * x, axis=-1, keepdims=True)
    return x * lax.rsqrt(ms + EPS) * g


def _sigmoid(x):
    return 1.0 / (1.0 + jnp.exp(-x))


def _eye(n):
    r = lax.broadcasted_iota(jnp.int32, (n, n), 0)
    c = lax.broadcasted_iota(jnp.int32, (n, n), 1)
    return jnp.where(r == c, 1.0, 0.0).astype(BF16)


def _flip(eye, x):
    return _nt_dot(eye, x)


def _pad_rows(x, top):
    zero = jnp.zeros_like(x)
    return jnp.concatenate([x, zero] if top else [zero, x], axis=0)


def _t5_bucket_np(dist):
    n = np.maximum(dist, 0)
    ratio = np.log(np.maximum(n, 1).astype(np.float32) / np.float32(MAX_EXACT)) / np.float32(
        math.log(REL_MAX_DIST / MAX_EXACT))
    large = np.minimum(MAX_EXACT + (ratio * np.float32(NUM_BUCKETS - MAX_EXACT)).astype(np.int32),
                       NUM_BUCKETS - 1)
    return np.where(n < MAX_EXACT, n, large).astype(np.int32)


def _causal_bucket_tiles(t, n_tiles, max_dist=None):
    i = np.arange(t)[None, None, :]
    j = np.arange(t)[None, :, None]
    dist = np.arange(n_tiles)[:, None, None] * t + i - j
    ok = dist >= 0
    if max_dist is not None:
        ok = ok & (dist < max_dist)
    return np.where(ok, _t5_bucket_np(dist), MASKED_BUCKET).astype(np.int32)


def _dilated_bucket_tiles(dilation):
    w = DIL_W
    steps = w + np.arange(w)[None, :] - np.arange(2 * w)[:, None]
    ok = (steps >= 0) & (steps <= w)
    bkt = np.where(ok, _t5_bucket_np(dilation * steps), MASKED_BUCKET)
    first = np.where(np.arange(2 * w)[:, None] < w, MASKED_BUCKET, bkt)
    return np.stack([first, bkt]).astype(np.int32)


def _bias_build_kernel(tab_ref, bkt_ref, o_ref):
    c = pl.program_id(0)
    bkt = bkt_ref[0]
    out = jnp.full(bkt.shape, NEG, F32)
    for b in range(NUM_BUCKETS):
        out = jnp.where(bkt == b, tab_ref[c, b], out)
    o_ref[0, 0] = out


def _build_bias_tiles(tab, buckets):
    n, r, c = buckets.shape
    return pl.pallas_call(
        _bias_build_kernel,
        out_shape=jax.ShapeDtypeStruct((N_HEADS, n, r, c), F32),
        grid=(N_HEADS, n),
        in_specs=[pl.BlockSpec(memory_space=pltpu.SMEM),
                  pl.BlockSpec((1, r, c), lambda h, i: (i, 0, 0))],
        out_specs=pl.BlockSpec((1, 1, r, c), lambda h, i: (h, i, 0, 0)),
        compiler_params=_cparams(("arbitrary", "arbitrary")),
        name="bias_tiles",
    )(tab, jnp.asarray(buckets))


def _norm_matmul_kernel(x_ref, g_ref, w_ref, *refs, chunk, feature_major):
    xn = _rms(x_ref[...], g_ref[...]).astype(BF16)
    o_ref = refs[1] if feature_major else refs[0]
    nout = o_ref.shape[1]
    for c0 in range(0, nout, chunk):
        c1 = min(c0 + chunk, nout)
        o_ref[:, c0:c1] = _dot(xn, w_ref[:, c0:c1]).astype(o_ref.dtype)
    if feature_major:
        wt_ref, ot_ref = refs[0], refs[2]
        nt = ot_ref.shape[0]
        for c0 in range(0, nt, chunk):
            c1 = min(c0 + chunk, nt)
            ot_ref[c0:c1, :] = _nt_dot(wt_ref[c0:c1, :], xn).astype(ot_ref.dtype)


def _norm_matmul(x, g, w, wt=None, tm=512):
    n, d = x.shape
    nout = w.shape[1]
    in_specs = [pl.BlockSpec((tm, d), lambda i: (i, 0)),
                pl.BlockSpec((1, d), lambda i: (0, 0)),
                pl.BlockSpec((d, nout), lambda i: (0, 0))]
    out_shape = [jax.ShapeDtypeStruct((n, nout), BF16)]
    out_specs = [pl.BlockSpec((tm, nout), lambda i: (i, 0))]
    args = [x, g.reshape(1, d), w]
    if wt is not None:
        nt = wt.shape[0]
        in_specs.append(pl.BlockSpec((nt, d), lambda i: (0, 0)))
        out_shape.append(jax.ShapeDtypeStruct((nt, n), BF16))
        out_specs.append(pl.BlockSpec((nt, tm), lambda i: (0, i)))
        args.append(wt)
    out = pl.pallas_call(
        functools.partial(_norm_matmul_kernel, chunk=512, feature_major=wt is not None),
        out_shape=tuple(out_shape),
        grid=(n // tm,),
        in_specs=in_specs,
        out_specs=tuple(out_specs),
        compiler_params=_cparams(("parallel",)),
        name="norm_matmul",
    )(*args)
    return out if wt is not None else out[0]


def _proj_res_kernel(a_ref, w_ref, h_ref, o_ref):
    o_ref[...] = h_ref[...] + _dot(a_ref[...], w_ref[...])


def _proj_residual(a, w, h, tm=512):
    n, d = h.shape
    k = a.shape[1]
    return pl.pallas_call(
        _proj_res_kernel,
        out_shape=jax.ShapeDtypeStruct((n, d), F32),
        grid=(n // tm,),
        in_specs=[pl.BlockSpec((tm, k), lambda i: (i, 0)),
                  pl.BlockSpec((k, d), lambda i: (0, 0)),
                  pl.BlockSpec((tm, d), lambda i: (i, 0))],
        out_specs=pl.BlockSpec((tm, d), lambda i: (i, 0)),
        compiler_params=_cparams(("parallel",)),
        name="proj_residual",
    )(a, w, h)


def _dil_combine_proj_kernel(o1, o2, o3, l1, l2, l3, w_ref, h_ref, o_ref):
    ls = [l1[...], l2[...], l3[...]]
    m = jnp.maximum(jnp.maximum(ls[0], ls[1]), ls[2])
    es = [jnp.exp(l - m) for l in ls]
    den = es[0] + es[1] + es[2]
    a = (es[0] * o1[...].astype(F32) + es[1] * o2[...].astype(F32) + es[2] * o3[...].astype(F32)) / den
    o_ref[...] = h_ref[...] + _dot(a.astype(BF16), w_ref[...])


def _dil_combine_proj(outs, lses, w, h, tm=512):
    n, d = h.shape
    row = pl.BlockSpec((tm, d), lambda i: (i, 0))
    return pl.pallas_call(
        _dil_combine_proj_kernel,
        out_shape=jax.ShapeDtypeStruct((n, d), F32),
        grid=(n // tm,),
        in_specs=[row] * 6 + [pl.BlockSpec((d, d), lambda i: (0, 0)), row],
        out_specs=row,
        compiler_params=_cparams(("parallel",)),
        name="dilated_combine_proj",
    )(*outs, *lses, w, h)


def _ffn_kernel(h_ref, g_ref, wg_ref, wu_ref, wd_ref, o_ref, xn_ref, acc_ref):
    f = pl.program_id(1)

    @pl.when(f == 0)
    def _():
        xn_ref[...] = _rms(h_ref[...], g_ref[...]).astype(BF16)
        acc_ref[...] = jnp.zeros_like(acc_ref)

    xn = xn_ref[...]
    gate = _dot(xn, wg_ref[...])
    up = _dot(xn, wu_ref[...])
    act = (gate * _sigmoid(gate) * up).astype(BF16)
    acc_ref[...] += _dot(act, wd_ref[...])

    @pl.when(f == pl.num_programs(1) - 1)
    def _():
        o_ref[...] = h_ref[...] + acc_ref[...]


def _ffn(h, g, wg, wu, wd, tm=512, n_f=2):
    n, d = h.shape
    dff = wg.shape[1]
    tf = dff // n_f
    return pl.pallas_call(
        _ffn_kernel,
        out_shape=jax.ShapeDtypeStruct((n, d), F32),
        grid=(n // tm, n_f),
        in_specs=[pl.BlockSpec((tm, d), lambda i, f: (i, 0)),
                  pl.BlockSpec((1, d), lambda i, f: (0, 0)),
                  pl.BlockSpec((d, tf), lambda i, f: (0, f)),
                  pl.BlockSpec((d, tf), lambda i, f: (0, f)),
                  pl.BlockSpec((tf, d), lambda i, f: (f, 0))],
        out_specs=pl.BlockSpec((tm, d), lambda i, f: (i, 0)),
        scratch_shapes=[pltpu.VMEM((tm, d), BF16), pltpu.VMEM((tm, d), F32)],
        compiler_params=_cparams(("parallel", "arbitrary")),
        name="ffn",
    )(h, g.reshape(1, d), wg, wu, wd)


def _moe_kernel(h_ref, g_ref, rh_ref, rl_ref, wg_ref, wu_ref, wd_ref, fg_ref, o_ref,
                xn_ref, comb_ref, acc_ref, *, final_norm):
    e = pl.program_id(1)
    col = lax.broadcasted_iota(jnp.int32, comb_ref.shape, 1)

    @pl.when(e == 0)
    def _():
        xn = _rms(h_ref[...], g_ref[...])
        xh, xl = _split2(xn)
        xn_ref[...] = xh
        logits = _dot(xh, rh_ref[...]) + _dot(xh, rl_ref[...]) + _dot(xl, rh_ref[...])
        colf = col.astype(F32)
        lg = jnp.where(col < N_EXPERTS, logits, NEG)
        m1 = jnp.max(lg, axis=-1, keepdims=True)
        i1 = jnp.min(jnp.where(lg == m1, colf, float(LANES)), axis=-1, keepdims=True)
        lg2 = jnp.where(colf == i1, NEG, lg)
        m2 = jnp.max(lg2, axis=-1, keepdims=True)
        i2 = jnp.min(jnp.where(lg2 == m2, colf, float(LANES)), axis=-1, keepdims=True)
        e2 = jnp.exp(m2 - m1)
        g1 = 1.0 / (1.0 + e2)
        g2 = e2 / (1.0 + e2)
        comb_ref[...] = jnp.where(colf == i1, g1, 0.0) + jnp.where(colf == i2, g2, 0.0)
        acc_ref[...] = jnp.zeros_like(acc_ref)

    xn = xn_ref[...]
    gate = _dot(xn, wg_ref[0])
    up = _dot(xn, wu_ref[0])
    ce = jnp.sum(jnp.where(col == e, comb_ref[...], 0.0), axis=-1, keepdims=True)
    act = (gate * _sigmoid(gate) * up * ce).astype(BF16)
    acc_ref[...] += _dot(act, wd_ref[0])

    @pl.when(e == pl.num_programs(1) - 1)
    def _():
        y = h_ref[...] + acc_ref[...]
        if final_norm:
            y = _rms(y, fg_ref[...])
        o_ref[...] = y


def _moe(h, g, router, wg, wu, wd, final_gain, final_norm, tm=512):
    n, d = h.shape
    ne, _, dfe = wg.shape
    rpad = jnp.zeros((d, LANES), F32).at[:, :ne].set(router)
    rh, rl = _split2(rpad)
    return pl.pallas_call(
        functools.partial(_moe_kernel, final_norm=final_norm),
        out_shape=jax.ShapeDtypeStruct((n, d), F32),
        grid=(n // tm, ne),
        in_specs=[pl.BlockSpec((tm, d), lambda i, e: (i, 0)),
                  pl.BlockSpec((1, d), lambda i, e: (0, 0)),
                  pl.BlockSpec((d, LANES), lambda i, e: (0, 0)),
                  pl.BlockSpec((d, LANES), lambda i, e: (0, 0)),
                  pl.BlockSpec((1, d, dfe), lambda i, e: (e, 0, 0)),
                  pl.BlockSpec((1, d, dfe), lambda i, e: (e, 0, 0)),
                  pl.BlockSpec((1, dfe, d), lambda i, e: (e, 0, 0)),
                  pl.BlockSpec((1, d), lambda i, e: (0, 0))],
        out_specs=pl.BlockSpec((tm, d), lambda i, e: (i, 0)),
        scratch_shapes=[pltpu.VMEM((tm, d), BF16), pltpu.VMEM((tm, LANES), F32),
                        pltpu.VMEM((tm, d), F32)],
        compiler_params=_cparams(("parallel", "arbitrary")),
        name="moe",
    )(h, g.reshape(1, d), rh, rl, wg, wu, wd, final_gain.reshape(1, d))


def _flash_step(s, vt, m_ref, l_ref, acc_ref, idx):
    m_old = m_ref[idx]
    m_new = jnp.maximum(m_old, jnp.max(s, axis=0, keepdims=True))
    p = jnp.exp(s - m_new)
    alpha = jnp.exp(m_old - m_new)
    l_ref[idx] = alpha * l_ref[idx] + jnp.sum(p, axis=0, keepdims=True)
    acc_ref[idx] = alpha * acc_ref[idx] + _dot(vt, p.astype(BF16))
    m_ref[idx] = m_new


def _flash_init(m_ref, l_ref, acc_ref):
    m_ref[...] = jnp.full(m_ref.shape, NEG, F32)
    l_ref[...] = jnp.zeros_like(l_ref)
    acc_ref[...] = jnp.zeros_like(acc_ref)


def _diff_attn_kernel(lam_ref, q_ref, k_ref, vt_ref, bias_ref, g_ref, o_ref, m_ref, l_ref, acc_ref,
                      *, t, lam_init):
    qi = pl.program_id(2)
    qt = _flip(_eye(LANES), q_ref[...]).astype(BF16)
    rhs = jnp.concatenate([_pad_rows(qt[:HEAD_DIM], True), _pad_rows(qt[HEAD_DIM:], False)], axis=1)
    _flash_init(m_ref, l_ref, acc_ref)

    def body(j, carry):
        start = pl.multiple_of(j * t, t)
        kb = k_ref[pl.ds(start, t), :]
        vt = vt_ref[:, pl.ds(start, t)]
        d = jnp.minimum(qi - j, N_NEAR)
        s2 = _dot(kb, rhs)
        for mi in range(2):
            _flash_step(s2[:, mi * t:(mi + 1) * t] + bias_ref[mi, d], vt, m_ref, l_ref, acc_ref, mi)
        return carry

    lax.fori_loop(0, qi + 1, body, 0)

    lam = lam_ref[...]
    lam_full = (jnp.exp(jnp.sum(lam[0:1] * lam[1:2], axis=-1, keepdims=True))
                - jnp.exp(jnp.sum(lam[2:3] * lam[3:4], axis=-1, keepdims=True)) + lam_init)
    o = acc_ref[0] / l_ref[0] - lam_full * (acc_ref[1] / l_ref[1])
    ms = jnp.mean(o * o, axis=0, keepdims=True)
    y = (o * lax.rsqrt(ms + EPS) * g_ref[...] * (1.0 - lam_init)).astype(BF16)
    o_ref[...] = _flip(_eye(t), y).astype(o_ref.dtype)


def _diff_attention(z, zt, lam, subln, bias_tiles, lam_init):
    b, s, _ = z.shape
    t = ATT_T
    nh = N_HEADS // 2
    n_tiles = bias_tiles.shape[1]
    return pl.pallas_call(
        functools.partial(_diff_attn_kernel, t=t, lam_init=lam_init),
        out_shape=jax.ShapeDtypeStruct((b, s, D_MODEL), BF16),
        grid=(b, nh, s // t),
        in_specs=[pl.BlockSpec((4, HEAD_DIM), lambda bi, h, qi: (0, 0)),
                  pl.BlockSpec((None, t, LANES), lambda bi, h, qi: (bi, qi, h)),
                  pl.BlockSpec((None, s, LANES), lambda bi, h, qi: (bi, 0, nh + h)),
                  pl.BlockSpec((LANES, s), lambda bi, h, qi: (h, bi)),
                  pl.BlockSpec((2, n_tiles, t, t), lambda bi, h, qi: (h, 0, 0, 0)),
                  pl.BlockSpec((LANES, 1), lambda bi, h, qi: (0, 0))],
        out_specs=pl.BlockSpec((None, t, LANES), lambda bi, h, qi: (bi, qi, h)),
        scratch_shapes=[pltpu.VMEM((2, 1, t), F32), pltpu.VMEM((2, 1, t), F32),
                        pltpu.VMEM((2, LANES, t), F32)],
        compiler_params=_cparams(("parallel", "parallel", "arbitrary")),
        name="diff_attention",
    )(lam, z, z, zt, bias_tiles, subln.reshape(LANES, 1))


def _dilated_kernel(q_ref, kp_ref, kc_ref, vp_ref, vc_ref, bias_ref, o_ref, lse_ref):
    n = pl.program_id(2)
    variant = jnp.minimum(n, 1)
    w = q_ref.shape[0]
    eye = _eye(LANES)
    eye_w = _eye(w)

    def pair(hp, carry):
        cols = pl.ds(pl.multiple_of(hp * LANES, LANES), LANES)
        qt = _flip(eye, q_ref[:, cols]).astype(BF16)
        kcat = jnp.concatenate([kp_ref[:, cols], kc_ref[:, cols]], axis=0)
        vcat = jnp.concatenate([vp_ref[:, cols], vc_ref[:, cols]], axis=0)
        vt = _flip(eye, vcat).astype(BF16)
        rhs = jnp.concatenate([_pad_rows(qt[:HEAD_DIM], True), _pad_rows(qt[HEAD_DIM:], False)], axis=1)
        s2 = _dot(kcat, rhs)
        outs, lses = [], []
        for sub in range(2):
            s = s2[:, sub * w:(sub + 1) * w] + bias_ref[2 * hp + sub, variant]
            m = jnp.max(s, axis=0, keepdims=True)
            e = jnp.exp(s - m)
            den = jnp.sum(e, axis=0, keepdims=True)
            p = (e * (1.0 / den)).astype(BF16)
            outs.append(_dot(vt[sub * HEAD_DIM:(sub + 1) * HEAD_DIM], p))
            lses.append(jnp.broadcast_to(m + jnp.log(den), (HEAD_DIM, w)))
        o_ref[:, cols] = _flip(eye_w, jnp.concatenate(outs, axis=0).astype(BF16)).astype(o_ref.dtype)
        lse_ref[:, cols] = sum(_flip(eye_w, term) for term in _split3(jnp.concatenate(lses, axis=0)))
        return carry

    lax.fori_loop(0, N_HEADS // 2, pair, 0)


def _dilated_group(z, bias, dilation):
    b, s, c3 = z.shape
    w = DIL_W
    m_len = s // dilation
    nb = m_len // w
    zv = z.reshape(b, m_len, dilation * c3)
    blk = lambda sect, prev: pl.BlockSpec(
        (None, w, D_MODEL),
        (lambda bi, r, n: (bi, jnp.maximum(n - 1, 0), 3 * r + sect)) if prev
        else (lambda bi, r, n: (bi, n, 3 * r + sect)))
    out_blk = pl.BlockSpec((None, w, D_MODEL), lambda bi, r, n: (bi, n, r))
    o, lse = pl.pallas_call(
        _dilated_kernel,
        out_shape=(jax.ShapeDtypeStruct((b, m_len, dilation * D_MODEL), BF16),
                   jax.ShapeDtypeStruct((b, m_len, dilation * D_MODEL), F32)),
        grid=(b, dilation, nb),
        in_specs=[blk(0, False), blk(1, True), blk(1, False), blk(2, True), blk(2, False),
                  pl.BlockSpec((N_HEADS, 2, 2 * w, w), lambda bi, r, n: (0, 0, 0, 0))],
        out_specs=(out_blk, out_blk),
        compiler_params=_cparams(("parallel", "parallel", "arbitrary")),
        name=f"dilated_d{dilation}",
    )(zv, zv, zv, zv, zv, bias)
    return o.reshape(b * s, D_MODEL), lse.reshape(b * s, D_MODEL)


def _stick_kernel(q_ref, k_ref, vt_ref, o_ref, acc_ref, carry_ref, *, t):
    qi = pl.program_id(2)
    qt = _flip(_eye(LANES), q_ref[...]).astype(BF16)
    row = lax.broadcasted_iota(jnp.int32, (t, t), 0)
    col = lax.broadcasted_iota(jnp.int32, (t, t), 1)
    after = jnp.where(col > row, 1.0, 0.0).astype(BF16)
    before = row < col
    outs = []
    for sub in range(2):
        rhs = _pad_rows(qt[sub * HEAD_DIM:(sub + 1) * HEAD_DIM], sub == 0)
        acc_ref[...] = jnp.zeros_like(acc_ref)
        carry_ref[...] = jnp.zeros_like(carry_ref)

        def block(j, masked, rhs=rhs, sub=sub):
            start = pl.multiple_of(j * t, t)
            kb = k_ref[pl.ds(start, t), :]
            vt = vt_ref[sub * HEAD_DIM:(sub + 1) * HEAD_DIM, pl.ds(start, t)]
            z = _dot(kb, rhs)
            log_keep = -(jnp.maximum(z, 0.0) + jnp.log(1.0 + jnp.exp(-jnp.abs(z))))
            if masked:
                log_keep = jnp.where(before, log_keep, 0.0)
            hi, lo = _split2(log_keep)
            later = _dot(after, hi) + _dot(after, lo) + carry_ref[...]
            a = jnp.exp(z + log_keep + later)
            if masked:
                a = jnp.where(before, a, 0.0)
            acc_ref[...] += _dot(vt, a.astype(BF16))
            carry_ref[...] += jnp.sum(log_keep, axis=0, keepdims=True)

        block(qi, True)

        def cond(c):
            return jnp.logical_and(c[0] >= 0, c[1] >= SB_CUTOFF)

        def body(c, block=block):
            block(c[0], False)
            return c[0] - 1, jnp.max(carry_ref[...])

        lax.while_loop(cond, body, (qi - 1, jnp.max(carry_ref[...])))
        outs.append(acc_ref[...])
    o_ref[...] = _flip(_eye(t), jnp.concatenate(outs, axis=0).astype(BF16)).astype(o_ref.dtype)


def _stick_attention(z, zt):
    b, s, _ = z.shape
    t = SB_T
    nh = N_HEADS // 2
    return pl.pallas_call(
        functools.partial(_stick_kernel, t=t),
        out_shape=jax.ShapeDtypeStruct((b, s, D_MODEL), BF16),
        grid=(b, nh, s // t),
        in_specs=[pl.BlockSpec((None, t, LANES), lambda bi, h, qi: (bi, qi, h)),
                  pl.BlockSpec((None, s, LANES), lambda bi, h, qi: (bi, 0, nh + h)),
                  pl.BlockSpec((LANES, s), lambda bi, h, qi: (h, bi))],
        out_specs=pl.BlockSpec((None, t, LANES), lambda bi, h, qi: (bi, qi, h)),
        scratch_shapes=[pltpu.VMEM((HEAD_DIM, t), F32), pltpu.VMEM((1, t), F32)],
        compiler_params=_cparams(("parallel", "parallel", "arbitrary")),
        name="stick_breaking",
    )(z, z, zt)


def _compress_kernel(a_ref, as_ref, w1_ref, pos_ref, w2_ref, w2t_ref, o_ref, ot_ref):
    half = a_ref.shape[1]
    w1 = w1_ref[...]
    const = _dot(jnp.broadcast_to(pos_ref[...], (8, 2 * half)), w1)[0:1]
    pre = _dot(a_ref[...], w1[:half]) + _dot(as_ref[...], w1[half:]) + const
    hid = 0.5 * pre * (1.0 + jnp.tanh(math.sqrt(2.0 / math.pi) * (pre + 0.044715 * pre * pre * pre)))
    hid = hid.astype(BF16)
    o_ref[...] = _dot(hid, w2_ref[...]).astype(o_ref.dtype)
    ot_ref[...] = _nt_dot(w2t_ref[...], hid).astype(ot_ref.dtype)


def _compress(blocks, blocks_next, w1, pos, w2, tr=256):
    _, nkv, b, r, half = blocks.shape
    tr = min(tr, r)
    w2pad = jnp.concatenate([w2, jnp.zeros_like(w2)], axis=-1)
    w2t = jnp.swapaxes(w2, 1, 2)
    spec = pl.BlockSpec((None, None, None, tr, half), lambda j, hk, bi, ri: (j, hk, bi, ri, 0))
    return pl.pallas_call(
        _compress_kernel,
        out_shape=(jax.ShapeDtypeStruct((2, nkv, b, r, LANES), BF16),
                   jax.ShapeDtypeStruct((2, nkv, b, HEAD_DIM, r), BF16)),
        grid=(2, nkv, b, r // tr),
        in_specs=[spec, spec,
                  pl.BlockSpec((None, 2 * half, NSA_CMP_HID), lambda j, hk, bi, ri: (j, 0, 0)),
                  pl.BlockSpec((None, 1, 2 * half), lambda j, hk, bi, ri: (j, 0, 0)),
                  pl.BlockSpec((None, NSA_CMP_HID, LANES), lambda j, hk, bi, ri: (j, 0, 0)),
                  pl.BlockSpec((None, HEAD_DIM, NSA_CMP_HID), lambda j, hk, bi, ri: (j, 0, 0))],
        out_specs=(pl.BlockSpec((None, None, None, tr, LANES), lambda j, hk, bi, ri: (j, hk, bi, ri, 0)),
                   pl.BlockSpec((None, None, None, HEAD_DIM, tr), lambda j, hk, bi, ri: (j, hk, bi, 0, ri))),
        compiler_params=_cparams(("parallel", "parallel", "parallel", "arbitrary")),
        name="nsa_compress",
    )(blocks, blocks_next, w1, pos, w2pad, w2t)


def _nsa_select_kernel(q_ref, kc_ref, vct_ref, cmat_ref, o_ref, sel_ref, *, t, n_sel):
    qi = pl.program_id(2)
    kc = kc_ref[...]
    vct = vct_ref[...]
    r = kc.shape[0]
    n_slc = sel_ref.shape[0]
    qt = _flip(_eye(NSA_GROUP * HEAD_DIM), q_ref[...]).astype(BF16)
    tpos = qi * t + lax.broadcasted_iota(jnp.int32, (1, t), 1)
    cidx = lax.broadcasted_iota(jnp.int32, (r, t), 0)
    seen = NSA_CMP_STRIDE * cidx + (NSA_CMP_LEN - 1) <= tpos
    live = (tpos >= NSA_CMP_LEN - 1).astype(F32)
    imp = jnp.zeros((r, t), F32)
    for g in range(NSA_GROUP):
        rhs = _pad_rows(qt[g * HEAD_DIM:(g + 1) * HEAD_DIM], True)
        sc = jnp.where(seen, _dot(kc, rhs), NEG)
        e = jnp.exp(sc - jnp.max(sc, axis=0, keepdims=True))
        pc = e * (live / jnp.sum(e, axis=0, keepdims=True))
        o_ref[g * HEAD_DIM:(g + 1) * HEAD_DIM, :] = _dot(vct, pc.astype(BF16)).astype(o_ref.dtype)
        imp = imp + pc
    cmat = cmat_ref[...]
    imp_slc = sum(_dot(cmat, term) for term in _split3(imp))
    blk = lax.broadcasted_iota(jnp.int32, (n_slc, t), 0)
    cur = tpos // NSA_SLC_LEN
    forced = (blk == 0) | (blk == cur) | (blk == cur - 1)
    valid = blk * NSA_SLC_LEN <= tpos
    score = jnp.where(valid, imp_slc + jnp.where(forced, FORCE_BONUS, 0.0), -1.0)
    sel = jnp.zeros((n_slc, t), F32)
    blkf = blk.astype(F32)
    for _ in range(n_sel):
        top = jnp.max(score, axis=0, keepdims=True)
        first = jnp.min(jnp.where(score == top, blkf, float(n_slc)), axis=0, keepdims=True)
        hit = blkf == first
        sel = jnp.where(hit, 1.0, sel)
        score = jnp.where(hit, -jnp.inf, score)
    sel_ref[...] = jnp.where(valid, sel, 0.0).astype(sel_ref.dtype)


def _nsa_select(z, kc, vct, cmat, n_sel):
    b, s, _ = z.shape
    t = ATT_T
    r = kc.shape[2]
    n_slc = cmat.shape[0]
    gw = NSA_GROUP * HEAD_DIM
    return pl.pallas_call(
        functools.partial(_nsa_select_kernel, t=t, n_sel=n_sel),
        out_shape=(jax.ShapeDtypeStruct((b, NSA_KV, gw, s), BF16),
                   jax.ShapeDtypeStruct((b, NSA_KV, n_slc, s), BF16)),
        grid=(b, NSA_KV, s // t),
        in_specs=[pl.BlockSpec((None, t, gw), lambda bi, hk, qi: (bi, qi, hk)),
                  pl.BlockSpec((None, None, r, LANES), lambda bi, hk, qi: (hk, bi, 0, 0)),
                  pl.BlockSpec((None, None, HEAD_DIM, r), lambda bi, hk, qi: (hk, bi, 0, 0)),
                  pl.BlockSpec((n_slc, r), lambda bi, hk, qi: (0, 0))],
        out_specs=(pl.BlockSpec((None, None, gw, t), lambda bi, hk, qi: (bi, hk, 0, qi)),
                   pl.BlockSpec((None, None, n_slc, t), lambda bi, hk, qi: (bi, hk, 0, qi))),
        compiler_params=_cparams(("parallel", "parallel", "arbitrary")),
        name="nsa_select",
    )(z, kc, vct, cmat)


def _nsa_attend_kernel(q_ref, ks_ref, kw_ref, vst_ref, vwt_ref, sel_ref, ocmp_ref, zg_ref, bs_ref, bw_ref,
                       o_ref, m_ref, l_ref, acc_ref, res_ref, gate_ref, *, t, n_win):
    hk = pl.program_id(1)
    qi = pl.program_id(2)
    n_slc = sel_ref.shape[0]
    per_tile = t // NSA_SLC_LEN
    gw = NSA_GROUP * HEAD_DIM
    qt = _flip(_eye(gw), q_ref[...]).astype(BF16)
    low_half = (hk % 2) == 0
    rhs = []
    for g in range(NSA_GROUP):
        qg = qt[g * HEAD_DIM:(g + 1) * HEAD_DIM]
        rhs.append(jnp.where(low_half, _pad_rows(qg, True), _pad_rows(qg, False)))
    rhs = jnp.concatenate(rhs, axis=1)
    sel = sel_ref[...]
    delta = (lax.broadcasted_iota(jnp.int32, (t, n_slc), 1)
             - lax.broadcasted_iota(jnp.int32, (t, n_slc), 0) // NSA_SLC_LEN)

    _flash_init(m_ref, l_ref, acc_ref)

    def slc_body(j, carry):
        start = pl.multiple_of(j * t, t)
        kb = ks_ref[pl.ds(start, t), :]
        vt = vst_ref[:, pl.ds(start, t)]
        expand = jnp.where(delta == j * per_tile, 1.0, 0.0).astype(BF16)
        keep = _dot(expand, sel) > 0.5
        d = jnp.minimum(qi - j, N_NEAR)
        s4 = _dot(kb, rhs)
        for g in range(NSA_GROUP):
            s = jnp.where(keep, s4[:, g * t:(g + 1) * t] + bs_ref[g, d], NEG)
            _flash_step(s, vt, m_ref, l_ref, acc_ref, g)
        return carry

    lax.fori_loop(0, qi + 1, slc_body, 0)
    for g in range(NSA_GROUP):
        res_ref[g] = acc_ref[g] / l_ref[g]

    _flash_init(m_ref, l_ref, acc_ref)
    for d in range(n_win):
        @pl.when(qi >= d)
        def _(d=d):
            start = pl.multiple_of((qi - d) * t, t)
            kb = kw_ref[pl.ds(start, t), :]
            vt = vwt_ref[:, pl.ds(start, t)]
            s4 = _dot(kb, rhs)
            for g in range(NSA_GROUP):
                _flash_step(s4[:, g * t:(g + 1) * t] + bw_ref[g, d], vt, m_ref, l_ref, acc_ref, g)

    gate_ref[...] = _sigmoid(_flip(_eye(LANES), zg_ref[...]))
    mixed = []
    for g in range(NSA_GROUP):
        base = 3 * (NSA_GROUP * hk + g)
        gates = [gate_ref[pl.ds(base + c, 1), :] for c in range(3)]
        o_cmp = ocmp_ref[g * HEAD_DIM:(g + 1) * HEAD_DIM, :].astype(F32)
        mixed.append(gates[0] * o_cmp + gates[1] * res_ref[g] + gates[2] * (acc_ref[g] / l_ref[g]))
    o_ref[...] = _flip(_eye(t), jnp.concatenate(mixed, axis=0).astype(BF16)).astype(o_ref.dtype)


def _nsa_attend(z, zt, sel, o_cmp, bias_slc, bias_win, cols):
    b, s, _ = z.shape
    t = ATT_T
    n_slc = sel.shape[2]
    gw = NSA_GROUP * HEAD_DIM
    n_win = bias_win.shape[1]
    n_tiles = bias_slc.shape[1]
    c_slc, c_win, c_gate = cols["k_slc"], cols["k_win"], cols["gate"]
    return pl.pallas_call(
        functools.partial(_nsa_attend_kernel, t=t, n_win=n_win),
        out_shape=jax.ShapeDtypeStruct((b, s, D_MODEL), BF16),
        grid=(b, NSA_KV, s // t),
        in_specs=[pl.BlockSpec((None, t, gw), lambda bi, hk, qi: (bi, qi, hk)),
                  pl.BlockSpec((None, s, LANES), lambda bi, hk, qi: (bi, 0, c_slc + hk // 2)),
                  pl.BlockSpec((None, s, LANES), lambda bi, hk, qi: (bi, 0, c_win + hk // 2)),
                  pl.BlockSpec((HEAD_DIM, s), lambda bi, hk, qi: (hk, bi)),
                  pl.BlockSpec((HEAD_DIM, s), lambda bi, hk, qi: (NSA_KV + hk, bi)),
                  pl.BlockSpec((None, None, n_slc, t), lambda bi, hk, qi: (bi, hk, 0, qi)),
                  pl.BlockSpec((None, None, gw, t), lambda bi, hk, qi: (bi, hk, 0, qi)),
                  pl.BlockSpec((None, t, LANES), lambda bi, hk, qi: (bi, qi, c_gate)),
                  pl.BlockSpec((NSA_GROUP, n_tiles, t, t), lambda bi, hk, qi: (hk, 0, 0, 0)),
                  pl.BlockSpec((NSA_GROUP, n_win, t, t), lambda bi, hk, qi: (hk, 0, 0, 0))],
        out_specs=pl.BlockSpec((None, t, gw), lambda bi, hk, qi: (bi, qi, hk)),
        scratch_shapes=[pltpu.VMEM((NSA_GROUP, 1, t), F32), pltpu.VMEM((NSA_GROUP, 1, t), F32),
                        pltpu.VMEM((NSA_GROUP, HEAD_DIM, t), F32), pltpu.VMEM((NSA_GROUP, HEAD_DIM, t), F32),
                        pltpu.VMEM((LANES, t), F32)],
        compiler_params=_cparams(("parallel", "parallel", "arbitrary")),
        name="nsa_attend",
    )(z, z, z, zt, zt, sel, o_cmp, z, bias_slc, bias_win)


def _nsa_layout(w_in):
    d = w_in.shape[0]
    nq = N_HEADS * HEAD_DIM
    kvw = NSA_KV * HEAD_DIM
    kv = [w_in[:, nq + j * kvw: nq + (j + 1) * kvw] for j in range(6)]
    wg = w_in[:, nq + 6 * kvw:]
    wg = jnp.concatenate([wg, jnp.zeros((d, LANES - wg.shape[1]), w_in.dtype)], axis=-1)
    w = jnp.concatenate([w_in[:, :nq] * SCALE, kv[2], kv[4], kv[0], kv[1], wg], axis=-1)
    wt = jnp.concatenate([kv[3], kv[5]], axis=-1).T
    cols = {"k_slc": nq // LANES, "k_win": (nq + kvw) // LANES, "k_cmp": nq + 2 * kvw,
            "v_cmp": nq + 3 * kvw, "gate": (nq + 4 * kvw) // LANES}
    return w.astype(BF16), wt.astype(BF16), cols


def _importance_matrix_np(n_slc, r):
    ratio = NSA_SLC_LEN // NSA_CMP_STRIDE
    span = NSA_CMP_LEN // NSA_CMP_STRIDE
    coef = np.convolve(np.ones(ratio), np.ones(span))
    cmat = np.zeros((n_slc, r), np.float32)
    for j in range(n_slc):
        for o, cf in enumerate(coef):
            c = ratio * j - o
            if c >= 0:
                cmat[j, c] = cf
    return cmat


def _nsa_mixer(h, g, b, s, w_in, cmp_pos, cmp_w1, cmp_w2, tab, bias_causal):
    w, wt, cols = _nsa_layout(w_in)
    z, zt = _norm_matmul(h, g, w, wt)
    z = z.reshape(b, s, -1)
    kvw = NSA_KV * HEAD_DIM
    r = s // NSA_CMP_STRIDE

    def rows_of_16(x):
        x = x.reshape(b, -1, NSA_KV, HEAD_DIM).transpose(2, 0, 1, 3)
        return x.reshape(NSA_KV, b, -1, NSA_CMP_STRIDE * HEAD_DIM)

    kv_c = jnp.stack([z[..., cols["k_cmp"]:cols["k_cmp"] + kvw], z[..., cols["v_cmp"]:cols["v_cmp"] + kvw]])
    nxt = jnp.concatenate([kv_c[:, :, NSA_CMP_STRIDE:], jnp.zeros_like(kv_c[:, :, :NSA_CMP_STRIDE])], axis=2)
    blocks = jnp.stack([rows_of_16(kv_c[0]), rows_of_16(kv_c[1])])
    blocks_next = jnp.stack([rows_of_16(nxt[0]), rows_of_16(nxt[1])])
    flat = NSA_CMP_LEN * HEAD_DIM
    cmp_tok, cmp_feat = _compress(blocks, blocks_next, cmp_w1.astype(BF16),
                                  cmp_pos.reshape(2, 1, flat).astype(BF16), cmp_w2.astype(BF16))

    n_slc = s // NSA_SLC_LEN
    cmat = jnp.asarray(_importance_matrix_np(n_slc, r), BF16)
    o_cmp, sel = _nsa_select(z, cmp_tok[0], cmp_feat[1], cmat, min(NSA_TOP_N, n_slc))
    bias_win = _build_bias_tiles(tab, _causal_bucket_tiles(ATT_T, -(-NSA_WINDOW // ATT_T) + 1, NSA_WINDOW))
    return _nsa_attend(z, zt, sel, o_cmp, bias_causal, bias_win, cols)


def _scaled_qkv(w_in):
    return jnp.concatenate([w_in[:, :D_MODEL] * SCALE, w_in[:, D_MODEL:]], axis=1).astype(BF16)


def kernel(x, rel_bias, norm_gains, final_gain, a_w_in, a_lambda, a_subln, a_w_out, b_w_in, b_w_out,
           c_w_in, c_w_out, d_w_in, d_cmp_pos, d_cmp_w1, d_cmp_w2, d_w_out, ffn_w_gate, ffn_w_up,
           ffn_w_down, moe_router, moe_w_gate, moe_w_up, moe_w_down):
    b, s, d = x.shape
    assert d == D_MODEL and s % (B_PAIRS[-1][0]) == 0 and s >= (N_NEAR + 1) * ATT_T
    n = b * s
    h = x.reshape(n, d)
    tab = rel_bias.T.astype(F32)
    bias_causal = _build_bias_tiles(tab, _causal_bucket_tiles(ATT_T, N_NEAR + 1))

    w = _scaled_qkv(a_w_in[0])
    z, zt = _norm_matmul(h, norm_gains[0, 0], w[:, :2 * d], w[:, 2 * d:].T)
    lam_init = 0.8 - 0.6 * math.exp(-0.3 * 0)
    o = _diff_attention(z.reshape(b, s, -1), zt, a_lambda[0].astype(F32), a_subln[0].astype(F32),
                        bias_causal, lam_init)
    h = _proj_residual(o.reshape(n, d), a_w_out[0].astype(BF16), h)
    h = _ffn(h, norm_gains[0, 1], ffn_w_gate[0].astype(BF16), ffn_w_up[0].astype(BF16),
             ffn_w_down[0].astype(BF16))

    z = _norm_matmul(h, norm_gains[1, 0], _scaled_qkv(b_w_in[0])).reshape(b, s, -1)
    outs, lses = [], []
    for _, dil in B_PAIRS:
        bias = _build_bias_tiles(tab, _dilated_bucket_tiles(dil))
        o, lse = _dilated_group(z, bias, dil)
        outs.append(o)
        lses.append(lse)
    h = _dil_combine_proj(outs, lses, b_w_out[0].astype(BF16), h)
    h = _moe(h, norm_gains[1, 1], moe_router[0], moe_w_gate[0].astype(BF16), moe_w_up[0].astype(BF16),
             moe_w_down[0].astype(BF16), final_gain, False)

    w = _scaled_qkv(c_w_in[0])
    z, zt = _norm_matmul(h, norm_gains[2, 0], w[:, :2 * d], w[:, 2 * d:].T)
    o = _stick_attention(z.reshape(b, s, -1), zt)
    h = _proj_residual(o.reshape(n, d), c_w_out[0].astype(BF16), h)
    h = _ffn(h, norm_gains[2, 1], ffn_w_gate[1].astype(BF16), ffn_w_up[1].astype(BF16),
             ffn_w_down[1].astype(BF16))

    o = _nsa_mixer(h, norm_gains[3, 0], b, s, d_w_in[0], d_cmp_pos[0], d_cmp_w1[0], d_cmp_w2[0], tab,
                   bias_causal)
    h = _proj_residual(o.reshape(n, d), d_w_out[0].astype(BF16), h)
    h = _moe(h, norm_gains[3, 1], moe_router[1], moe_w_gate[1].astype(BF16), moe_w_up[1].astype(BF16),
             moe_w_down[1].astype(BF16), final_gain, True)
    return h.reshape(b, s, d)
```

```python
import functools
import math

import numpy as np
import jax
import jax.numpy as jnp
from jax import lax
from jax.experimental import pallas as pl
from jax.experimental.pallas import tpu as pltpu

F32 = jnp.float32
BF16 = jnp.bfloat16

D_MODEL = 1024
HEAD_DIM = 64
N_HEADS = 16
LANES = 128
B_PAIRS = ((128, 1), (512, 4), (2048, 16))
DIL_W = 128
NSA_KV = 4
NSA_GROUP = 4
NSA_CMP_LEN = 32
NSA_CMP_STRIDE = 16
NSA_CMP_HID = 256
NSA_SLC_LEN = 64
NSA_TOP_N = 16
NSA_WINDOW = 512
NUM_BUCKETS = 32
MAX_EXACT = 16
REL_MAX_DIST = 2048
N_EXPERTS = 8
EPS = 1e-6
SCALE = HEAD_DIM ** -0.5
NEG = -1e30
FORCE_BONUS = 1e4
MASKED_BUCKET = NUM_BUCKETS

ATT_T = 256
N_NEAR = 7
ATT_GROUP = 4
NSA_SWEEP_GROUP = 4
DEN_ROWS = 16
LOG2E = math.log2(math.e)
SB_T = 128
SB_CUTOFF = -104.0
VMEM_LIMIT = 56 << 20


def _cparams(sem):
    return pltpu.CompilerParams(dimension_semantics=sem, vmem_limit_bytes=VMEM_LIMIT)


def _resident(block_shape, index_map):
    return pl.BlockSpec(block_shape, index_map, pipeline_mode=pl.Buffered(1))


def _nt_dot(a, b):
    return lax.dot_general(a, b, (((1,), (1,)), ((), ())), preferred_element_type=F32)


def _dot(a, b):
    return jnp.dot(a, b, preferred_element_type=F32)


def _split2(x):
    hi = x.astype(BF16)
    lo = (x - hi.astype(F32)).astype(BF16)
    return hi, lo


def _split3(x):
    hi = x.astype(BF16)
    rem = x - hi.astype(F32)
    mid = rem.astype(BF16)
    lo = (rem - mid.astype(F32)).astype(BF16)
    return hi, mid, lo


def _rms(x, g):
    ms = jnp.mean(x * x, axis=-1, keepdims=True)
    return x * lax.rsqrt(ms + EPS) * g


def _sigmoid(x):
    return 1.0 / (1.0 + jnp.exp(-x))


def _eye(n):
    r = lax.broadcasted_iota(jnp.int32, (n, n), 0)
    c = lax.broadcasted_iota(jnp.int32, (n, n), 1)
    return jnp.where(r == c, 1.0, 0.0).astype(BF16)


def _flip(eye, x):
    return _nt_dot(eye, x)


def _pad_rows(x, top):
    zero = jnp.zeros_like(x)
    return jnp.concatenate([x, zero] if top else [zero, x], axis=0)


def _t5_bucket_np(dist):
    n = np.maximum(dist, 0)
    ratio = np.log(np.maximum(n, 1).astype(np.float32) / np.float32(MAX_EXACT)) / np.float32(
        math.log(REL_MAX_DIST / MAX_EXACT))
    large = np.minimum(MAX_EXACT + (ratio * np.float32(NUM_BUCKETS - MAX_EXACT)).astype(np.int32),
                       NUM_BUCKETS - 1)
    return np.where(n < MAX_EXACT, n, large).astype(np.int32)


def _causal_bucket_tiles(t, n_tiles, max_dist=None):
    i = np.arange(t)[None, None, :]
    j = np.arange(t)[None, :, None]
    dist = np.arange(n_tiles)[:, None, None] * t + i - j
    ok = dist >= 0
    if max_dist is not None:
        ok = ok & (dist < max_dist)
    return np.where(ok, _t5_bucket_np(dist), MASKED_BUCKET).astype(np.int32)


def _sweep_bucket_tiles(t):
    far = np.full((1, t, t), NUM_BUCKETS - 1, np.int32)
    future = np.full((1, t, t), MASKED_BUCKET, np.int32)
    return np.concatenate([_causal_bucket_tiles(t, N_NEAR), far, future], axis=0)


def _bias_tile_index(delta):
    return jnp.where(delta < 0, N_NEAR + 1, jnp.minimum(delta, N_NEAR))


def _dilated_bucket_tiles(dilation):
    w = DIL_W
    steps = w + np.arange(w)[None, :] - np.arange(2 * w)[:, None]
    ok = (steps >= 0) & (steps <= w)
    bkt = np.where(ok, _t5_bucket_np(dilation * steps), MASKED_BUCKET)
    first = np.where(np.arange(2 * w)[:, None] < w, MASKED_BUCKET, bkt)
    return np.stack([first, bkt]).astype(np.int32)


def _bias_build_kernel(tab_ref, bkt_ref, o_ref):
    c = pl.program_id(0)
    bkt = bkt_ref[0]
    out = jnp.full(bkt.shape, NEG, F32)
    for b in range(NUM_BUCKETS):
        out = jnp.where(bkt == b, tab_ref[c, b], out)
    o_ref[0, 0] = out


def _build_bias_tiles(tab, buckets):
    n, r, c = buckets.shape
    return pl.pallas_call(
        _bias_build_kernel,
        out_shape=jax.ShapeDtypeStruct((N_HEADS, n, r, c), F32),
        grid=(N_HEADS, n),
        in_specs=[pl.BlockSpec(memory_space=pltpu.SMEM),
                  pl.BlockSpec((1, r, c), lambda h, i: (i, 0, 0))],
        out_specs=pl.BlockSpec((1, 1, r, c), lambda h, i: (h, i, 0, 0)),
        compiler_params=_cparams(("arbitrary", "arbitrary")),
        name="bias_tiles",
    )(tab, jnp.asarray(buckets))


def _norm_matmul_kernel(x_ref, g_ref, w_ref, *refs, chunk, feature_major):
    xn = _rms(x_ref[...], g_ref[...]).astype(BF16)
    o_ref = refs[1] if feature_major else refs[0]
    nout = o_ref.shape[1]
    for c0 in range(0, nout, chunk):
        c1 = min(c0 + chunk, nout)
        o_ref[:, c0:c1] = _dot(xn, w_ref[:, c0:c1]).astype(o_ref.dtype)
    if feature_major:
        wt_ref, ot_ref = refs[0], refs[2]
        nt = ot_ref.shape[0]
        for c0 in range(0, nt, chunk):
            c1 = min(c0 + chunk, nt)
            ot_ref[c0:c1, :] = _nt_dot(wt_ref[c0:c1, :], xn).astype(ot_ref.dtype)


def _norm_matmul(x, g, w, wt=None, tm=512):
    n, d = x.shape
    nout = w.shape[1]
    in_specs = [pl.BlockSpec((tm, d), lambda i: (i, 0)),
                pl.BlockSpec((1, d), lambda i: (0, 0)),
                pl.BlockSpec((d, nout), lambda i: (0, 0))]
    out_shape = [jax.ShapeDtypeStruct((n, nout), BF16)]
    out_specs = [pl.BlockSpec((tm, nout), lambda i: (i, 0))]
    args = [x, g.reshape(1, d), w]
    if wt is not None:
        nt = wt.shape[0]
        in_specs.append(pl.BlockSpec((nt, d), lambda i: (0, 0)))
        out_shape.append(jax.ShapeDtypeStruct((nt, n), BF16))
        out_specs.append(pl.BlockSpec((nt, tm), lambda i: (0, i)))
        args.append(wt)
    out = pl.pallas_call(
        functools.partial(_norm_matmul_kernel, chunk=512, feature_major=wt is not None),
        out_shape=tuple(out_shape),
        grid=(n // tm,),
        in_specs=in_specs,
        out_specs=tuple(out_specs),
        compiler_params=_cparams(("parallel",)),
        name="norm_matmul",
    )(*args)
    return out if wt is not None else out[0]


def _proj_res_kernel(a_ref, w_ref, h_ref, o_ref):
    o_ref[...] = h_ref[...] + _dot(a_ref[...], w_ref[...])


def _proj_residual(a, w, h, tm=512):
    n, d = h.shape
    k = a.shape[1]
    return pl.pallas_call(
        _proj_res_kernel,
        out_shape=jax.ShapeDtypeStruct((n, d), F32),
        grid=(n // tm,),
        in_specs=[pl.BlockSpec((tm, k), lambda i: (i, 0)),
                  pl.BlockSpec((k, d), lambda i: (0, 0)),
                  pl.BlockSpec((tm, d), lambda i: (i, 0))],
        out_specs=pl.BlockSpec((tm, d), lambda i: (i, 0)),
        compiler_params=_cparams(("parallel",)),
        name="proj_residual",
    )(a, w, h)


def _dil_combine_proj_kernel(o1, o2, o3, l1, l2, l3, w_ref, h_ref, o_ref):
    ls = [l1[...], l2[...], l3[...]]
    m = jnp.maximum(jnp.maximum(ls[0], ls[1]), ls[2])
    es = [jnp.exp(l - m) for l in ls]
    den = es[0] + es[1] + es[2]
    a = (es[0] * o1[...].astype(F32) + es[1] * o2[...].astype(F32) + es[2] * o3[...].astype(F32)) / den
    o_ref[...] = h_ref[...] + _dot(a.astype(BF16), w_ref[...])


def _dil_combine_proj(outs, lses, w, h, tm=512):
    n, d = h.shape
    row = pl.BlockSpec((tm, d), lambda i: (i, 0))
    return pl.pallas_call(
        _dil_combine_proj_kernel,
        out_shape=jax.ShapeDtypeStruct((n, d), F32),
        grid=(n // tm,),
        in_specs=[row] * 6 + [pl.BlockSpec((d, d), lambda i: (0, 0)), row],
        out_specs=row,
        compiler_params=_cparams(("parallel",)),
        name="dilated_combine_proj",
    )(*outs, *lses, w, h)


def _ffn_kernel(h_ref, g_ref, wg_ref, wu_ref, wd_ref, o_ref, xn_ref, acc_ref):
    f = pl.program_id(1)

    @pl.when(f == 0)
    def _():
        xn_ref[...] = _rms(h_ref[...], g_ref[...]).astype(BF16)
        acc_ref[...] = jnp.zeros_like(acc_ref)

    xn = xn_ref[...]
    gate = _dot(xn, wg_ref[...])
    up = _dot(xn, wu_ref[...])
    act = (gate * _sigmoid(gate) * up).astype(BF16)
    acc_ref[...] += _dot(act, wd_ref[...])

    @pl.when(f == pl.num_programs(1) - 1)
    def _():
        o_ref[...] = h_ref[...] + acc_ref[...]


def _ffn(h, g, wg, wu, wd, tm=512, n_f=2):
    n, d = h.shape
    dff = wg.shape[1]
    tf = dff // n_f
    return pl.pallas_call(
        _ffn_kernel,
        out_shape=jax.ShapeDtypeStruct((n, d), F32),
        grid=(n // tm, n_f),
        in_specs=[pl.BlockSpec((tm, d), lambda i, f: (i, 0)),
                  pl.BlockSpec((1, d), lambda i, f: (0, 0)),
                  pl.BlockSpec((d, tf), lambda i, f: (0, f)),
                  pl.BlockSpec((d, tf), lambda i, f: (0, f)),
                  pl.BlockSpec((tf, d), lambda i, f: (f, 0))],
        out_specs=pl.BlockSpec((tm, d), lambda i, f: (i, 0)),
        scratch_shapes=[pltpu.VMEM((tm, d), BF16), pltpu.VMEM((tm, d), F32)],
        compiler_params=_cparams(("parallel", "arbitrary")),
        name="ffn",
    )(h, g.reshape(1, d), wg, wu, wd)


def _moe_kernel(h_ref, g_ref, rh_ref, rl_ref, wg_ref, wu_ref, wd_ref, fg_ref, o_ref,
                xn_ref, comb_ref, acc_ref, *, final_norm):
    e = pl.program_id(1)
    col = lax.broadcasted_iota(jnp.int32, comb_ref.shape, 1)

    @pl.when(e == 0)
    def _():
        xn = _rms(h_ref[...], g_ref[...])
        xh, xl = _split2(xn)
        xn_ref[...] = xh
        logits = _dot(xh, rh_ref[...]) + _dot(xh, rl_ref[...]) + _dot(xl, rh_ref[...])
        colf = col.astype(F32)
        lg = jnp.where(col < N_EXPERTS, logits, NEG)
        m1 = jnp.max(lg, axis=-1, keepdims=True)
        i1 = jnp.min(jnp.where(lg == m1, colf, float(LANES)), axis=-1, keepdims=True)
        lg2 = jnp.where(colf == i1, NEG, lg)
        m2 = jnp.max(lg2, axis=-1, keepdims=True)
        i2 = jnp.min(jnp.where(lg2 == m2, colf, float(LANES)), axis=-1, keepdims=True)
        e2 = jnp.exp(m2 - m1)
        g1 = 1.0 / (1.0 + e2)
        g2 = e2 / (1.0 + e2)
        comb_ref[...] = jnp.where(colf == i1, g1, 0.0) + jnp.where(colf == i2, g2, 0.0)
        acc_ref[...] = jnp.zeros_like(acc_ref)

    xn = xn_ref[...]
    gate = _dot(xn, wg_ref[0])
    up = _dot(xn, wu_ref[0])
    ce = jnp.sum(jnp.where(col == e, comb_ref[...], 0.0), axis=-1, keepdims=True)
    act = (gate * _sigmoid(gate) * up * ce).astype(BF16)
    acc_ref[...] += _dot(act, wd_ref[0])

    @pl.when(e == pl.num_programs(1) - 1)
    def _():
        y = h_ref[...] + acc_ref[...]
        if final_norm:
            y = _rms(y, fg_ref[...])
        o_ref[...] = y


def _moe(h, g, router, wg, wu, wd, final_gain, final_norm, tm=512):
    n, d = h.shape
    ne, _, dfe = wg.shape
    rpad = jnp.zeros((d, LANES), F32).at[:, :ne].set(router)
    rh, rl = _split2(rpad)
    return pl.pallas_call(
        functools.partial(_moe_kernel, final_norm=final_norm),
        out_shape=jax.ShapeDtypeStruct((n, d), F32),
        grid=(n // tm, ne),
        in_specs=[pl.BlockSpec((tm, d), lambda i, e: (i, 0)),
                  pl.BlockSpec((1, d), lambda i, e: (0, 0)),
                  pl.BlockSpec((d, LANES), lambda i, e: (0, 0)),
                  pl.BlockSpec((d, LANES), lambda i, e: (0, 0)),
                  pl.BlockSpec((1, d, dfe), lambda i, e: (e, 0, 0)),
                  pl.BlockSpec((1, d, dfe), lambda i, e: (e, 0, 0)),
                  pl.BlockSpec((1, dfe, d), lambda i, e: (e, 0, 0)),
                  pl.BlockSpec((1, d), lambda i, e: (0, 0))],
        out_specs=pl.BlockSpec((tm, d), lambda i, e: (i, 0)),
        scratch_shapes=[pltpu.VMEM((tm, d), BF16), pltpu.VMEM((tm, LANES), F32),
                        pltpu.VMEM((tm, d), F32)],
        compiler_params=_cparams(("parallel", "arbitrary")),
        name="moe",
    )(h, g.reshape(1, d), rh, rl, wg, wu, wd, final_gain.reshape(1, d))


def _with_ones(vt):
    return jnp.concatenate([vt, jnp.ones((DEN_ROWS, vt.shape[1]), vt.dtype)], axis=0)


def _flash_step(s, vt1, m_ref, acc_ref, idx):
    m_old = m_ref[idx]
    m_new = jnp.maximum(m_old, jnp.max(s, axis=0, keepdims=True))
    p = jnp.exp2(s - m_new).astype(BF16)
    acc_ref[idx] = jnp.exp2(m_old - m_new) * acc_ref[idx] + _dot(vt1, p)
    m_ref[idx] = m_new


def _flash_result(acc_ref, idx):
    acc = acc_ref[idx]
    nf = acc.shape[0] - DEN_ROWS
    return acc[:nf] / acc[nf:nf + 1]


def _sweep_tiles(lo, hi, group, body):
    n_tiles = jnp.maximum(hi - lo, 0)
    full = lax.shift_right_logical(n_tiles, int(math.log2(group)))

    def step(i, carry):
        body(lo + i * group, group)
        return carry

    lax.fori_loop(0, full, step, 0)
    base = lo + full * group
    rem = n_tiles - full * group
    p = group // 2
    while p >= 1:
        take = (rem & p) != 0

        @pl.when(take)
        def _(base=base, p=p):
            body(base, p)

        base = base + jnp.where(take, p, 0)
        p //= 2


def _flash_init(m_ref, acc_ref):
    m_ref[...] = jnp.full(m_ref.shape, NEG, F32)
    acc_ref[...] = jnp.zeros_like(acc_ref)


def _diff_attn_kernel(lam_ref, q_ref, k_ref, vt_ref, bias_ref, g_ref, o_ref, m_ref, acc_ref, sa_ref, sb_ref,
                      *, t, lam_init):
    qi = pl.program_id(2)
    qt = _flip(_eye(LANES), q_ref[...]).astype(BF16)
    rhs = jnp.concatenate([_pad_rows(qt[:HEAD_DIM], True), _pad_rows(qt[HEAD_DIM:], False)], axis=1)
    _flash_init(m_ref, acc_ref)
    gt = ATT_GROUP * t
    n_groups = lax.shift_right_logical(qi + ATT_GROUP, int(math.log2(ATT_GROUP)))

    def key_start(gi):
        return pl.multiple_of(jnp.minimum(gi, n_groups - 1) * gt, gt)

    def scores(gi, s_ref):
        s_ref[...] = _dot(k_ref[pl.ds(key_start(gi), gt), :], rhs)

    def softmax(gi, s_ref):
        vt1 = _with_ones(vt_ref[:, pl.ds(key_start(gi), gt)])
        for mi in range(2):
            bias = jnp.concatenate(
                [bias_ref[mi, _bias_tile_index(qi - (gi * ATT_GROUP + u))] for u in range(ATT_GROUP)], axis=0)
            _flash_step(s_ref[:, mi * t:(mi + 1) * t] + bias, vt1, m_ref, acc_ref, mi)

    scores(0, sa_ref)

    def pair(k, carry):
        scores(2 * k + 1, sb_ref)
        softmax(2 * k, sa_ref)
        scores(2 * k + 2, sa_ref)
        softmax(2 * k + 1, sb_ref)
        return carry

    lax.fori_loop(0, lax.shift_right_logical(n_groups + 1, 1), pair, 0)

    lam = lam_ref[...]
    lam_full = (jnp.exp(jnp.sum(lam[0:1] * lam[1:2], axis=-1, keepdims=True))
                - jnp.exp(jnp.sum(lam[2:3] * lam[3:4], axis=-1, keepdims=True)) + lam_init)
    o = _flash_result(acc_ref, 0) - lam_full * _flash_result(acc_ref, 1)
    ms = jnp.mean(o * o, axis=0, keepdims=True)
    y = (o * lax.rsqrt(ms + EPS) * g_ref[...] * (1.0 - lam_init)).astype(BF16)
    o_ref[...] = _flip(_eye(t), y).astype(o_ref.dtype)


def _diff_attention(z, zt, lam, subln, bias_tiles, lam_init):
    b, s, _ = z.shape
    t = ATT_T
    nh = N_HEADS // 2
    n_tiles = bias_tiles.shape[1]
    return pl.pallas_call(
        functools.partial(_diff_attn_kernel, t=t, lam_init=lam_init),
        out_shape=jax.ShapeDtypeStruct((b, s, D_MODEL), BF16),
        grid=(b, nh, s // t),
        in_specs=[pl.BlockSpec((4, HEAD_DIM), lambda bi, h, qi: (0, 0)),
                  pl.BlockSpec((None, t, LANES), lambda bi, h, qi: (bi, qi, h)),
                  _resident((None, s, LANES), lambda bi, h, qi: (bi, 0, nh + h)),
                  _resident((LANES, s), lambda bi, h, qi: (h, bi)),
                  _resident((2, n_tiles, t, t), lambda bi, h, qi: (h, 0, 0, 0)),
                  pl.BlockSpec((LANES, 1), lambda bi, h, qi: (0, 0))],
        out_specs=pl.BlockSpec((None, t, LANES), lambda bi, h, qi: (bi, qi, h)),
        scratch_shapes=[pltpu.VMEM((2, 1, t), F32), pltpu.VMEM((2, LANES + DEN_ROWS, t), F32),
                        pltpu.VMEM((ATT_GROUP * t, 2 * t), F32), pltpu.VMEM((ATT_GROUP * t, 2 * t), F32)],
        compiler_params=_cparams(("parallel", "parallel", "arbitrary")),
        name="diff_attention",
    )(lam, z, z, zt, bias_tiles, subln.reshape(LANES, 1))


def _dilated_kernel(q_ref, kp_ref, kc_ref, vp_ref, vc_ref, bias_ref, o_ref, lse_ref):
    n = pl.program_id(2)
    variant = jnp.minimum(n, 1)
    w = q_ref.shape[0]
    eye = _eye(LANES)
    eye_w = _eye(w)

    def pair(hp, carry):
        cols = pl.ds(pl.multiple_of(hp * LANES, LANES), LANES)
        qt = _flip(eye, q_ref[:, cols]).astype(BF16)
        kcat = jnp.concatenate([kp_ref[:, cols], kc_ref[:, cols]], axis=0)
        vcat = jnp.concatenate([vp_ref[:, cols], vc_ref[:, cols]], axis=0)
        vt = _flip(eye, vcat).astype(BF16)
        rhs = jnp.concatenate([_pad_rows(qt[:HEAD_DIM], True), _pad_rows(qt[HEAD_DIM:], False)], axis=1)
        s2 = _dot(kcat, rhs)
        outs, lses = [], []
        for sub in range(2):
            s = s2[:, sub * w:(sub + 1) * w] + bias_ref[2 * hp + sub, variant]
            m = jnp.max(s, axis=0, keepdims=True)
            e = jnp.exp(s - m)
            den = jnp.sum(e, axis=0, keepdims=True)
            p = (e * (1.0 / den)).astype(BF16)
            outs.append(_dot(vt[sub * HEAD_DIM:(sub + 1) * HEAD_DIM], p))
            lses.append(jnp.broadcast_to(m + jnp.log(den), (HEAD_DIM, w)))
        o_ref[:, cols] = _flip(eye_w, jnp.concatenate(outs, axis=0).astype(BF16)).astype(o_ref.dtype)
        lse_ref[:, cols] = sum(_flip(eye_w, term) for term in _split3(jnp.concatenate(lses, axis=0)))
        return carry

    lax.fori_loop(0, N_HEADS // 2, pair, 0)


def _dilated_group(z, bias, dilation):
    b, s, c3 = z.shape
    w = DIL_W
    m_len = s // dilation
    nb = m_len // w
    zv = z.reshape(b, m_len, dilation * c3)
    blk = lambda sect, prev: pl.BlockSpec(
        (None, w, D_MODEL),
        (lambda bi, r, n: (bi, jnp.maximum(n - 1, 0), 3 * r + sect)) if prev
        else (lambda bi, r, n: (bi, n, 3 * r + sect)))
    out_blk = pl.BlockSpec((None, w, D_MODEL), lambda bi, r, n: (bi, n, r))
    o, lse = pl.pallas_call(
        _dilated_kernel,
        out_shape=(jax.ShapeDtypeStruct((b, m_len, dilation * D_MODEL), BF16),
                   jax.ShapeDtypeStruct((b, m_len, dilation * D_MODEL), F32)),
        grid=(b, dilation, nb),
        in_specs=[blk(0, False), blk(1, True), blk(1, False), blk(2, True), blk(2, False),
                  pl.BlockSpec((N_HEADS, 2, 2 * w, w), lambda bi, r, n: (0, 0, 0, 0))],
        out_specs=(out_blk, out_blk),
        compiler_params=_cparams(("parallel", "parallel", "arbitrary")),
        name=f"dilated_d{dilation}",
    )(zv, zv, zv, zv, zv, bias)
    return o.reshape(b * s, D_MODEL), lse.reshape(b * s, D_MODEL)


def _stick_kernel(q_ref, k_ref, vt_ref, o_ref, acc_ref, carry_ref, *, t):
    qi = pl.program_id(2)
    qt = _flip(_eye(LANES), q_ref[...]).astype(BF16)
    row = lax.broadcasted_iota(jnp.int32, (t, t), 0)
    col = lax.broadcasted_iota(jnp.int32, (t, t), 1)
    after = jnp.where(col > row, 1.0, 0.0).astype(BF16)
    before = row < col
    outs = []
    for sub in range(2):
        rhs = _pad_rows(qt[sub * HEAD_DIM:(sub + 1) * HEAD_DIM], sub == 0)
        acc_ref[...] = jnp.zeros_like(acc_ref)
        carry_ref[...] = jnp.zeros_like(carry_ref)

        def block(j, masked, rhs=rhs, sub=sub):
            start = pl.multiple_of(j * t, t)
            kb = k_ref[pl.ds(start, t), :]
            vt = vt_ref[sub * HEAD_DIM:(sub + 1) * HEAD_DIM, pl.ds(start, t)]
            z = _dot(kb, rhs)
            log_keep = -(jnp.maximum(z, 0.0) + jnp.log(1.0 + jnp.exp(-jnp.abs(z))))
            if masked:
                log_keep = jnp.where(before, log_keep, 0.0)
            hi, lo = _split2(log_keep)
            later = _dot(after, hi) + _dot(after, lo) + carry_ref[...]
            a = jnp.exp(z + log_keep + later)
            if masked:
                a = jnp.where(before, a, 0.0)
            acc_ref[...] += _dot(vt, a.astype(BF16))
            carry_ref[...] += jnp.sum(log_keep, axis=0, keepdims=True)

        block(qi, True)

        def cond(c):
            return jnp.logical_and(c[0] >= 0, c[1] >= SB_CUTOFF)

        def body(c, block=block):
            block(c[0], False)
            return c[0] - 1, jnp.max(carry_ref[...])

        lax.while_loop(cond, body, (qi - 1, jnp.max(carry_ref[...])))
        outs.append(acc_ref[...])
    o_ref[...] = _flip(_eye(t), jnp.concatenate(outs, axis=0).astype(BF16)).astype(o_ref.dtype)


def _stick_attention(z, zt):
    b, s, _ = z.shape
    t = SB_T
    nh = N_HEADS // 2
    return pl.pallas_call(
        functools.partial(_stick_kernel, t=t),
        out_shape=jax.ShapeDtypeStruct((b, s, D_MODEL), BF16),
        grid=(b, nh, s // t),
        in_specs=[pl.BlockSpec((None, t, LANES), lambda bi, h, qi: (bi, qi, h)),
                  pl.BlockSpec((None, s, LANES), lambda bi, h, qi: (bi, 0, nh + h)),
                  pl.BlockSpec((LANES, s), lambda bi, h, qi: (h, bi))],
        out_specs=pl.BlockSpec((None, t, LANES), lambda bi, h, qi: (bi, qi, h)),
        scratch_shapes=[pltpu.VMEM((HEAD_DIM, t), F32), pltpu.VMEM((1, t), F32)],
        compiler_params=_cparams(("parallel", "parallel", "arbitrary")),
        name="stick_breaking",
    )(z, z, zt)


def _compress_kernel(a_ref, as_ref, w1_ref, pos_ref, w2_ref, w2t_ref, o_ref, ot_ref):
    half = a_ref.shape[1]
    w1 = w1_ref[...]
    const = _dot(jnp.broadcast_to(pos_ref[...], (8, 2 * half)), w1)[0:1]
    pre = _dot(a_ref[...], w1[:half]) + _dot(as_ref[...], w1[half:]) + const
    hid = 0.5 * pre * (1.0 + jnp.tanh(math.sqrt(2.0 / math.pi) * (pre + 0.044715 * pre * pre * pre)))
    hid = hid.astype(BF16)
    o_ref[...] = _dot(hid, w2_ref[...]).astype(o_ref.dtype)
    ot_ref[...] = _nt_dot(w2t_ref[...], hid).astype(ot_ref.dtype)


def _compress(blocks, blocks_next, w1, pos, w2, tr=256):
    _, nkv, b, r, half = blocks.shape
    tr = min(tr, r)
    w2pad = jnp.concatenate([w2, jnp.zeros_like(w2)], axis=-1)
    w2t = jnp.swapaxes(w2, 1, 2)
    spec = pl.BlockSpec((None, None, None, tr, half), lambda j, hk, bi, ri: (j, hk, bi, ri, 0))
    return pl.pallas_call(
        _compress_kernel,
        out_shape=(jax.ShapeDtypeStruct((2, nkv, b, r, LANES), BF16),
                   jax.ShapeDtypeStruct((2, nkv, b, HEAD_DIM, r), BF16)),
        grid=(2, nkv, b, r // tr),
        in_specs=[spec, spec,
                  pl.BlockSpec((None, 2 * half, NSA_CMP_HID), lambda j, hk, bi, ri: (j, 0, 0)),
                  pl.BlockSpec((None, 1, 2 * half), lambda j, hk, bi, ri: (j, 0, 0)),
                  pl.BlockSpec((None, NSA_CMP_HID, LANES), lambda j, hk, bi, ri: (j, 0, 0)),
                  pl.BlockSpec((None, HEAD_DIM, NSA_CMP_HID), lambda j, hk, bi, ri: (j, 0, 0))],
        out_specs=(pl.BlockSpec((None, None, None, tr, LANES), lambda j, hk, bi, ri: (j, hk, bi, ri, 0)),
                   pl.BlockSpec((None, None, None, HEAD_DIM, tr), lambda j, hk, bi, ri: (j, hk, bi, 0, ri))),
        compiler_params=_cparams(("parallel", "parallel", "parallel", "arbitrary")),
        name="nsa_compress",
    )(blocks, blocks_next, w1, pos, w2pad, w2t)


def _nsa_select_kernel(q_ref, kc_ref, vct_ref, cmat_ref, o_ref, sel_ref, *, t, n_sel):
    qi = pl.program_id(2)
    kc = kc_ref[...]
    vct = vct_ref[...]
    r = kc.shape[0]
    n_slc = sel_ref.shape[0]
    qt = _flip(_eye(NSA_GROUP * HEAD_DIM), q_ref[...]).astype(BF16)
    tpos = qi * t + lax.broadcasted_iota(jnp.int32, (1, t), 1)
    cidx = lax.broadcasted_iota(jnp.int32, (r, t), 0)
    seen = NSA_CMP_STRIDE * cidx + (NSA_CMP_LEN - 1) <= tpos
    live = (tpos >= NSA_CMP_LEN - 1).astype(F32)
    imp = jnp.zeros((r, t), F32)
    for g in range(NSA_GROUP):
        rhs = _pad_rows(qt[g * HEAD_DIM:(g + 1) * HEAD_DIM], True)
        sc = jnp.where(seen, _dot(kc, rhs), NEG)
        e = jnp.exp2(sc - jnp.max(sc, axis=0, keepdims=True))
        pc = e * (live / jnp.sum(e, axis=0, keepdims=True))
        o_ref[g * HEAD_DIM:(g + 1) * HEAD_DIM, :] = _dot(vct, pc.astype(BF16)).astype(o_ref.dtype)
        imp = imp + pc
    cmat = cmat_ref[...]
    imp_slc = sum(_dot(cmat, term) for term in _split3(imp))
    blk = lax.broadcasted_iota(jnp.int32, (n_slc, t), 0)
    cur = tpos // NSA_SLC_LEN
    forced = (blk == 0) | (blk == cur) | (blk == cur - 1)
    valid = blk * NSA_SLC_LEN <= tpos
    score = jnp.where(valid, imp_slc + jnp.where(forced, FORCE_BONUS, 0.0), -1.0)
    sel = jnp.zeros((n_slc, t), F32)
    blkf = blk.astype(F32)
    for _ in range(n_sel):
        top = jnp.max(score, axis=0, keepdims=True)
        first = jnp.min(jnp.where(score == top, blkf, float(n_slc)), axis=0, keepdims=True)
        hit = blkf == first
        sel = jnp.where(hit, 1.0, sel)
        score = jnp.where(hit, -jnp.inf, score)
    sel_ref[...] = jnp.where(valid, sel, 0.0).astype(sel_ref.dtype)


def _nsa_select(z, kc, vct, cmat, n_sel):
    b, s, _ = z.shape
    t = ATT_T
    r = kc.shape[2]
    n_slc = cmat.shape[0]
    gw = NSA_GROUP * HEAD_DIM
    return pl.pallas_call(
        functools.partial(_nsa_select_kernel, t=t, n_sel=n_sel),
        out_shape=(jax.ShapeDtypeStruct((b, NSA_KV, gw, s), BF16),
                   jax.ShapeDtypeStruct((b, NSA_KV, n_slc, s), BF16)),
        grid=(b, NSA_KV, s // t),
        in_specs=[pl.BlockSpec((None, t, gw), lambda bi, hk, qi: (bi, qi, hk)),
                  pl.BlockSpec((None, None, r, LANES), lambda bi, hk, qi: (hk, bi, 0, 0)),
                  pl.BlockSpec((None, None, HEAD_DIM, r), lambda bi, hk, qi: (hk, bi, 0, 0)),
                  pl.BlockSpec((n_slc, r), lambda bi, hk, qi: (0, 0))],
        out_specs=(pl.BlockSpec((None, None, gw, t), lambda bi, hk, qi: (bi, hk, 0, qi)),
                   pl.BlockSpec((None, None, n_slc, t), lambda bi, hk, qi: (bi, hk, 0, qi))),
        compiler_params=_cparams(("parallel", "parallel", "arbitrary")),
        name="nsa_select",
    )(z, kc, vct, cmat)


def _nsa_attend_kernel(q_ref, ks_ref, kw_ref, vst_ref, vwt_ref, sel_ref, ocmp_ref, zg_ref, bs_ref, bw_ref,
                       o_ref, m_ref, acc_ref, res_ref, gate_ref, sel32_ref, sa_ref, sb_ref, *, t, n_win):
    hk = pl.program_id(1)
    qi = pl.program_id(2)
    per_tile = t // NSA_SLC_LEN
    gw = NSA_GROUP * HEAD_DIM
    qt = _flip(_eye(gw), q_ref[...]).astype(BF16)
    low_half = (hk % 2) == 0
    rhs = []
    for g in range(NSA_GROUP):
        qg = qt[g * HEAD_DIM:(g + 1) * HEAD_DIM]
        rhs.append(jnp.where(low_half, _pad_rows(qg, True), _pad_rows(qg, False)))
    rhs = jnp.concatenate(rhs, axis=1)
    sel32_ref[...] = sel_ref[...].astype(F32)

    _flash_init(m_ref, acc_ref)
    grp = NSA_SWEEP_GROUP
    gt = grp * t
    n_groups = lax.shift_right_logical(qi + grp, int(math.log2(grp)))

    def first_tile(gi):
        return jnp.minimum(gi, n_groups - 1) * grp

    def scores(gi, s_ref):
        start = pl.multiple_of(first_tile(gi) * t, t)
        s_ref[...] = _dot(ks_ref[pl.ds(start, gt), :], rhs)

    def softmax(gi, s_ref):
        j0 = first_tile(gi)
        vt1 = _with_ones(vst_ref[:, pl.ds(pl.multiple_of(j0 * t, t), gt)])
        keep = jnp.concatenate(
            [jnp.broadcast_to(sel32_ref[pl.ds(j0 * per_tile + u, 1), :], (NSA_SLC_LEN, t))
             for u in range(grp * per_tile)], axis=0) > 0.5
        for g in range(NSA_GROUP):
            bias = jnp.concatenate(
                [bs_ref[g, _bias_tile_index(qi - (gi * grp + u))] for u in range(grp)], axis=0)
            s = jnp.where(keep, s_ref[:, g * t:(g + 1) * t] + bias, NEG)
            _flash_step(s, vt1, m_ref, acc_ref, g)

    scores(0, sa_ref)

    def pair(k, carry):
        scores(2 * k + 1, sb_ref)
        softmax(2 * k, sa_ref)
        scores(2 * k + 2, sa_ref)
        softmax(2 * k + 1, sb_ref)
        return carry

    lax.fori_loop(0, lax.shift_right_logical(n_groups + 1, 1), pair, 0)
    for g in range(NSA_GROUP):
        res_ref[g] = _flash_result(acc_ref, g)

    _flash_init(m_ref, acc_ref)

    def win_tiles(count):
        start = pl.multiple_of((qi - (count - 1)) * t, t)
        s4 = _dot(kw_ref[pl.ds(start, count * t), :], rhs)
        vt1 = _with_ones(vwt_ref[:, pl.ds(start, count * t)])
        for g in range(NSA_GROUP):
            bias = jnp.concatenate([bw_ref[g, count - 1 - u] for u in range(count)], axis=0)
            _flash_step(s4[:, g * t:(g + 1) * t] + bias, vt1, m_ref, acc_ref, g)

    for count in range(1, n_win + 1):
        @pl.when((qi == count - 1) if count < n_win else (qi >= count - 1))
        def _(count=count):
            win_tiles(count)

    gate_ref[...] = _sigmoid(_flip(_eye(LANES), zg_ref[...]))
    mixed = []
    for g in range(NSA_GROUP):
        base = 3 * (NSA_GROUP * hk + g)
        gates = [gate_ref[pl.ds(base + c, 1), :] for c in range(3)]
        o_cmp = ocmp_ref[g * HEAD_DIM:(g + 1) * HEAD_DIM, :].astype(F32)
        mixed.append(gates[0] * o_cmp + gates[1] * res_ref[g] + gates[2] * _flash_result(acc_ref, g))
    o_ref[...] = _flip(_eye(t), jnp.concatenate(mixed, axis=0).astype(BF16)).astype(o_ref.dtype)


def _nsa_attend(z, zt, sel, o_cmp, bias_slc, bias_win, cols):
    b, s, _ = z.shape
    t = ATT_T
    n_slc = sel.shape[2]
    gw = NSA_GROUP * HEAD_DIM
    n_win = bias_win.shape[1]
    n_tiles = bias_slc.shape[1]
    c_slc, c_win, c_gate = cols["k_slc"], cols["k_win"], cols["gate"]
    return pl.pallas_call(
        functools.partial(_nsa_attend_kernel, t=t, n_win=n_win),
        out_shape=jax.ShapeDtypeStruct((b, s, D_MODEL), BF16),
        grid=(b, NSA_KV, s // t),
        in_specs=[pl.BlockSpec((None, t, gw), lambda bi, hk, qi: (bi, qi, hk)),
                  _resident((None, s, LANES), lambda bi, hk, qi: (bi, 0, c_slc + hk // 2)),
                  _resident((None, s, LANES), lambda bi, hk, qi: (bi, 0, c_win + hk // 2)),
                  _resident((HEAD_DIM, s), lambda bi, hk, qi: (hk, bi)),
                  _resident((HEAD_DIM, s), lambda bi, hk, qi: (NSA_KV + hk, bi)),
                  pl.BlockSpec((None, None, n_slc, t), lambda bi, hk, qi: (bi, hk, 0, qi)),
                  pl.BlockSpec((None, None, gw, t), lambda bi, hk, qi: (bi, hk, 0, qi)),
                  pl.BlockSpec((None, t, LANES), lambda bi, hk, qi: (bi, qi, c_gate)),
                  _resident((NSA_GROUP, n_tiles, t, t), lambda bi, hk, qi: (hk, 0, 0, 0)),
                  _resident((NSA_GROUP, n_win, t, t), lambda bi, hk, qi: (hk, 0, 0, 0))],
        out_specs=pl.BlockSpec((None, t, gw), lambda bi, hk, qi: (bi, qi, hk)),
        scratch_shapes=[pltpu.VMEM((NSA_GROUP, 1, t), F32),
                        pltpu.VMEM((NSA_GROUP, HEAD_DIM + DEN_ROWS, t), F32),
                        pltpu.VMEM((NSA_GROUP, HEAD_DIM, t), F32),
                        pltpu.VMEM((LANES, t), F32), pltpu.VMEM((n_slc, t), F32),
                        pltpu.VMEM((NSA_SWEEP_GROUP * t, NSA_GROUP * t), F32),
                        pltpu.VMEM((NSA_SWEEP_GROUP * t, NSA_GROUP * t), F32)],
        compiler_params=_cparams(("parallel", "parallel", "arbitrary")),
        name="nsa_attend",
    )(z, z, z, zt, zt, sel, o_cmp, z, bias_slc, bias_win)


def _nsa_layout(w_in):
    d = w_in.shape[0]
    nq = N_HEADS * HEAD_DIM
    kvw = NSA_KV * HEAD_DIM
    kv = [w_in[:, nq + j * kvw: nq + (j + 1) * kvw] for j in range(6)]
    wg = w_in[:, nq + 6 * kvw:]
    wg = jnp.concatenate([wg, jnp.zeros((d, LANES - wg.shape[1]), w_in.dtype)], axis=-1)
    w = jnp.concatenate([w_in[:, :nq] * (SCALE * LOG2E), kv[2], kv[4], kv[0], kv[1], wg], axis=-1)
    wt = jnp.concatenate([kv[3], kv[5]], axis=-1).T
    cols = {"k_slc": nq // LANES, "k_win": (nq + kvw) // LANES, "k_cmp": nq + 2 * kvw,
            "v_cmp": nq + 3 * kvw, "gate": (nq + 4 * kvw) // LANES}
    return w.astype(BF16), wt.astype(BF16), cols


def _importance_matrix_np(n_slc, r):
    ratio = NSA_SLC_LEN // NSA_CMP_STRIDE
    span = NSA_CMP_LEN // NSA_CMP_STRIDE
    coef = np.convolve(np.ones(ratio), np.ones(span))
    cmat = np.zeros((n_slc, r), np.float32)
    for j in range(n_slc):
        for o, cf in enumerate(coef):
            c = ratio * j - o
            if c >= 0:
                cmat[j, c] = cf
    return cmat


def _nsa_mixer(h, g, b, s, w_in, cmp_pos, cmp_w1, cmp_w2, tab, bias_causal):
    w, wt, cols = _nsa_layout(w_in)
    z, zt = _norm_matmul(h, g, w, wt)
    z = z.reshape(b, s, -1)
    kvw = NSA_KV * HEAD_DIM
    r = s // NSA_CMP_STRIDE

    def rows_of_16(x):
        x = x.reshape(b, -1, NSA_KV, HEAD_DIM).transpose(2, 0, 1, 3)
        return x.reshape(NSA_KV, b, -1, NSA_CMP_STRIDE * HEAD_DIM)

    kv_c = jnp.stack([z[..., cols["k_cmp"]:cols["k_cmp"] + kvw], z[..., cols["v_cmp"]:cols["v_cmp"] + kvw]])
    nxt = jnp.concatenate([kv_c[:, :, NSA_CMP_STRIDE:], jnp.zeros_like(kv_c[:, :, :NSA_CMP_STRIDE])], axis=2)
    blocks = jnp.stack([rows_of_16(kv_c[0]), rows_of_16(kv_c[1])])
    blocks_next = jnp.stack([rows_of_16(nxt[0]), rows_of_16(nxt[1])])
    flat = NSA_CMP_LEN * HEAD_DIM
    cmp_tok, cmp_feat = _compress(blocks, blocks_next, cmp_w1.astype(BF16),
                                  cmp_pos.reshape(2, 1, flat).astype(BF16), cmp_w2.astype(BF16))

    n_slc = s // NSA_SLC_LEN
    cmat = jnp.asarray(_importance_matrix_np(n_slc, r), BF16)
    o_cmp, sel = _nsa_select(z, cmp_tok[0], cmp_feat[1], cmat, min(NSA_TOP_N, n_slc))
    bias_win = _build_bias_tiles(tab, _causal_bucket_tiles(ATT_T, -(-NSA_WINDOW // ATT_T) + 1, NSA_WINDOW))
    return _nsa_attend(z, zt, sel, o_cmp, bias_causal, bias_win, cols)


def _scaled_qkv(w_in, scale=SCALE):
    return jnp.concatenate([w_in[:, :D_MODEL] * scale, w_in[:, D_MODEL:]], axis=1).astype(BF16)


def kernel(x, rel_bias, norm_gains, final_gain, a_w_in, a_lambda, a_subln, a_w_out, b_w_in, b_w_out,
           c_w_in, c_w_out, d_w_in, d_cmp_pos, d_cmp_w1, d_cmp_w2, d_w_out, ffn_w_gate, ffn_w_up,
           ffn_w_down, moe_router, moe_w_gate, moe_w_up, moe_w_down):
    b, s, d = x.shape
    assert d == D_MODEL and s % (B_PAIRS[-1][0]) == 0 and s >= (N_NEAR + 1) * ATT_T
    assert _t5_bucket_np(np.array([(N_NEAR - 1) * ATT_T + 1]))[0] == NUM_BUCKETS - 1
    n = b * s
    h = x.reshape(n, d)
    tab = rel_bias.T.astype(F32)
    tab_rel = (tab - tab[:, NUM_BUCKETS - 1:]) * LOG2E
    bias_causal = _build_bias_tiles(tab_rel, _sweep_bucket_tiles(ATT_T))

    w = _scaled_qkv(a_w_in[0], SCALE * LOG2E)
    z, zt = _norm_matmul(h, norm_gains[0, 0], w[:, :2 * d], w[:, 2 * d:].T)
    lam_init = 0.8 - 0.6 * math.exp(-0.3 * 0)
    o = _diff_attention(z.reshape(b, s, -1), zt, a_lambda[0].astype(F32), a_subln[0].astype(F32),
                        bias_causal, lam_init)
    h = _proj_residual(o.reshape(n, d), a_w_out[0].astype(BF16), h)
    h = _ffn(h, norm_gains[0, 1], ffn_w_gate[0].astype(BF16), ffn_w_up[0].astype(BF16),
             ffn_w_down[0].astype(BF16))

    z = _norm_matmul(h, norm_gains[1, 0], _scaled_qkv(b_w_in[0])).reshape(b, s, -1)
    outs, lses = [], []
    for _, dil in B_PAIRS:
        bias = _build_bias_tiles(tab, _dilated_bucket_tiles(dil))
        o, lse = _dilated_group(z, bias, dil)
        outs.append(o)
        lses.append(lse)
    h = _dil_combine_proj(outs, lses, b_w_out[0].astype(BF16), h)
    h = _moe(h, norm_gains[1, 1], moe_router[0], moe_w_gate[0].astype(BF16), moe_w_up[0].astype(BF16),
             moe_w_down[0].astype(BF16), final_gain, False)

    w = _scaled_qkv(c_w_in[0])
    z, zt = _norm_matmul(h, norm_gains[2, 0], w[:, :2 * d], w[:, 2 * d:].T)
    o = _stick_attention(z.reshape(b, s, -1), zt)
    h = _proj_residual(o.reshape(n, d), c_w_out[0].astype(BF16), h)
    h = _ffn(h, norm_gains[2, 1], ffn_w_gate[1].astype(BF16), ffn_w_up[1].astype(BF16),
             ffn_w_down[1].astype(BF16))

    o = _nsa_mixer(h, norm_gains[3, 0], b, s, d_w_in[0], d_cmp_pos[0], d_cmp_w1[0], d_cmp_w2[0], tab_rel,
                   bias_causal)
    h = _proj_residual(o.reshape(n, d), d_w_out[0].astype(BF16), h)
    h = _moe(h, norm_gains[3, 1], moe_router[1], moe_w_gate[1].astype(BF16), moe_w_up[1].astype(BF16),
             moe_w_down[1].astype(BF16), final_gain, True)
    return h.reshape(b, s, d)
```

```python
import functools
import math

import numpy as np
import jax
import jax.numpy as jnp
from jax import lax
from jax.experimental import pallas as pl
from jax.experimental.pallas import tpu as pltpu

F32 = jnp.float32
BF16 = jnp.bfloat16

D_MODEL = 1024
HEAD_DIM = 64
N_HEADS = 16
LANES = 128
B_PAIRS = ((128, 1), (512, 4), (2048, 16))
DIL_W = 128
NSA_KV = 4
NSA_GROUP = 4
NSA_CMP_LEN = 32
NSA_CMP_STRIDE = 16
NSA_CMP_HID = 256
NSA_SLC_LEN = 64
NSA_TOP_N = 16
NSA_WINDOW = 512
NUM_BUCKETS = 32
MAX_EXACT = 16
REL_MAX_DIST = 2048
N_EXPERTS = 8
EPS = 1e-6
SCALE = HEAD_DIM ** -0.5
NEG = -1e30
MASKED_BUCKET = NUM_BUCKETS

ATT_T = 256
N_NEAR = 7
ATT_GROUP = 2
NSA_SWEEP_GROUP = 2
DEN_ROWS = 16
LOG2E = math.log2(math.e)
SB_T = 256
SB_CUTOFF = -104.0
VMEM_LIMIT = 56 << 20


def _cparams(sem):
    return pltpu.CompilerParams(dimension_semantics=sem, vmem_limit_bytes=VMEM_LIMIT)


def _resident(block_shape, index_map):
    return pl.BlockSpec(block_shape, index_map, pipeline_mode=pl.Buffered(1))


def _nt_dot(a, b):
    return lax.dot_general(a, b, (((1,), (1,)), ((), ())), preferred_element_type=F32)


def _dot(a, b):
    return jnp.dot(a, b, preferred_element_type=F32)


def _split2(x):
    hi = x.astype(BF16)
    lo = (x - hi.astype(F32)).astype(BF16)
    return hi, lo


def _split3(x):
    hi = x.astype(BF16)
    rem = x - hi.astype(F32)
    mid = rem.astype(BF16)
    lo = (rem - mid.astype(F32)).astype(BF16)
    return hi, mid, lo


def _rms(x, g):
    ms = jnp.mean(x * x, axis=-1, keepdims=True)
    return x * lax.rsqrt(ms + EPS) * g


def _sigmoid(x):
    return 1.0 / (1.0 + jnp.exp(-x))


def _eye(n):
    r = lax.broadcasted_iota(jnp.int32, (n, n), 0)
    c = lax.broadcasted_iota(jnp.int32, (n, n), 1)
    return jnp.where(r == c, 1.0, 0.0).astype(BF16)


def _flip(eye, x):
    return _nt_dot(eye, x)


def _pad_rows(x, top):
    zero = jnp.zeros_like(x)
    return jnp.concatenate([x, zero] if top else [zero, x], axis=0)


def _t5_bucket_np(dist):
    n = np.maximum(dist, 0)
    ratio = np.log(np.maximum(n, 1).astype(np.float32) / np.float32(MAX_EXACT)) / np.float32(
        math.log(REL_MAX_DIST / MAX_EXACT))
    large = np.minimum(MAX_EXACT + (ratio * np.float32(NUM_BUCKETS - MAX_EXACT)).astype(np.int32),
                       NUM_BUCKETS - 1)
    return np.where(n < MAX_EXACT, n, large).astype(np.int32)


def _causal_bucket_tiles(t, n_tiles, max_dist=None):
    i = np.arange(t)[None, None, :]
    j = np.arange(t)[None, :, None]
    dist = np.arange(n_tiles)[:, None, None] * t + i - j
    ok = dist >= 0
    if max_dist is not None:
        ok = ok & (dist < max_dist)
    return np.where(ok, _t5_bucket_np(dist), MASKED_BUCKET).astype(np.int32)


def _sweep_bucket_tiles(t):
    far = np.full((1, t, t), NUM_BUCKETS - 1, np.int32)
    future = np.full((1, t, t), MASKED_BUCKET, np.int32)
    return np.concatenate([_causal_bucket_tiles(t, N_NEAR), far, future], axis=0)


def _bias_tile_index(delta):
    return jnp.where(delta < 0, N_NEAR + 1, jnp.minimum(delta, N_NEAR))


def _dilated_bucket_tiles(dilation):
    w = DIL_W
    steps = w + np.arange(w)[None, :] - np.arange(2 * w)[:, None]
    ok = (steps >= 0) & (steps <= w)
    bkt = np.where(ok, _t5_bucket_np(dilation * steps), MASKED_BUCKET)
    first = np.where(np.arange(2 * w)[:, None] < w, MASKED_BUCKET, bkt)
    return np.stack([first, bkt]).astype(np.int32)


def _bias_build_kernel(tab_ref, bkt_ref, o_ref):
    c = pl.program_id(0)
    bkt = bkt_ref[0]
    out = jnp.full(bkt.shape, NEG, F32)
    for b in range(NUM_BUCKETS):
        out = jnp.where(bkt == b, tab_ref[c, b], out)
    o_ref[0, 0] = out


def _build_bias_tiles(tab, buckets):
    n, r, c = buckets.shape
    return pl.pallas_call(
        _bias_build_kernel,
        out_shape=jax.ShapeDtypeStruct((N_HEADS, n, r, c), F32),
        grid=(N_HEADS, n),
        in_specs=[pl.BlockSpec(memory_space=pltpu.SMEM),
                  pl.BlockSpec((1, r, c), lambda h, i: (i, 0, 0))],
        out_specs=pl.BlockSpec((1, 1, r, c), lambda h, i: (h, i, 0, 0)),
        compiler_params=_cparams(("arbitrary", "arbitrary")),
        name="bias_tiles",
    )(tab, jnp.asarray(buckets))


def _norm_matmul_kernel(x_ref, g_ref, w_ref, *refs, chunk, feature_major):
    xn = _rms(x_ref[...], g_ref[...]).astype(BF16)
    o_ref = refs[1] if feature_major else refs[0]
    nout = o_ref.shape[1]
    for c0 in range(0, nout, chunk):
        c1 = min(c0 + chunk, nout)
        o_ref[:, c0:c1] = _dot(xn, w_ref[:, c0:c1]).astype(o_ref.dtype)
    if feature_major:
        wt_ref, ot_ref = refs[0], refs[2]
        nt = ot_ref.shape[0]
        for c0 in range(0, nt, chunk):
            c1 = min(c0 + chunk, nt)
            ot_ref[c0:c1, :] = _nt_dot(wt_ref[c0:c1, :], xn).astype(ot_ref.dtype)


def _norm_matmul(x, g, w, wt=None, tm=512):
    n, d = x.shape
    nout = w.shape[1]
    in_specs = [pl.BlockSpec((tm, d), lambda i: (i, 0)),
                pl.BlockSpec((1, d), lambda i: (0, 0)),
                pl.BlockSpec((d, nout), lambda i: (0, 0))]
    out_shape = [jax.ShapeDtypeStruct((n, nout), BF16)]
    out_specs = [pl.BlockSpec((tm, nout), lambda i: (i, 0))]
    args = [x, g.reshape(1, d), w]
    if wt is not None:
        nt = wt.shape[0]
        in_specs.append(pl.BlockSpec((nt, d), lambda i: (0, 0)))
        out_shape.append(jax.ShapeDtypeStruct((nt, n), BF16))
        out_specs.append(pl.BlockSpec((nt, tm), lambda i: (0, i)))
        args.append(wt)
    out = pl.pallas_call(
        functools.partial(_norm_matmul_kernel, chunk=512, feature_major=wt is not None),
        out_shape=tuple(out_shape),
        grid=(n // tm,),
        in_specs=in_specs,
        out_specs=tuple(out_specs),
        compiler_params=_cparams(("parallel",)),
        name="norm_matmul",
    )(*args)
    return out if wt is not None else out[0]


def _proj_res_kernel(a_ref, w_ref, h_ref, o_ref):
    o_ref[...] = h_ref[...] + _dot(a_ref[...], w_ref[...])


def _proj_residual(a, w, h, tm=512):
    n, d = h.shape
    k = a.shape[1]
    return pl.pallas_call(
        _proj_res_kernel,
        out_shape=jax.ShapeDtypeStruct((n, d), F32),
        grid=(n // tm,),
        in_specs=[pl.BlockSpec((tm, k), lambda i: (i, 0)),
                  pl.BlockSpec((k, d), lambda i: (0, 0)),
                  pl.BlockSpec((tm, d), lambda i: (i, 0))],
        out_specs=pl.BlockSpec((tm, d), lambda i: (i, 0)),
        compiler_params=_cparams(("parallel",)),
        name="proj_residual",
    )(a, w, h)


def _dil_combine_proj_kernel(o1, o2, o3, l1, l2, l3, w_ref, h_ref, o_ref):
    ls = [l1[...], l2[...], l3[...]]
    m = jnp.maximum(jnp.maximum(ls[0], ls[1]), ls[2])
    es = [jnp.exp(l - m) for l in ls]
    den = es[0] + es[1] + es[2]
    a = (es[0] * o1[...].astype(F32) + es[1] * o2[...].astype(F32) + es[2] * o3[...].astype(F32)) / den
    o_ref[...] = h_ref[...] + _dot(a.astype(BF16), w_ref[...])


def _dil_combine_proj(outs, lses, w, h, tm=512):
    n, d = h.shape
    row = pl.BlockSpec((tm, d), lambda i: (i, 0))
    return pl.pallas_call(
        _dil_combine_proj_kernel,
        out_shape=jax.ShapeDtypeStruct((n, d), F32),
        grid=(n // tm,),
        in_specs=[row] * 6 + [pl.BlockSpec((d, d), lambda i: (0, 0)), row],
        out_specs=row,
        compiler_params=_cparams(("parallel",)),
        name="dilated_combine_proj",
    )(*outs, *lses, w, h)


def _ffn_kernel(h_ref, g_ref, wg_ref, wu_ref, wd_ref, o_ref, xn_ref, acc_ref):
    f = pl.program_id(1)

    @pl.when(f == 0)
    def _():
        xn_ref[...] = _rms(h_ref[...], g_ref[...]).astype(BF16)
        acc_ref[...] = jnp.zeros_like(acc_ref)

    xn = xn_ref[...]
    gate = _dot(xn, wg_ref[...])
    up = _dot(xn, wu_ref[...])
    act = (gate * _sigmoid(gate) * up).astype(BF16)
    acc_ref[...] += _dot(act, wd_ref[...])

    @pl.when(f == pl.num_programs(1) - 1)
    def _():
        o_ref[...] = h_ref[...] + acc_ref[...]


def _ffn(h, g, wg, wu, wd, tm=512, n_f=2):
    n, d = h.shape
    dff = wg.shape[1]
    tf = dff // n_f
    return pl.pallas_call(
        _ffn_kernel,
        out_shape=jax.ShapeDtypeStruct((n, d), F32),
        grid=(n // tm, n_f),
        in_specs=[pl.BlockSpec((tm, d), lambda i, f: (i, 0)),
                  pl.BlockSpec((1, d), lambda i, f: (0, 0)),
                  pl.BlockSpec((d, tf), lambda i, f: (0, f)),
                  pl.BlockSpec((d, tf), lambda i, f: (0, f)),
                  pl.BlockSpec((tf, d), lambda i, f: (f, 0))],
        out_specs=pl.BlockSpec((tm, d), lambda i, f: (i, 0)),
        scratch_shapes=[pltpu.VMEM((tm, d), BF16), pltpu.VMEM((tm, d), F32)],
        compiler_params=_cparams(("parallel", "arbitrary")),
        name="ffn",
    )(h, g.reshape(1, d), wg, wu, wd)


def _moe_kernel(h_ref, g_ref, rh_ref, rl_ref, wg_ref, wu_ref, wd_ref, fg_ref, o_ref,
                xn_ref, comb_ref, acc_ref, *, final_norm):
    e = pl.program_id(1)
    col = lax.broadcasted_iota(jnp.int32, comb_ref.shape, 1)

    @pl.when(e == 0)
    def _():
        xn = _rms(h_ref[...], g_ref[...])
        xh, xl = _split2(xn)
        xn_ref[...] = xh
        logits = _dot(xh, rh_ref[...]) + _dot(xh, rl_ref[...]) + _dot(xl, rh_ref[...])
        colf = col.astype(F32)
        lg = jnp.where(col < N_EXPERTS, logits, NEG)
        m1 = jnp.max(lg, axis=-1, keepdims=True)
        i1 = jnp.min(jnp.where(lg == m1, colf, float(LANES)), axis=-1, keepdims=True)
        lg2 = jnp.where(colf == i1, NEG, lg)
        m2 = jnp.max(lg2, axis=-1, keepdims=True)
        i2 = jnp.min(jnp.where(lg2 == m2, colf, float(LANES)), axis=-1, keepdims=True)
        e2 = jnp.exp(m2 - m1)
        g1 = 1.0 / (1.0 + e2)
        g2 = e2 / (1.0 + e2)
        comb_ref[...] = jnp.where(colf == i1, g1, 0.0) + jnp.where(colf == i2, g2, 0.0)
        acc_ref[...] = jnp.zeros_like(acc_ref)

    xn = xn_ref[...]
    gate = _dot(xn, wg_ref[0])
    up = _dot(xn, wu_ref[0])
    ce = jnp.sum(jnp.where(col == e, comb_ref[...], 0.0), axis=-1, keepdims=True)
    act = (gate * _sigmoid(gate) * up * ce).astype(BF16)
    acc_ref[...] += _dot(act, wd_ref[0])

    @pl.when(e == pl.num_programs(1) - 1)
    def _():
        y = h_ref[...] + acc_ref[...]
        if final_norm:
            y = _rms(y, fg_ref[...])
        o_ref[...] = y


def _moe(h, g, router, wg, wu, wd, final_gain, final_norm, tm=512):
    n, d = h.shape
    ne, _, dfe = wg.shape
    rpad = jnp.zeros((d, LANES), F32).at[:, :ne].set(router)
    rh, rl = _split2(rpad)
    return pl.pallas_call(
        functools.partial(_moe_kernel, final_norm=final_norm),
        out_shape=jax.ShapeDtypeStruct((n, d), F32),
        grid=(n // tm, ne),
        in_specs=[pl.BlockSpec((tm, d), lambda i, e: (i, 0)),
                  pl.BlockSpec((1, d), lambda i, e: (0, 0)),
                  pl.BlockSpec((d, LANES), lambda i, e: (0, 0)),
                  pl.BlockSpec((d, LANES), lambda i, e: (0, 0)),
                  pl.BlockSpec((1, d, dfe), lambda i, e: (e, 0, 0)),
                  pl.BlockSpec((1, d, dfe), lambda i, e: (e, 0, 0)),
                  pl.BlockSpec((1, dfe, d), lambda i, e: (e, 0, 0)),
                  pl.BlockSpec((1, d), lambda i, e: (0, 0))],
        out_specs=pl.BlockSpec((tm, d), lambda i, e: (i, 0)),
        scratch_shapes=[pltpu.VMEM((tm, d), BF16), pltpu.VMEM((tm, LANES), F32),
                        pltpu.VMEM((tm, d), F32)],
        compiler_params=_cparams(("parallel", "arbitrary")),
        name="moe",
    )(h, g.reshape(1, d), rh, rl, wg, wu, wd, final_gain.reshape(1, d))


def _with_ones(vt):
    return jnp.concatenate([vt, jnp.ones((DEN_ROWS, vt.shape[1]), vt.dtype)], axis=0)


def _flash_step(s, vt1, m_ref, acc_ref, idx):
    m_old = m_ref[idx]
    m_new = jnp.maximum(m_old, jnp.max(s, axis=0, keepdims=True))
    p = jnp.exp2(s - m_new).astype(BF16)
    acc_ref[idx] = jnp.exp2(m_old - m_new) * acc_ref[idx] + _dot(vt1, p)
    m_ref[idx] = m_new


def _flash_result(acc_ref, idx):
    acc = acc_ref[idx]
    nf = acc.shape[0] - DEN_ROWS
    return acc[:nf] / acc[nf:nf + 1]


def _sweep_tiles(lo, hi, group, body):
    n_tiles = jnp.maximum(hi - lo, 0)
    full = lax.shift_right_logical(n_tiles, int(math.log2(group)))

    def step(i, carry):
        body(lo + i * group, group)
        return carry

    lax.fori_loop(0, full, step, 0)
    base = lo + full * group
    rem = n_tiles - full * group
    p = group // 2
    while p >= 1:
        take = (rem & p) != 0

        @pl.when(take)
        def _(base=base, p=p):
            body(base, p)

        base = base + jnp.where(take, p, 0)
        p //= 2


def _flash_init(m_ref, acc_ref):
    m_ref[...] = jnp.full(m_ref.shape, NEG, F32)
    acc_ref[...] = jnp.zeros_like(acc_ref)


def _diff_attn_kernel(lam_ref, q_ref, k_ref, vt_ref, bias_ref, g_ref, o_ref, m_ref, acc_ref, sa_ref, sb_ref,
                      *, t, lam_init):
    qi = pl.program_id(2)
    qt = _flip(_eye(LANES), q_ref[...]).astype(BF16)
    rhs = jnp.concatenate([_pad_rows(qt[:HEAD_DIM], True), _pad_rows(qt[HEAD_DIM:], False)], axis=1)
    _flash_init(m_ref, acc_ref)
    gt = ATT_GROUP * t
    n_groups = lax.shift_right_logical(qi + ATT_GROUP, int(math.log2(ATT_GROUP)))

    def key_start(gi):
        return pl.multiple_of(jnp.minimum(gi, n_groups - 1) * gt, gt)

    def scores(gi, s_ref):
        s_ref[...] = _dot(k_ref[pl.ds(key_start(gi), gt), :], rhs)

    def softmax(gi, s_ref):
        vt1 = _with_ones(vt_ref[:, pl.ds(key_start(gi), gt)])
        for mi in range(2):
            bias = jnp.concatenate(
                [bias_ref[mi, _bias_tile_index(qi - (gi * ATT_GROUP + u))] for u in range(ATT_GROUP)], axis=0)
            _flash_step(s_ref[:, mi * t:(mi + 1) * t] + bias, vt1, m_ref, acc_ref, mi)

    scores(0, sa_ref)

    def pair(k, carry):
        scores(2 * k + 1, sb_ref)
        softmax(2 * k, sa_ref)
        scores(2 * k + 2, sa_ref)
        softmax(2 * k + 1, sb_ref)
        return carry

    lax.fori_loop(0, lax.shift_right_logical(n_groups + 1, 1), pair, 0)

    lam = lam_ref[...]
    lam_full = (jnp.exp(jnp.sum(lam[0:1] * lam[1:2], axis=-1, keepdims=True))
                - jnp.exp(jnp.sum(lam[2:3] * lam[3:4], axis=-1, keepdims=True)) + lam_init)
    o = _flash_result(acc_ref, 0) - lam_full * _flash_result(acc_ref, 1)
    ms = jnp.mean(o * o, axis=0, keepdims=True)
    y = (o * lax.rsqrt(ms + EPS) * g_ref[...] * (1.0 - lam_init)).astype(BF16)
    o_ref[...] = _flip(_eye(t), y).astype(o_ref.dtype)


def _diff_attention(z, zt, lam, subln, bias_tiles, lam_init):
    b, s, _ = z.shape
    t = ATT_T
    nh = N_HEADS // 2
    n_tiles = bias_tiles.shape[1]
    return pl.pallas_call(
        functools.partial(_diff_attn_kernel, t=t, lam_init=lam_init),
        out_shape=jax.ShapeDtypeStruct((b, s, D_MODEL), BF16),
        grid=(b, nh, s // t),
        in_specs=[pl.BlockSpec((4, HEAD_DIM), lambda bi, h, qi: (0, 0)),
                  pl.BlockSpec((None, t, LANES), lambda bi, h, qi: (bi, qi, h)),
                  _resident((None, s, LANES), lambda bi, h, qi: (bi, 0, nh + h)),
                  _resident((LANES, s), lambda bi, h, qi: (h, bi)),
                  _resident((2, n_tiles, t, t), lambda bi, h, qi: (h, 0, 0, 0)),
                  pl.BlockSpec((LANES, 1), lambda bi, h, qi: (0, 0))],
        out_specs=pl.BlockSpec((None, t, LANES), lambda bi, h, qi: (bi, qi, h)),
        scratch_shapes=[pltpu.VMEM((2, 1, t), F32), pltpu.VMEM((2, LANES + DEN_ROWS, t), F32),
                        pltpu.VMEM((ATT_GROUP * t, 2 * t), F32), pltpu.VMEM((ATT_GROUP * t, 2 * t), F32)],
        compiler_params=_cparams(("parallel", "parallel", "arbitrary")),
        name="diff_attention",
    )(lam, z, z, zt, bias_tiles, subln.reshape(LANES, 1))


def _dilated_kernel(q_ref, kp_ref, kc_ref, vp_ref, vc_ref, bias_ref, o_ref, lse_ref):
    n = pl.program_id(2)
    variant = jnp.minimum(n, 1)
    w = q_ref.shape[0]
    eye = _eye(LANES)
    eye_w = _eye(w)
    pairs = range(N_HEADS // 2)
    cols = [slice(hp * LANES, (hp + 1) * LANES) for hp in pairs]
    qts = [_flip(eye, q_ref[:, c]).astype(BF16) for c in cols]
    kcats = [jnp.concatenate([kp_ref[:, c], kc_ref[:, c]], axis=0) for c in cols]
    vts = [_flip(eye, jnp.concatenate([vp_ref[:, c], vc_ref[:, c]], axis=0)).astype(BF16) for c in cols]
    s2s = [_dot(kcats[hp], jnp.concatenate([_pad_rows(qts[hp][:HEAD_DIM], True),
                                            _pad_rows(qts[hp][HEAD_DIM:], False)], axis=1))
           for hp in pairs]
    ps, lses = [], []
    for hp in pairs:
        for sub in range(2):
            s = s2s[hp][:, sub * w:(sub + 1) * w] + bias_ref[2 * hp + sub, variant]
            m = jnp.max(s, axis=0, keepdims=True)
            e = jnp.exp(s - m)
            den = jnp.sum(e, axis=0, keepdims=True)
            ps.append((e * (1.0 / den)).astype(BF16))
            lses.append(jnp.broadcast_to(m + jnp.log(den), (HEAD_DIM, w)))
    outs = [_dot(vts[hp][sub * HEAD_DIM:(sub + 1) * HEAD_DIM], ps[2 * hp + sub])
            for hp in pairs for sub in range(2)]
    for hp in pairs:
        pair_out = jnp.concatenate(outs[2 * hp:2 * hp + 2], axis=0).astype(BF16)
        o_ref[:, cols[hp]] = _flip(eye_w, pair_out).astype(o_ref.dtype)
        pair_lse = jnp.concatenate(lses[2 * hp:2 * hp + 2], axis=0)
        lse_ref[:, cols[hp]] = sum(_flip(eye_w, term) for term in _split3(pair_lse))


def _dilated_group(z, bias, dilation):
    b, s, c3 = z.shape
    w = DIL_W
    m_len = s // dilation
    nb = m_len // w
    zv = z.reshape(b, m_len, dilation * c3)
    blk = lambda sect, prev: pl.BlockSpec(
        (None, w, D_MODEL),
        (lambda bi, r, n: (bi, jnp.maximum(n - 1, 0), 3 * r + sect)) if prev
        else (lambda bi, r, n: (bi, n, 3 * r + sect)))
    out_blk = pl.BlockSpec((None, w, D_MODEL), lambda bi, r, n: (bi, n, r))
    o, lse = pl.pallas_call(
        _dilated_kernel,
        out_shape=(jax.ShapeDtypeStruct((b, m_len, dilation * D_MODEL), BF16),
                   jax.ShapeDtypeStruct((b, m_len, dilation * D_MODEL), F32)),
        grid=(b, dilation, nb),
        in_specs=[blk(0, False), blk(1, True), blk(1, False), blk(2, True), blk(2, False),
                  pl.BlockSpec((N_HEADS, 2, 2 * w, w), lambda bi, r, n: (0, 0, 0, 0))],
        out_specs=(out_blk, out_blk),
        compiler_params=_cparams(("parallel", "parallel", "arbitrary")),
        name=f"dilated_d{dilation}",
    )(zv, zv, zv, zv, zv, bias)
    return o.reshape(b * s, D_MODEL), lse.reshape(b * s, D_MODEL)


def _stick_kernel(q_ref, k_ref, vt_ref, o_ref, acc_ref, carry_ref, *, t):
    qi = pl.program_id(2)
    qt = _flip(_eye(LANES), q_ref[...]).astype(BF16)
    rhs = jnp.concatenate([_pad_rows(qt[:HEAD_DIM], True), _pad_rows(qt[HEAD_DIM:], False)], axis=1)
    row = lax.broadcasted_iota(jnp.int32, (t, t), 0)
    col = lax.broadcasted_iota(jnp.int32, (t, t), 1)
    after = jnp.where(col > row, 1.0, 0.0).astype(BF16)
    before = jnp.concatenate([row < col, row < col], axis=1)
    acc_ref[...] = jnp.zeros_like(acc_ref)
    carry_ref[...] = jnp.zeros_like(carry_ref)

    def block(j, masked):
        start = pl.multiple_of(j * t, t)
        z = _dot(k_ref[pl.ds(start, t), :], rhs)
        log_keep = -(jnp.maximum(z, 0.0) + jnp.log(1.0 + jnp.exp(-jnp.abs(z))))
        if masked:
            log_keep = jnp.where(before, log_keep, 0.0)
        hi, lo = _split2(log_keep)
        later = _dot(after, hi) + _dot(after, lo) + carry_ref[...]
        a = jnp.exp(z + log_keep + later)
        if masked:
            a = jnp.where(before, a, 0.0)
        a = a.astype(BF16)
        for sub in range(2):
            vt = vt_ref[sub * HEAD_DIM:(sub + 1) * HEAD_DIM, pl.ds(start, t)]
            acc_ref[sub] += _dot(vt, a[:, sub * t:(sub + 1) * t])
        carry_ref[...] += jnp.sum(log_keep, axis=0, keepdims=True)

    block(qi, True)

    def cond(c):
        return jnp.logical_and(c[0] >= 0, c[1] >= SB_CUTOFF)

    def body(c):
        block(c[0], False)
        return c[0] - 1, jnp.max(carry_ref[...])

    lax.while_loop(cond, body, (qi - 1, jnp.max(carry_ref[...])))
    out = jnp.concatenate([acc_ref[0], acc_ref[1]], axis=0).astype(BF16)
    o_ref[...] = _flip(_eye(t), out).astype(o_ref.dtype)


def _stick_attention(z, zt):
    b, s, _ = z.shape
    t = SB_T
    nh = N_HEADS // 2
    return pl.pallas_call(
        functools.partial(_stick_kernel, t=t),
        out_shape=jax.ShapeDtypeStruct((b, s, D_MODEL), BF16),
        grid=(b, nh, s // t),
        in_specs=[pl.BlockSpec((None, t, LANES), lambda bi, h, qi: (bi, qi, h)),
                  _resident((None, s, LANES), lambda bi, h, qi: (bi, 0, nh + h)),
                  _resident((LANES, s), lambda bi, h, qi: (h, bi))],
        out_specs=pl.BlockSpec((None, t, LANES), lambda bi, h, qi: (bi, qi, h)),
        scratch_shapes=[pltpu.VMEM((2, HEAD_DIM, t), F32), pltpu.VMEM((1, 2 * t), F32)],
        compiler_params=_cparams(("parallel", "parallel", "arbitrary")),
        name="stick_breaking",
    )(z, z, zt)


def _compress_kernel(a_ref, as_ref, w1_ref, pos_ref, w2_ref, w2t_ref, o_ref, ot_ref):
    half = a_ref.shape[1]
    w1 = w1_ref[...]
    const = _dot(jnp.broadcast_to(pos_ref[...], (8, 2 * half)), w1)[0:1]
    pre = _dot(a_ref[...], w1[:half]) + _dot(as_ref[...], w1[half:]) + const
    hid = 0.5 * pre * (1.0 + jnp.tanh(math.sqrt(2.0 / math.pi) * (pre + 0.044715 * pre * pre * pre)))
    hid = hid.astype(BF16)
    o_ref[...] = _dot(hid, w2_ref[...]).astype(o_ref.dtype)
    ot_ref[...] = _nt_dot(w2t_ref[...], hid).astype(ot_ref.dtype)


def _compress(blocks, blocks_next, w1, pos, w2, tr=256):
    _, nkv, b, r, half = blocks.shape
    tr = min(tr, r)
    w2pad = jnp.concatenate([w2, jnp.zeros_like(w2)], axis=-1)
    w2t = jnp.swapaxes(w2, 1, 2)
    spec = pl.BlockSpec((None, None, None, tr, half), lambda j, hk, bi, ri: (j, hk, bi, ri, 0))
    return pl.pallas_call(
        _compress_kernel,
        out_shape=(jax.ShapeDtypeStruct((2, nkv, b, r, LANES), BF16),
                   jax.ShapeDtypeStruct((2, nkv, b, HEAD_DIM, r), BF16)),
        grid=(2, nkv, b, r // tr),
        in_specs=[spec, spec,
                  pl.BlockSpec((None, 2 * half, NSA_CMP_HID), lambda j, hk, bi, ri: (j, 0, 0)),
                  pl.BlockSpec((None, 1, 2 * half), lambda j, hk, bi, ri: (j, 0, 0)),
                  pl.BlockSpec((None, NSA_CMP_HID, LANES), lambda j, hk, bi, ri: (j, 0, 0)),
                  pl.BlockSpec((None, HEAD_DIM, NSA_CMP_HID), lambda j, hk, bi, ri: (j, 0, 0))],
        out_specs=(pl.BlockSpec((None, None, None, tr, LANES), lambda j, hk, bi, ri: (j, hk, bi, ri, 0)),
                   pl.BlockSpec((None, None, None, HEAD_DIM, tr), lambda j, hk, bi, ri: (j, hk, bi, 0, ri))),
        compiler_params=_cparams(("parallel", "parallel", "parallel", "arbitrary")),
        name="nsa_compress",
    )(blocks, blocks_next, w1, pos, w2pad, w2t)


def _nsa_select_kernel(q_ref, kc_ref, vct_ref, cmat_ref, o_ref, sel_ref, *, t, n_sel):
    qi = pl.program_id(2)
    kc = kc_ref[...]
    vct = vct_ref[...]
    r = kc.shape[0]
    n_slc = sel_ref.shape[0]
    qt = _flip(_eye(NSA_GROUP * HEAD_DIM), q_ref[...]).astype(BF16)
    tpos = qi * t + lax.broadcasted_iota(jnp.int32, (1, t), 1)
    cidx = lax.broadcasted_iota(jnp.int32, (r, t), 0)
    seen = NSA_CMP_STRIDE * cidx + (NSA_CMP_LEN - 1) <= tpos
    live = (tpos >= NSA_CMP_LEN - 1).astype(F32)
    imp = jnp.zeros((r, t), F32)
    for g in range(NSA_GROUP):
        rhs = _pad_rows(qt[g * HEAD_DIM:(g + 1) * HEAD_DIM], True)
        sc = jnp.where(seen, _dot(kc, rhs), NEG)
        e = jnp.exp2(sc - jnp.max(sc, axis=0, keepdims=True))
        pc = e * (live / jnp.sum(e, axis=0, keepdims=True))
        o_ref[g * HEAD_DIM:(g + 1) * HEAD_DIM, :] = _dot(vct, pc.astype(BF16)).astype(o_ref.dtype)
        imp = imp + pc
    cmat = cmat_ref[...]
    imp_slc = sum(_dot(cmat, term) for term in _split3(imp))
    blk = lax.broadcasted_iota(jnp.int32, (n_slc, t), 0)
    cur = tpos // NSA_SLC_LEN
    forced = (blk == 0) | (blk == cur) | (blk == cur - 1)
    valid = blk * NSA_SLC_LEN <= tpos
    ranked = jnp.logical_and(valid, jnp.logical_not(forced))
    score = jnp.where(ranked, imp_slc, -1.0)
    blkf = blk.astype(F32)
    for _ in range(n_sel - 3):
        top = jnp.max(score, axis=0, keepdims=True)
        first = jnp.min(jnp.where(score == top, blkf, float(n_slc)), axis=0, keepdims=True)
        score = jnp.where(blkf == first, -jnp.inf, score)
    chosen = jnp.logical_or(jnp.logical_and(valid, forced), jnp.logical_and(ranked, score == -jnp.inf))
    sel_ref[...] = jnp.where(chosen, 1.0, 0.0).astype(sel_ref.dtype)


def _nsa_select(z, kc, vct, cmat, n_sel):
    b, s, _ = z.shape
    t = ATT_T
    r = kc.shape[2]
    n_slc = cmat.shape[0]
    gw = NSA_GROUP * HEAD_DIM
    return pl.pallas_call(
        functools.partial(_nsa_select_kernel, t=t, n_sel=n_sel),
        out_shape=(jax.ShapeDtypeStruct((b, NSA_KV, gw, s), BF16),
                   jax.ShapeDtypeStruct((b, NSA_KV, n_slc, s), BF16)),
        grid=(b, NSA_KV, s // t),
        in_specs=[pl.BlockSpec((None, t, gw), lambda bi, hk, qi: (bi, qi, hk)),
                  pl.BlockSpec((None, None, r, LANES), lambda bi, hk, qi: (hk, bi, 0, 0)),
                  pl.BlockSpec((None, None, HEAD_DIM, r), lambda bi, hk, qi: (hk, bi, 0, 0)),
                  pl.BlockSpec((n_slc, r), lambda bi, hk, qi: (0, 0))],
        out_specs=(pl.BlockSpec((None, None, gw, t), lambda bi, hk, qi: (bi, hk, 0, qi)),
                   pl.BlockSpec((None, None, n_slc, t), lambda bi, hk, qi: (bi, hk, 0, qi))),
        compiler_params=_cparams(("parallel", "parallel", "arbitrary")),
        name="nsa_select",
    )(z, kc, vct, cmat)


def _nsa_attend_kernel(q_ref, ks_ref, kw_ref, vst_ref, vwt_ref, sel_ref, ocmp_ref, zg_ref, bs_ref, bw_ref,
                       o_ref, m_ref, acc_ref, res_ref, gate_ref, sel32_ref, sa_ref, sb_ref, *, t, n_win):
    hk = pl.program_id(1)
    qi = pl.program_id(2)
    per_tile = t // NSA_SLC_LEN
    gw = NSA_GROUP * HEAD_DIM
    qt = _flip(_eye(gw), q_ref[...]).astype(BF16)
    low_half = (hk % 2) == 0
    rhs = []
    for g in range(NSA_GROUP):
        qg = qt[g * HEAD_DIM:(g + 1) * HEAD_DIM]
        rhs.append(jnp.where(low_half, _pad_rows(qg, True), _pad_rows(qg, False)))
    rhs = jnp.concatenate(rhs, axis=1)
    sel32_ref[...] = sel_ref[...].astype(F32)

    _flash_init(m_ref, acc_ref)
    grp = NSA_SWEEP_GROUP
    gt = grp * t
    n_groups = lax.shift_right_logical(qi + grp, int(math.log2(grp)))

    def first_tile(gi):
        return jnp.minimum(gi, n_groups - 1) * grp

    def scores(gi, s_ref):
        start = pl.multiple_of(first_tile(gi) * t, t)
        s_ref[...] = _dot(ks_ref[pl.ds(start, gt), :], rhs)

    def softmax(gi, s_ref):
        j0 = first_tile(gi)
        vt1 = _with_ones(vst_ref[:, pl.ds(pl.multiple_of(j0 * t, t), gt)])
        keep = jnp.concatenate(
            [jnp.broadcast_to(sel32_ref[pl.ds(j0 * per_tile + u, 1), :], (NSA_SLC_LEN, t))
             for u in range(grp * per_tile)], axis=0) > 0.5
        for g in range(NSA_GROUP):
            bias = jnp.concatenate(
                [bs_ref[g, _bias_tile_index(qi - (gi * grp + u))] for u in range(grp)], axis=0)
            s = jnp.where(keep, s_ref[:, g * t:(g + 1) * t] + bias, NEG)
            _flash_step(s, vt1, m_ref, acc_ref, g)

    scores(0, sa_ref)

    def pair(k, carry):
        scores(2 * k + 1, sb_ref)
        softmax(2 * k, sa_ref)
        scores(2 * k + 2, sa_ref)
        softmax(2 * k + 1, sb_ref)
        return carry

    lax.fori_loop(0, lax.shift_right_logical(n_groups + 1, 1), pair, 0)
    for g in range(NSA_GROUP):
        res_ref[g] = _flash_result(acc_ref, g)

    _flash_init(m_ref, acc_ref)

    def win_tiles(count):
        start = pl.multiple_of((qi - (count - 1)) * t, t)
        s4 = _dot(kw_ref[pl.ds(start, count * t), :], rhs)
        vt1 = _with_ones(vwt_ref[:, pl.ds(start, count * t)])
        for g in range(NSA_GROUP):
            bias = jnp.concatenate([bw_ref[g, count - 1 - u] for u in range(count)], axis=0)
            _flash_step(s4[:, g * t:(g + 1) * t] + bias, vt1, m_ref, acc_ref, g)

    for count in range(1, n_win + 1):
        @pl.when((qi == count - 1) if count < n_win else (qi >= count - 1))
        def _(count=count):
            win_tiles(count)

    gate_ref[...] = _sigmoid(_flip(_eye(LANES), zg_ref[...]))
    mixed = []
    for g in range(NSA_GROUP):
        base = 3 * (NSA_GROUP * hk + g)
        gates = [gate_ref[pl.ds(base + c, 1), :] for c in range(3)]
        o_cmp = ocmp_ref[g * HEAD_DIM:(g + 1) * HEAD_DIM, :].astype(F32)
        mixed.append(gates[0] * o_cmp + gates[1] * res_ref[g] + gates[2] * _flash_result(acc_ref, g))
    o_ref[...] = _flip(_eye(t), jnp.concatenate(mixed, axis=0).astype(BF16)).astype(o_ref.dtype)


def _nsa_attend(z, zt, sel, o_cmp, bias_slc, bias_win, cols):
    b, s, _ = z.shape
    t = ATT_T
    n_slc = sel.shape[2]
    gw = NSA_GROUP * HEAD_DIM
    n_win = bias_win.shape[1]
    n_tiles = bias_slc.shape[1]
    c_slc, c_win, c_gate = cols["k_slc"], cols["k_win"], cols["gate"]
    return pl.pallas_call(
        functools.partial(_nsa_attend_kernel, t=t, n_win=n_win),
        out_shape=jax.ShapeDtypeStruct((b, s, D_MODEL), BF16),
        grid=(b, NSA_KV, s // t),
        in_specs=[pl.BlockSpec((None, t, gw), lambda bi, hk, qi: (bi, qi, hk)),
                  _resident((None, s, LANES), lambda bi, hk, qi: (bi, 0, c_slc + hk // 2)),
                  _resident((None, s, LANES), lambda bi, hk, qi: (bi, 0, c_win + hk // 2)),
                  _resident((HEAD_DIM, s), lambda bi, hk, qi: (hk, bi)),
                  _resident((HEAD_DIM, s), lambda bi, hk, qi: (NSA_KV + hk, bi)),
                  pl.BlockSpec((None, None, n_slc, t), lambda bi, hk, qi: (bi, hk, 0, qi)),
                  pl.BlockSpec((None, None, gw, t), lambda bi, hk, qi: (bi, hk, 0, qi)),
                  pl.BlockSpec((None, t, LANES), lambda bi, hk, qi: (bi, qi, c_gate)),
                  _resident((NSA_GROUP, n_tiles, t, t), lambda bi, hk, qi: (hk, 0, 0, 0)),
                  _resident((NSA_GROUP, n_win, t, t), lambda bi, hk, qi: (hk, 0, 0, 0))],
        out_specs=pl.BlockSpec((None, t, gw), lambda bi, hk, qi: (bi, qi, hk)),
        scratch_shapes=[pltpu.VMEM((NSA_GROUP, 1, t), F32),
                        pltpu.VMEM((NSA_GROUP, HEAD_DIM + DEN_ROWS, t), F32),
                        pltpu.VMEM((NSA_GROUP, HEAD_DIM, t), F32),
                        pltpu.VMEM((LANES, t), F32), pltpu.VMEM((n_slc, t), F32),
                        pltpu.VMEM((NSA_SWEEP_GROUP * t, NSA_GROUP * t), F32),
                        pltpu.VMEM((NSA_SWEEP_GROUP * t, NSA_GROUP * t), F32)],
        compiler_params=_cparams(("parallel", "parallel", "arbitrary")),
        name="nsa_attend",
    )(z, z, z, zt, zt, sel, o_cmp, z, bias_slc, bias_win)


def _nsa_layout(w_in):
    d = w_in.shape[0]
    nq = N_HEADS * HEAD_DIM
    kvw = NSA_KV * HEAD_DIM
    kv = [w_in[:, nq + j * kvw: nq + (j + 1) * kvw] for j in range(6)]
    wg = w_in[:, nq + 6 * kvw:]
    wg = jnp.concatenate([wg, jnp.zeros((d, LANES - wg.shape[1]), w_in.dtype)], axis=-1)
    w = jnp.concatenate([w_in[:, :nq] * (SCALE * LOG2E), kv[2], kv[4], kv[0], kv[1], wg], axis=-1)
    wt = jnp.concatenate([kv[3], kv[5]], axis=-1).T
    cols = {"k_slc": nq // LANES, "k_win": (nq + kvw) // LANES, "k_cmp": nq + 2 * kvw,
            "v_cmp": nq + 3 * kvw, "gate": (nq + 4 * kvw) // LANES}
    return w.astype(BF16), wt.astype(BF16), cols


def _importance_matrix_np(n_slc, r):
    ratio = NSA_SLC_LEN // NSA_CMP_STRIDE
    span = NSA_CMP_LEN // NSA_CMP_STRIDE
    coef = np.convolve(np.ones(ratio), np.ones(span))
    cmat = np.zeros((n_slc, r), np.float32)
    for j in range(n_slc):
        for o, cf in enumerate(coef):
            c = ratio * j - o
            if c >= 0:
                cmat[j, c] = cf
    return cmat


def _nsa_mixer(h, g, b, s, w_in, cmp_pos, cmp_w1, cmp_w2, tab, bias_causal):
    w, wt, cols = _nsa_layout(w_in)
    z, zt = _norm_matmul(h, g, w, wt)
    z = z.reshape(b, s, -1)
    kvw = NSA_KV * HEAD_DIM
    r = s // NSA_CMP_STRIDE

    def rows_of_16(x):
        x = x.reshape(b, -1, NSA_KV, HEAD_DIM).transpose(2, 0, 1, 3)
        return x.reshape(NSA_KV, b, -1, NSA_CMP_STRIDE * HEAD_DIM)

    kv_c = jnp.stack([z[..., cols["k_cmp"]:cols["k_cmp"] + kvw], z[..., cols["v_cmp"]:cols["v_cmp"] + kvw]])
    nxt = jnp.concatenate([kv_c[:, :, NSA_CMP_STRIDE:], jnp.zeros_like(kv_c[:, :, :NSA_CMP_STRIDE])], axis=2)
    blocks = jnp.stack([rows_of_16(kv_c[0]), rows_of_16(kv_c[1])])
    blocks_next = jnp.stack([rows_of_16(nxt[0]), rows_of_16(nxt[1])])
    flat = NSA_CMP_LEN * HEAD_DIM
    cmp_tok, cmp_feat = _compress(blocks, blocks_next, cmp_w1.astype(BF16),
                                  cmp_pos.reshape(2, 1, flat).astype(BF16), cmp_w2.astype(BF16))

    n_slc = s // NSA_SLC_LEN
    cmat = jnp.asarray(_importance_matrix_np(n_slc, r), BF16)
    o_cmp, sel = _nsa_select(z, cmp_tok[0], cmp_feat[1], cmat, min(NSA_TOP_N, n_slc))
    bias_win = _build_bias_tiles(tab, _causal_bucket_tiles(ATT_T, -(-NSA_WINDOW // ATT_T) + 1, NSA_WINDOW))
    return _nsa_attend(z, zt, sel, o_cmp, bias_causal, bias_win, cols)


def _scaled_qkv(w_in, scale=SCALE):
    return jnp.concatenate([w_in[:, :D_MODEL] * scale, w_in[:, D_MODEL:]], axis=1).astype(BF16)


def kernel(x, rel_bias, norm_gains, final_gain, a_w_in, a_lambda, a_subln, a_w_out, b_w_in, b_w_out,
           c_w_in, c_w_out, d_w_in, d_cmp_pos, d_cmp_w1, d_cmp_w2, d_w_out, ffn_w_gate, ffn_w_up,
           ffn_w_down, moe_router, moe_w_gate, moe_w_up, moe_w_down):
    b, s, d = x.shape
    assert d == D_MODEL and s % (B_PAIRS[-1][0]) == 0 and s >= (N_NEAR + 1) * ATT_T
    assert _t5_bucket_np(np.array([(N_NEAR - 1) * ATT_T + 1]))[0] == NUM_BUCKETS - 1
    n = b * s
    h = x.reshape(n, d)
    tab = rel_bias.T.astype(F32)
    tab_rel = (tab - tab[:, NUM_BUCKETS - 1:]) * LOG2E
    bias_causal = _build_bias_tiles(tab_rel, _sweep_bucket_tiles(ATT_T))

    w = _scaled_qkv(a_w_in[0], SCALE * LOG2E)
    z, zt = _norm_matmul(h, norm_gains[0, 0], w[:, :2 * d], w[:, 2 * d:].T)
    lam_init = 0.8 - 0.6 * math.exp(-0.3 * 0)
    o = _diff_attention(z.reshape(b, s, -1), zt, a_lambda[0].astype(F32), a_subln[0].astype(F32),
                        bias_causal, lam_init)
    h = _proj_residual(o.reshape(n, d), a_w_out[0].astype(BF16), h)
    h = _ffn(h, norm_gains[0, 1], ffn_w_gate[0].astype(BF16), ffn_w_up[0].astype(BF16),
             ffn_w_down[0].astype(BF16))

    z = _norm_matmul(h, norm_gains[1, 0], _scaled_qkv(b_w_in[0])).reshape(b, s, -1)
    outs, lses = [], []
    for _, dil in B_PAIRS:
        bias = _build_bias_tiles(tab, _dilated_bucket_tiles(dil))
        o, lse = _dilated_group(z, bias, dil)
        outs.append(o)
        lses.append(lse)
    h = _dil_combine_proj(outs, lses, b_w_out[0].astype(BF16), h)
    h = _moe(h, norm_gains[1, 1], moe_router[0], moe_w_gate[0].astype(BF16), moe_w_up[0].astype(BF16),
             moe_w_down[0].astype(BF16), final_gain, False)

    w = _scaled_qkv(c_w_in[0])
    z, zt = _norm_matmul(h, norm_gains[2, 0], w[:, :2 * d], w[:, 2 * d:].T)
    o = _stick_attention(z.reshape(b, s, -1), zt)
    h = _proj_residual(o.reshape(n, d), c_w_out[0].astype(BF16), h)
    h = _ffn(h, norm_gains[2, 1], ffn_w_gate[1].astype(BF16), ffn_w_up[1].astype(BF16),
             ffn_w_down[1].astype(BF16))

    o = _nsa_mixer(h, norm_gains[3, 0], b, s, d_w_in[0], d_cmp_pos[0], d_cmp_w1[0], d_cmp_w2[0], tab_rel,
                   bias_causal)
    h = _proj_residual(o.reshape(n, d), d_w_out[0].astype(BF16), h)
    h = _moe(h, norm_gains[3, 1], moe_router[1], moe_w_gate[1].astype(BF16), moe_w_up[1].astype(BF16),
             moe_w_down[1].astype(BF16), final_gain, True)
    return h.reshape(b, s, d)
```

```python
import functools
import math

import numpy as np
import jax
import jax.numpy as jnp
from jax import lax
from jax.experimental import pallas as pl
from jax.experimental.pallas import tpu as pltpu

F32 = jnp.float32
BF16 = jnp.bfloat16

D_MODEL = 1024
HEAD_DIM = 64
N_HEADS = 16
LANES = 128
B_PAIRS = ((128, 1), (512, 4), (2048, 16))
DIL_W = 128
NSA_KV = 4
NSA_GROUP = 4
NSA_CMP_LEN = 32
NSA_CMP_STRIDE = 16
NSA_CMP_HID = 256
NSA_SLC_LEN = 64
NSA_TOP_N = 16
NSA_WINDOW = 512
NUM_BUCKETS = 32
MAX_EXACT = 16
REL_MAX_DIST = 2048
N_EXPERTS = 8
EPS = 1e-6
SCALE = HEAD_DIM ** -0.5
NEG = -1e30
MASKED_BUCKET = NUM_BUCKETS

ATT_T = 256
N_NEAR = 7
ATT_GROUP = 2
NSA_SWEEP_GROUP = 2
DEN_ROWS = 16
LOG2E = math.log2(math.e)
SB_T = 256
SB_CUTOFF = -104.0
VMEM_LIMIT = 56 << 20


def _cparams(sem):
    return pltpu.CompilerParams(dimension_semantics=sem, vmem_limit_bytes=VMEM_LIMIT)


def _resident(block_shape, index_map):
    return pl.BlockSpec(block_shape, index_map, pipeline_mode=pl.Buffered(1))


def _nt_dot(a, b):
    return lax.dot_general(a, b, (((1,), (1,)), ((), ())), preferred_element_type=F32)


def _dot(a, b):
    return jnp.dot(a, b, preferred_element_type=F32)


def _split2(x):
    hi = x.astype(BF16)
    lo = (x - hi.astype(F32)).astype(BF16)
    return hi, lo


def _split3(x):
    hi = x.astype(BF16)
    rem = x - hi.astype(F32)
    mid = rem.astype(BF16)
    lo = (rem - mid.astype(F32)).astype(BF16)
    return hi, mid, lo


def _rms(x, g):
    ms = jnp.mean(x * x, axis=-1, keepdims=True)
    return x * lax.rsqrt(ms + EPS) * g


def _sigmoid(x):
    return 1.0 / (1.0 + jnp.exp(-x))


def _eye(n):
    r = lax.broadcasted_iota(jnp.int32, (n, n), 0)
    c = lax.broadcasted_iota(jnp.int32, (n, n), 1)
    return jnp.where(r == c, 1.0, 0.0).astype(BF16)


def _flip(eye, x):
    return _nt_dot(eye, x)


def _pad_rows(x, top):
    zero = jnp.zeros_like(x)
    return jnp.concatenate([x, zero] if top else [zero, x], axis=0)


def _t5_bucket_np(dist):
    n = np.maximum(dist, 0)
    ratio = np.log(np.maximum(n, 1).astype(np.float32) / np.float32(MAX_EXACT)) / np.float32(
        math.log(REL_MAX_DIST / MAX_EXACT))
    large = np.minimum(MAX_EXACT + (ratio * np.float32(NUM_BUCKETS - MAX_EXACT)).astype(np.int32),
                       NUM_BUCKETS - 1)
    return np.where(n < MAX_EXACT, n, large).astype(np.int32)


def _causal_bucket_tiles(t, n_tiles, max_dist=None):
    i = np.arange(t)[None, None, :]
    j = np.arange(t)[None, :, None]
    dist = np.arange(n_tiles)[:, None, None] * t + i - j
    ok = dist >= 0
    if max_dist is not None:
        ok = ok & (dist < max_dist)
    return np.where(ok, _t5_bucket_np(dist), MASKED_BUCKET).astype(np.int32)


def _sweep_bucket_tiles(t):
    far = np.full((1, t, t), NUM_BUCKETS - 1, np.int32)
    future = np.full((1, t, t), MASKED_BUCKET, np.int32)
    return np.concatenate([_causal_bucket_tiles(t, N_NEAR), far, future], axis=0)


def _bias_tile_index(delta):
    return jnp.where(delta < 0, N_NEAR + 1, jnp.minimum(delta, N_NEAR))


def _dilated_bucket_tiles(dilation):
    w = DIL_W
    steps = w + np.arange(w)[None, :] - np.arange(2 * w)[:, None]
    ok = (steps >= 0) & (steps <= w)
    bkt = np.where(ok, _t5_bucket_np(dilation * steps), MASKED_BUCKET)
    first = np.where(np.arange(2 * w)[:, None] < w, MASKED_BUCKET, bkt)
    return np.stack([first, bkt]).astype(np.int32)


def _bias_build_kernel(tab_ref, bkt_ref, o_ref):
    c = pl.program_id(0)
    bkt = bkt_ref[0]
    out = jnp.full(bkt.shape, NEG, F32)
    for b in range(NUM_BUCKETS):
        out = jnp.where(bkt == b, tab_ref[c, b], out)
    o_ref[0, 0] = out


def _build_bias_tiles(tab, buckets):
    n, r, c = buckets.shape
    return pl.pallas_call(
        _bias_build_kernel,
        out_shape=jax.ShapeDtypeStruct((N_HEADS, n, r, c), F32),
        grid=(N_HEADS, n),
        in_specs=[pl.BlockSpec(memory_space=pltpu.SMEM),
                  pl.BlockSpec((1, r, c), lambda h, i: (i, 0, 0))],
        out_specs=pl.BlockSpec((1, 1, r, c), lambda h, i: (h, i, 0, 0)),
        compiler_params=_cparams(("arbitrary", "arbitrary")),
        name="bias_tiles",
    )(tab, jnp.asarray(buckets))


def _norm_matmul_kernel(x_ref, g_ref, w_ref, *refs, chunk, feature_major):
    xn = _rms(x_ref[...], g_ref[...]).astype(BF16)
    o_ref = refs[1] if feature_major else refs[0]
    nout = o_ref.shape[1]
    for c0 in range(0, nout, chunk):
        c1 = min(c0 + chunk, nout)
        o_ref[:, c0:c1] = _dot(xn, w_ref[:, c0:c1]).astype(o_ref.dtype)
    if feature_major:
        wt_ref, ot_ref = refs[0], refs[2]
        nt = ot_ref.shape[0]
        for c0 in range(0, nt, chunk):
            c1 = min(c0 + chunk, nt)
            ot_ref[c0:c1, :] = _nt_dot(wt_ref[c0:c1, :], xn).astype(ot_ref.dtype)


def _norm_matmul(x, g, w, wt=None, tm=512):
    n, d = x.shape
    nout = w.shape[1]
    in_specs = [pl.BlockSpec((tm, d), lambda i: (i, 0)),
                pl.BlockSpec((1, d), lambda i: (0, 0)),
                pl.BlockSpec((d, nout), lambda i: (0, 0))]
    out_shape = [jax.ShapeDtypeStruct((n, nout), BF16)]
    out_specs = [pl.BlockSpec((tm, nout), lambda i: (i, 0))]
    args = [x, g.reshape(1, d), w]
    if wt is not None:
        nt = wt.shape[0]
        in_specs.append(pl.BlockSpec((nt, d), lambda i: (0, 0)))
        out_shape.append(jax.ShapeDtypeStruct((nt, n), BF16))
        out_specs.append(pl.BlockSpec((nt, tm), lambda i: (0, i)))
        args.append(wt)
    out = pl.pallas_call(
        functools.partial(_norm_matmul_kernel, chunk=512, feature_major=wt is not None),
        out_shape=tuple(out_shape),
        grid=(n // tm,),
        in_specs=in_specs,
        out_specs=tuple(out_specs),
        compiler_params=_cparams(("parallel",)),
        name="norm_matmul",
    )(*args)
    return out if wt is not None else out[0]


def _proj_res_kernel(a_ref, w_ref, h_ref, o_ref):
    o_ref[...] = h_ref[...] + _dot(a_ref[...], w_ref[...])


def _proj_residual(a, w, h, tm=512):
    n, d = h.shape
    k = a.shape[1]
    return pl.pallas_call(
        _proj_res_kernel,
        out_shape=jax.ShapeDtypeStruct((n, d), F32),
        grid=(n // tm,),
        in_specs=[pl.BlockSpec((tm, k), lambda i: (i, 0)),
                  pl.BlockSpec((k, d), lambda i: (0, 0)),
                  pl.BlockSpec((tm, d), lambda i: (i, 0))],
        out_specs=pl.BlockSpec((tm, d), lambda i: (i, 0)),
        compiler_params=_cparams(("parallel",)),
        name="proj_residual",
    )(a, w, h)


def _dil_combine_proj_kernel(o1, o2, o3, l1, l2, l3, w_ref, h_ref, o_ref):
    ls = [l1[...], l2[...], l3[...]]
    m = jnp.maximum(jnp.maximum(ls[0], ls[1]), ls[2])
    es = [jnp.exp(l - m) for l in ls]
    den = es[0] + es[1] + es[2]
    a = (es[0] * o1[...].astype(F32) + es[1] * o2[...].astype(F32) + es[2] * o3[...].astype(F32)) / den
    o_ref[...] = h_ref[...] + _dot(a.astype(BF16), w_ref[...])


def _dil_combine_proj(outs, lses, w, h, tm=512):
    n, d = h.shape
    row = pl.BlockSpec((tm, d), lambda i: (i, 0))
    return pl.pallas_call(
        _dil_combine_proj_kernel,
        out_shape=jax.ShapeDtypeStruct((n, d), F32),
        grid=(n // tm,),
        in_specs=[row] * 6 + [pl.BlockSpec((d, d), lambda i: (0, 0)), row],
        out_specs=row,
        compiler_params=_cparams(("parallel",)),
        name="dilated_combine_proj",
    )(*outs, *lses, w, h)


def _ffn_kernel(h_ref, g_ref, wg_ref, wu_ref, wd_ref, o_ref, xn_ref, acc_ref):
    f = pl.program_id(1)

    @pl.when(f == 0)
    def _():
        xn_ref[...] = _rms(h_ref[...], g_ref[...]).astype(BF16)
        acc_ref[...] = jnp.zeros_like(acc_ref)

    xn = xn_ref[...]
    gate = _dot(xn, wg_ref[...])
    up = _dot(xn, wu_ref[...])
    act = (gate * _sigmoid(gate) * up).astype(BF16)
    acc_ref[...] += _dot(act, wd_ref[...])

    @pl.when(f == pl.num_programs(1) - 1)
    def _():
        o_ref[...] = h_ref[...] + acc_ref[...]


def _ffn(h, g, wg, wu, wd, tm=512, n_f=2):
    n, d = h.shape
    dff = wg.shape[1]
    tf = dff // n_f
    return pl.pallas_call(
        _ffn_kernel,
        out_shape=jax.ShapeDtypeStruct((n, d), F32),
        grid=(n // tm, n_f),
        in_specs=[pl.BlockSpec((tm, d), lambda i, f: (i, 0)),
                  pl.BlockSpec((1, d), lambda i, f: (0, 0)),
                  pl.BlockSpec((d, tf), lambda i, f: (0, f)),
                  pl.BlockSpec((d, tf), lambda i, f: (0, f)),
                  pl.BlockSpec((tf, d), lambda i, f: (f, 0))],
        out_specs=pl.BlockSpec((tm, d), lambda i, f: (i, 0)),
        scratch_shapes=[pltpu.VMEM((tm, d), BF16), pltpu.VMEM((tm, d), F32)],
        compiler_params=_cparams(("parallel", "arbitrary")),
        name="ffn",
    )(h, g.reshape(1, d), wg, wu, wd)


def _moe_kernel(h_ref, g_ref, rh_ref, rl_ref, wg_ref, wu_ref, wd_ref, fg_ref, o_ref,
                xn_ref, comb_ref, acc_ref, *, final_norm):
    e = pl.program_id(1)
    col = lax.broadcasted_iota(jnp.int32, comb_ref.shape, 1)

    @pl.when(e == 0)
    def _():
        xn = _rms(h_ref[...], g_ref[...])
        xh, xl = _split2(xn)
        xn_ref[...] = xh
        logits = _dot(xh, rh_ref[...]) + _dot(xh, rl_ref[...]) + _dot(xl, rh_ref[...])
        colf = col.astype(F32)
        lg = jnp.where(col < N_EXPERTS, logits, NEG)
        m1 = jnp.max(lg, axis=-1, keepdims=True)
        i1 = jnp.min(jnp.where(lg == m1, colf, float(LANES)), axis=-1, keepdims=True)
        lg2 = jnp.where(colf == i1, NEG, lg)
        m2 = jnp.max(lg2, axis=-1, keepdims=True)
        i2 = jnp.min(jnp.where(lg2 == m2, colf, float(LANES)), axis=-1, keepdims=True)
        e2 = jnp.exp(m2 - m1)
        g1 = 1.0 / (1.0 + e2)
        g2 = e2 / (1.0 + e2)
        comb_ref[...] = jnp.where(colf == i1, g1, 0.0) + jnp.where(colf == i2, g2, 0.0)
        acc_ref[...] = jnp.zeros_like(acc_ref)

    xn = xn_ref[...]
    gate = _dot(xn, wg_ref[0])
    up = _dot(xn, wu_ref[0])
    ce = jnp.sum(jnp.where(col == e, comb_ref[...], 0.0), axis=-1, keepdims=True)
    act = (gate * _sigmoid(gate) * up * ce).astype(BF16)
    acc_ref[...] += _dot(act, wd_ref[0])

    @pl.when(e == pl.num_programs(1) - 1)
    def _():
        y = h_ref[...] + acc_ref[...]
        if final_norm:
            y = _rms(y, fg_ref[...])
        o_ref[...] = y


def _moe(h, g, router, wg, wu, wd, final_gain, final_norm, tm=512):
    n, d = h.shape
    ne, _, dfe = wg.shape
    rpad = jnp.zeros((d, LANES), F32).at[:, :ne].set(router)
    rh, rl = _split2(rpad)
    return pl.pallas_call(
        functools.partial(_moe_kernel, final_norm=final_norm),
        out_shape=jax.ShapeDtypeStruct((n, d), F32),
        grid=(n // tm, ne),
        in_specs=[pl.BlockSpec((tm, d), lambda i, e: (i, 0)),
                  pl.BlockSpec((1, d), lambda i, e: (0, 0)),
                  pl.BlockSpec((d, LANES), lambda i, e: (0, 0)),
                  pl.BlockSpec((d, LANES), lambda i, e: (0, 0)),
                  pl.BlockSpec((1, d, dfe), lambda i, e: (e, 0, 0)),
                  pl.BlockSpec((1, d, dfe), lambda i, e: (e, 0, 0)),
                  pl.BlockSpec((1, dfe, d), lambda i, e: (e, 0, 0)),
                  pl.BlockSpec((1, d), lambda i, e: (0, 0))],
        out_specs=pl.BlockSpec((tm, d), lambda i, e: (i, 0)),
        scratch_shapes=[pltpu.VMEM((tm, d), BF16), pltpu.VMEM((tm, LANES), F32),
                        pltpu.VMEM((tm, d), F32)],
        compiler_params=_cparams(("parallel", "arbitrary")),
        name="moe",
    )(h, g.reshape(1, d), rh, rl, wg, wu, wd, final_gain.reshape(1, d))


def _with_ones(vt):
    return jnp.concatenate([vt, jnp.ones((DEN_ROWS, vt.shape[1]), vt.dtype)], axis=0)


def _flash_step(s, vt1, m_ref, acc_ref, idx):
    m_old = m_ref[idx]
    m_new = jnp.maximum(m_old, jnp.max(s, axis=0, keepdims=True))
    p = jnp.exp2(s - m_new).astype(BF16)
    acc_ref[idx] = jnp.exp2(m_old - m_new) * acc_ref[idx] + _dot(vt1, p)
    m_ref[idx] = m_new


def _flash_result(acc_ref, idx):
    acc = acc_ref[idx]
    nf = acc.shape[0] - DEN_ROWS
    return acc[:nf] / acc[nf:nf + 1]


def _far_groups(qi, group):
    return lax.shift_right_logical(jnp.maximum(qi + 1 - N_NEAR, 0), int(math.log2(group)))


def _pipelined_sweep(n_far, n_groups, scores, softmax, sa_ref, sb_ref):
    scores(0, sa_ref)

    def pairs(near):
        def pair(k, carry):
            scores(2 * k + 1, sb_ref)
            softmax(2 * k, sa_ref, near)
            scores(2 * k + 2, sa_ref)
            softmax(2 * k + 1, sb_ref, near)
            return carry
        return pair

    far_pairs = lax.shift_right_logical(n_far, 1)
    lax.fori_loop(0, far_pairs, pairs(False), 0)
    lax.fori_loop(far_pairs, lax.shift_right_logical(n_groups + 1, 1), pairs(True), 0)


def _flash_init(m_ref, acc_ref):
    m_ref[...] = jnp.full(m_ref.shape, NEG, F32)
    acc_ref[...] = jnp.zeros_like(acc_ref)


def _diff_attn_kernel(lam_ref, q_ref, k_ref, vt_ref, bias_ref, g_ref, o_ref, m_ref, acc_ref, sa_ref, sb_ref,
                      *, t, lam_init):
    qi = pl.program_id(2)
    qt = _flip(_eye(LANES), q_ref[...]).astype(BF16)
    rhs = jnp.concatenate([_pad_rows(qt[:HEAD_DIM], True), _pad_rows(qt[HEAD_DIM:], False)], axis=1)
    _flash_init(m_ref, acc_ref)
    gt = ATT_GROUP * t
    n_groups = lax.shift_right_logical(qi + ATT_GROUP, int(math.log2(ATT_GROUP)))

    def key_start(gi):
        return pl.multiple_of(jnp.minimum(gi, n_groups - 1) * gt, gt)

    def scores(gi, s_ref):
        s_ref[...] = _dot(k_ref[pl.ds(key_start(gi), gt), :], rhs)

    def softmax(gi, s_ref, near):
        vt1 = _with_ones(vt_ref[:, pl.ds(key_start(gi), gt)])
        for mi in range(2):
            s = s_ref[:, mi * t:(mi + 1) * t]
            if near:
                s = s + jnp.concatenate(
                    [bias_ref[mi, _bias_tile_index(qi - (gi * ATT_GROUP + u))] for u in range(ATT_GROUP)], axis=0)
            _flash_step(s, vt1, m_ref, acc_ref, mi)

    _pipelined_sweep(_far_groups(qi, ATT_GROUP), n_groups, scores, softmax, sa_ref, sb_ref)

    lam = lam_ref[...]
    lam_full = (jnp.exp(jnp.sum(lam[0:1] * lam[1:2], axis=-1, keepdims=True))
                - jnp.exp(jnp.sum(lam[2:3] * lam[3:4], axis=-1, keepdims=True)) + lam_init)
    o = _flash_result(acc_ref, 0) - lam_full * _flash_result(acc_ref, 1)
    ms = jnp.mean(o * o, axis=0, keepdims=True)
    y = (o * lax.rsqrt(ms + EPS) * g_ref[...] * (1.0 - lam_init)).astype(BF16)
    o_ref[...] = _flip(_eye(t), y).astype(o_ref.dtype)


def _diff_attention(z, zt, lam, subln, bias_tiles, lam_init):
    b, s, _ = z.shape
    t = ATT_T
    nh = N_HEADS // 2
    n_tiles = bias_tiles.shape[1]
    return pl.pallas_call(
        functools.partial(_diff_attn_kernel, t=t, lam_init=lam_init),
        out_shape=jax.ShapeDtypeStruct((b, s, D_MODEL), BF16),
        grid=(b, nh, s // t),
        in_specs=[pl.BlockSpec((4, HEAD_DIM), lambda bi, h, qi: (0, 0)),
                  pl.BlockSpec((None, t, LANES), lambda bi, h, qi: (bi, qi, h)),
                  _resident((None, s, LANES), lambda bi, h, qi: (bi, 0, nh + h)),
                  _resident((LANES, s), lambda bi, h, qi: (h, bi)),
                  _resident((2, n_tiles, t, t), lambda bi, h, qi: (h, 0, 0, 0)),
                  pl.BlockSpec((LANES, 1), lambda bi, h, qi: (0, 0))],
        out_specs=pl.BlockSpec((None, t, LANES), lambda bi, h, qi: (bi, qi, h)),
        scratch_shapes=[pltpu.VMEM((2, 1, t), F32), pltpu.VMEM((2, LANES + DEN_ROWS, t), F32),
                        pltpu.VMEM((ATT_GROUP * t, 2 * t), F32), pltpu.VMEM((ATT_GROUP * t, 2 * t), F32)],
        compiler_params=_cparams(("parallel", "parallel", "arbitrary")),
        name="diff_attention",
    )(lam, z, z, zt, bias_tiles, subln.reshape(LANES, 1))


def _dilated_kernel(q_ref, kp_ref, kc_ref, vp_ref, vc_ref, bias_ref, o_ref, lse_ref):
    n = pl.program_id(2)
    variant = jnp.minimum(n, 1)
    w = q_ref.shape[0]
    eye = _eye(LANES)
    eye_w = _eye(w)
    pairs = range(N_HEADS // 2)
    cols = [slice(hp * LANES, (hp + 1) * LANES) for hp in pairs]
    qts = [_flip(eye, q_ref[:, c]).astype(BF16) for c in cols]
    kcats = [jnp.concatenate([kp_ref[:, c], kc_ref[:, c]], axis=0) for c in cols]
    vts = [_flip(eye, jnp.concatenate([vp_ref[:, c], vc_ref[:, c]], axis=0)).astype(BF16) for c in cols]
    s2s = [_dot(kcats[hp], jnp.concatenate([_pad_rows(qts[hp][:HEAD_DIM], True),
                                            _pad_rows(qts[hp][HEAD_DIM:], False)], axis=1))
           for hp in pairs]
    ps, lses = [], []
    for hp in pairs:
        for sub in range(2):
            s = s2s[hp][:, sub * w:(sub + 1) * w] + bias_ref[2 * hp + sub, variant]
            m = jnp.max(s, axis=0, keepdims=True)
            e = jnp.exp(s - m)
            den = jnp.sum(e, axis=0, keepdims=True)
            ps.append((e * (1.0 / den)).astype(BF16))
            lses.append(jnp.broadcast_to(m + jnp.log(den), (HEAD_DIM, w)))
    outs = [_dot(vts[hp][sub * HEAD_DIM:(sub + 1) * HEAD_DIM], ps[2 * hp + sub])
            for hp in pairs for sub in range(2)]
    for hp in pairs:
        pair_out = jnp.concatenate(outs[2 * hp:2 * hp + 2], axis=0).astype(BF16)
        o_ref[:, cols[hp]] = _flip(eye_w, pair_out).astype(o_ref.dtype)
        pair_lse = jnp.concatenate(lses[2 * hp:2 * hp + 2], axis=0)
        lse_ref[:, cols[hp]] = sum(_flip(eye_w, term) for term in _split3(pair_lse))


def _dilated_group(z, bias, dilation):
    b, s, c3 = z.shape
    w = DIL_W
    m_len = s // dilation
    nb = m_len // w
    zv = z.reshape(b, m_len, dilation * c3)
    blk = lambda sect, prev: pl.BlockSpec(
        (None, w, D_MODEL),
        (lambda bi, r, n: (bi, jnp.maximum(n - 1, 0), 3 * r + sect)) if prev
        else (lambda bi, r, n: (bi, n, 3 * r + sect)))
    out_blk = pl.BlockSpec((None, w, D_MODEL), lambda bi, r, n: (bi, n, r))
    o, lse = pl.pallas_call(
        _dilated_kernel,
        out_shape=(jax.ShapeDtypeStruct((b, m_len, dilation * D_MODEL), BF16),
                   jax.ShapeDtypeStruct((b, m_len, dilation * D_MODEL), F32)),
        grid=(b, dilation, nb),
        in_specs=[blk(0, False), blk(1, True), blk(1, False), blk(2, True), blk(2, False),
                  pl.BlockSpec((N_HEADS, 2, 2 * w, w), lambda bi, r, n: (0, 0, 0, 0))],
        out_specs=(out_blk, out_blk),
        compiler_params=_cparams(("parallel", "parallel", "arbitrary")),
        name=f"dilated_d{dilation}",
    )(zv, zv, zv, zv, zv, bias)
    return o.reshape(b * s, D_MODEL), lse.reshape(b * s, D_MODEL)


def _stick_kernel(q_ref, k_ref, vt_ref, o_ref, acc_ref, carry_ref, *, t):
    qi = pl.program_id(2)
    qt = _flip(_eye(LANES), q_ref[...]).astype(BF16)
    rhs = jnp.concatenate([_pad_rows(qt[:HEAD_DIM], True), _pad_rows(qt[HEAD_DIM:], False)], axis=1)
    row = lax.broadcasted_iota(jnp.int32, (t, t), 0)
    col = lax.broadcasted_iota(jnp.int32, (t, t), 1)
    after = jnp.where(col > row, 1.0, 0.0).astype(BF16)
    before = jnp.concatenate([row < col, row < col], axis=1)
    acc_ref[...] = jnp.zeros_like(acc_ref)
    carry_ref[...] = jnp.zeros_like(carry_ref)

    def block(j, masked):
        start = pl.multiple_of(j * t, t)
        z = _dot(k_ref[pl.ds(start, t), :], rhs)
        log_keep = -(jnp.maximum(z, 0.0) + jnp.log(1.0 + jnp.exp(-jnp.abs(z))))
        if masked:
            log_keep = jnp.where(before, log_keep, 0.0)
        hi, lo = _split2(log_keep)
        later = _dot(after, hi) + _dot(after, lo) + carry_ref[...]
        a = jnp.exp(z + log_keep + later)
        if masked:
            a = jnp.where(before, a, 0.0)
        a = a.astype(BF16)
        for sub in range(2):
            vt = vt_ref[sub * HEAD_DIM:(sub + 1) * HEAD_DIM, pl.ds(start, t)]
            acc_ref[sub] += _dot(vt, a[:, sub * t:(sub + 1) * t])
        carry_ref[...] += jnp.sum(log_keep, axis=0, keepdims=True)

    block(qi, True)

    def cond(c):
        return jnp.logical_and(c[0] >= 0, c[1] >= SB_CUTOFF)

    def body(c):
        block(c[0], False)
        return c[0] - 1, jnp.max(carry_ref[...])

    lax.while_loop(cond, body, (qi - 1, jnp.max(carry_ref[...])))
    out = jnp.concatenate([acc_ref[0], acc_ref[1]], axis=0).astype(BF16)
    o_ref[...] = _flip(_eye(t), out).astype(o_ref.dtype)


def _stick_attention(z, zt):
    b, s, _ = z.shape
    t = SB_T
    nh = N_HEADS // 2
    return pl.pallas_call(
        functools.partial(_stick_kernel, t=t),
        out_shape=jax.ShapeDtypeStruct((b, s, D_MODEL), BF16),
        grid=(b, nh, s // t),
        in_specs=[pl.BlockSpec((None, t, LANES), lambda bi, h, qi: (bi, qi, h)),
                  _resident((None, s, LANES), lambda bi, h, qi: (bi, 0, nh + h)),
                  _resident((LANES, s), lambda bi, h, qi: (h, bi))],
        out_specs=pl.BlockSpec((None, t, LANES), lambda bi, h, qi: (bi, qi, h)),
        scratch_shapes=[pltpu.VMEM((2, HEAD_DIM, t), F32), pltpu.VMEM((1, 2 * t), F32)],
        compiler_params=_cparams(("parallel", "parallel", "arbitrary")),
        name="stick_breaking",
    )(z, z, zt)


def _compress_kernel(a_ref, as_ref, w1_ref, pos_ref, w2_ref, w2t_ref, o_ref, ot_ref):
    half = a_ref.shape[1]
    w1 = w1_ref[...]
    const = _dot(jnp.broadcast_to(pos_ref[...], (8, 2 * half)), w1)[0:1]
    pre = _dot(a_ref[...], w1[:half]) + _dot(as_ref[...], w1[half:]) + const
    hid = 0.5 * pre * (1.0 + jnp.tanh(math.sqrt(2.0 / math.pi) * (pre + 0.044715 * pre * pre * pre)))
    hid = hid.astype(BF16)
    o_ref[...] = _dot(hid, w2_ref[...]).astype(o_ref.dtype)
    ot_ref[...] = _nt_dot(w2t_ref[...], hid).astype(ot_ref.dtype)


def _compress(blocks, blocks_next, w1, pos, w2, tr=256):
    _, nkv, b, r, half = blocks.shape
    tr = min(tr, r)
    w2pad = jnp.concatenate([w2, jnp.zeros_like(w2)], axis=-1)
    w2t = jnp.swapaxes(w2, 1, 2)
    spec = pl.BlockSpec((None, None, None, tr, half), lambda j, hk, bi, ri: (j, hk, bi, ri, 0))
    return pl.pallas_call(
        _compress_kernel,
        out_shape=(jax.ShapeDtypeStruct((2, nkv, b, r, LANES), BF16),
                   jax.ShapeDtypeStruct((2, nkv, b, HEAD_DIM, r), BF16)),
        grid=(2, nkv, b, r // tr),
        in_specs=[spec, spec,
                  pl.BlockSpec((None, 2 * half, NSA_CMP_HID), lambda j, hk, bi, ri: (j, 0, 0)),
                  pl.BlockSpec((None, 1, 2 * half), lambda j, hk, bi, ri: (j, 0, 0)),
                  pl.BlockSpec((None, NSA_CMP_HID, LANES), lambda j, hk, bi, ri: (j, 0, 0)),
                  pl.BlockSpec((None, HEAD_DIM, NSA_CMP_HID), lambda j, hk, bi, ri: (j, 0, 0))],
        out_specs=(pl.BlockSpec((None, None, None, tr, LANES), lambda j, hk, bi, ri: (j, hk, bi, ri, 0)),
                   pl.BlockSpec((None, None, None, HEAD_DIM, tr), lambda j, hk, bi, ri: (j, hk, bi, 0, ri))),
        compiler_params=_cparams(("parallel", "parallel", "parallel", "arbitrary")),
        name="nsa_compress",
    )(blocks, blocks_next, w1, pos, w2pad, w2t)


def _nsa_select_kernel(q_ref, kc_ref, vct_ref, cmat_ref, o_ref, sel_ref, *, t, n_sel):
    qi = pl.program_id(2)
    kc = kc_ref[...]
    vct = vct_ref[...]
    r = kc.shape[0]
    n_slc = sel_ref.shape[0]
    qt = _flip(_eye(NSA_GROUP * HEAD_DIM), q_ref[...]).astype(BF16)
    tpos = qi * t + lax.broadcasted_iota(jnp.int32, (1, t), 1)
    cidx = lax.broadcasted_iota(jnp.int32, (r, t), 0)
    seen = NSA_CMP_STRIDE * cidx + (NSA_CMP_LEN - 1) <= tpos
    live = (tpos >= NSA_CMP_LEN - 1).astype(F32)
    imp = jnp.zeros((r, t), F32)
    for g in range(NSA_GROUP):
        rhs = _pad_rows(qt[g * HEAD_DIM:(g + 1) * HEAD_DIM], True)
        sc = jnp.where(seen, _dot(kc, rhs), NEG)
        e = jnp.exp2(sc - jnp.max(sc, axis=0, keepdims=True))
        pc = e * (live / jnp.sum(e, axis=0, keepdims=True))
        o_ref[g * HEAD_DIM:(g + 1) * HEAD_DIM, :] = _dot(vct, pc.astype(BF16)).astype(o_ref.dtype)
        imp = imp + pc
    cmat = cmat_ref[...]
    imp_slc = sum(_dot(cmat, term) for term in _split3(imp))
    blk = lax.broadcasted_iota(jnp.int32, (n_slc, t), 0)
    cur = tpos // NSA_SLC_LEN
    forced = (blk == 0) | (blk == cur) | (blk == cur - 1)
    valid = blk * NSA_SLC_LEN <= tpos
    ranked = jnp.logical_and(valid, jnp.logical_not(forced))
    score = jnp.where(ranked, imp_slc, -1.0)
    blkf = blk.astype(F32)
    for _ in range(n_sel - 3):
        top = jnp.max(score, axis=0, keepdims=True)
        first = jnp.min(jnp.where(score == top, blkf, float(n_slc)), axis=0, keepdims=True)
        score = jnp.where(blkf == first, -jnp.inf, score)
    chosen = jnp.logical_or(jnp.logical_and(valid, forced), jnp.logical_and(ranked, score == -jnp.inf))
    sel_ref[...] = jnp.where(chosen, 1.0, 0.0).astype(sel_ref.dtype)


def _nsa_select(z, kc, vct, cmat, n_sel):
    b, s, _ = z.shape
    t = ATT_T
    r = kc.shape[2]
    n_slc = cmat.shape[0]
    gw = NSA_GROUP * HEAD_DIM
    return pl.pallas_call(
        functools.partial(_nsa_select_kernel, t=t, n_sel=n_sel),
        out_shape=(jax.ShapeDtypeStruct((b, NSA_KV, gw, s), BF16),
                   jax.ShapeDtypeStruct((b, NSA_KV, n_slc, s), BF16)),
        grid=(b, NSA_KV, s // t),
        in_specs=[pl.BlockSpec((None, t, gw), lambda bi, hk, qi: (bi, qi, hk)),
                  pl.BlockSpec((None, None, r, LANES), lambda bi, hk, qi: (hk, bi, 0, 0)),
                  pl.BlockSpec((None, None, HEAD_DIM, r), lambda bi, hk, qi: (hk, bi, 0, 0)),
                  pl.BlockSpec((n_slc, r), lambda bi, hk, qi: (0, 0))],
        out_specs=(pl.BlockSpec((None, None, gw, t), lambda bi, hk, qi: (bi, hk, 0, qi)),
                   pl.BlockSpec((None, None, n_slc, t), lambda bi, hk, qi: (bi, hk, 0, qi))),
        compiler_params=_cparams(("parallel", "parallel", "arbitrary")),
        name="nsa_select",
    )(z, kc, vct, cmat)


def _nsa_attend_kernel(q_ref, ks_ref, kw_ref, vst_ref, vwt_ref, sel_ref, ocmp_ref, zg_ref, bs_ref, bw_ref,
                       o_ref, m_ref, acc_ref, res_ref, gate_ref, sel32_ref, sa_ref, sb_ref, *, t, n_win):
    hk = pl.program_id(1)
    qi = pl.program_id(2)
    per_tile = t // NSA_SLC_LEN
    gw = NSA_GROUP * HEAD_DIM
    qt = _flip(_eye(gw), q_ref[...]).astype(BF16)
    low_half = (hk % 2) == 0
    q_heads = [qt[g * HEAD_DIM:(g + 1) * HEAD_DIM] for g in range(NSA_GROUP)]
    rhs = jnp.concatenate([jnp.where(low_half, _pad_rows(qg, True), _pad_rows(qg, False)) for qg in q_heads],
                          axis=1)
    sel32_ref[...] = sel_ref[...].astype(F32)

    _flash_init(m_ref, acc_ref)
    grp = NSA_SWEEP_GROUP
    gt = grp * t
    n_groups = lax.shift_right_logical(qi + grp, int(math.log2(grp)))

    def first_tile(gi):
        return jnp.minimum(gi, n_groups - 1) * grp

    n_blk = grp * per_tile
    lane = lax.broadcasted_iota(jnp.int32, (gt, LANES), 1)
    blk_of_key = lax.broadcasted_iota(jnp.int32, (gt, LANES), 0) // NSA_SLC_LEN
    key_lanes = lax.shift_right_logical(lane, int(math.log2(HEAD_DIM))) == hk % 2
    one_hot = jnp.where(lane == blk_of_key + jnp.where(low_half, HEAD_DIM, 0), 1.0, 0.0).astype(BF16)

    def scores(gi, s_ref):
        j0 = first_tile(gi)
        keys = jnp.where(key_lanes, ks_ref[pl.ds(pl.multiple_of(j0 * t, t), gt), :], one_hot)
        drop = (sel32_ref[pl.ds(pl.multiple_of(j0 * per_tile, n_blk), n_blk), :] - 1.0) * (-NEG)
        drop = jnp.concatenate([drop, jnp.zeros((HEAD_DIM - n_blk, t), F32)], axis=0).astype(BF16)
        rhs_g = [jnp.where(low_half, jnp.concatenate([qg, drop], axis=0), jnp.concatenate([drop, qg], axis=0))
                 for qg in q_heads]
        s_ref[...] = _dot(keys, jnp.concatenate(rhs_g, axis=1))

    def softmax(gi, s_ref, near):
        j0 = first_tile(gi)
        vt1 = _with_ones(vst_ref[:, pl.ds(pl.multiple_of(j0 * t, t), gt)])
        for g in range(NSA_GROUP):
            s = s_ref[:, g * t:(g + 1) * t]
            if near:
                s = s + jnp.concatenate(
                    [bs_ref[g, _bias_tile_index(qi - (gi * grp + u))] for u in range(grp)], axis=0)
            _flash_step(s, vt1, m_ref, acc_ref, g)

    _pipelined_sweep(_far_groups(qi, grp), n_groups, scores, softmax, sa_ref, sb_ref)
    for g in range(NSA_GROUP):
        res_ref[g] = _flash_result(acc_ref, g)

    _flash_init(m_ref, acc_ref)

    def win_tiles(count):
        start = pl.multiple_of((qi - (count - 1)) * t, t)
        s4 = _dot(kw_ref[pl.ds(start, count * t), :], rhs)
        vt1 = _with_ones(vwt_ref[:, pl.ds(start, count * t)])
        for g in range(NSA_GROUP):
            bias = jnp.concatenate([bw_ref[g, count - 1 - u] for u in range(count)], axis=0)
            _flash_step(s4[:, g * t:(g + 1) * t] + bias, vt1, m_ref, acc_ref, g)

    for count in range(1, n_win + 1):
        @pl.when((qi == count - 1) if count < n_win else (qi >= count - 1))
        def _(count=count):
            win_tiles(count)

    gate_ref[...] = _sigmoid(_flip(_eye(LANES), zg_ref[...]))
    mixed = []
    for g in range(NSA_GROUP):
        base = 3 * (NSA_GROUP * hk + g)
        gates = [gate_ref[pl.ds(base + c, 1), :] for c in range(3)]
        o_cmp = ocmp_ref[g * HEAD_DIM:(g + 1) * HEAD_DIM, :].astype(F32)
        mixed.append(gates[0] * o_cmp + gates[1] * res_ref[g] + gates[2] * _flash_result(acc_ref, g))
    o_ref[...] = _flip(_eye(t), jnp.concatenate(mixed, axis=0).astype(BF16)).astype(o_ref.dtype)


def _nsa_attend(z, zt, sel, o_cmp, bias_slc, bias_win, cols):
    b, s, _ = z.shape
    t = ATT_T
    n_slc = sel.shape[2]
    gw = NSA_GROUP * HEAD_DIM
    n_win = bias_win.shape[1]
    n_tiles = bias_slc.shape[1]
    c_slc, c_win, c_gate = cols["k_slc"], cols["k_win"], cols["gate"]
    return pl.pallas_call(
        functools.partial(_nsa_attend_kernel, t=t, n_win=n_win),
        out_shape=jax.ShapeDtypeStruct((b, s, D_MODEL), BF16),
        grid=(b, NSA_KV, s // t),
        in_specs=[pl.BlockSpec((None, t, gw), lambda bi, hk, qi: (bi, qi, hk)),
                  _resident((None, s, LANES), lambda bi, hk, qi: (bi, 0, c_slc + hk // 2)),
                  _resident((None, s, LANES), lambda bi, hk, qi: (bi, 0, c_win + hk // 2)),
                  _resident((HEAD_DIM, s), lambda bi, hk, qi: (hk, bi)),
                  _resident((HEAD_DIM, s), lambda bi, hk, qi: (NSA_KV + hk, bi)),
                  pl.BlockSpec((None, None, n_slc, t), lambda bi, hk, qi: (bi, hk, 0, qi)),
                  pl.BlockSpec((None, None, gw, t), lambda bi, hk, qi: (bi, hk, 0, qi)),
                  pl.BlockSpec((None, t, LANES), lambda bi, hk, qi: (bi, qi, c_gate)),
                  _resident((NSA_GROUP, n_tiles, t, t), lambda bi, hk, qi: (hk, 0, 0, 0)),
                  _resident((NSA_GROUP, n_win, t, t), lambda bi, hk, qi: (hk, 0, 0, 0))],
        out_specs=pl.BlockSpec((None, t, gw), lambda bi, hk, qi: (bi, qi, hk)),
        scratch_shapes=[pltpu.VMEM((NSA_GROUP, 1, t), F32),
                        pltpu.VMEM((NSA_GROUP, HEAD_DIM + DEN_ROWS, t), F32),
                        pltpu.VMEM((NSA_GROUP, HEAD_DIM, t), F32),
                        pltpu.VMEM((LANES, t), F32), pltpu.VMEM((n_slc, t), F32),
                        pltpu.VMEM((NSA_SWEEP_GROUP * t, NSA_GROUP * t), F32),
                        pltpu.VMEM((NSA_SWEEP_GROUP * t, NSA_GROUP * t), F32)],
        compiler_params=_cparams(("parallel", "parallel", "arbitrary")),
        name="nsa_attend",
    )(z, z, z, zt, zt, sel, o_cmp, z, bias_slc, bias_win)


def _nsa_layout(w_in):
    d = w_in.shape[0]
    nq = N_HEADS * HEAD_DIM
    kvw = NSA_KV * HEAD_DIM
    kv = [w_in[:, nq + j * kvw: nq + (j + 1) * kvw] for j in range(6)]
    wg = w_in[:, nq + 6 * kvw:]
    wg = jnp.concatenate([wg, jnp.zeros((d, LANES - wg.shape[1]), w_in.dtype)], axis=-1)
    w = jnp.concatenate([w_in[:, :nq] * (SCALE * LOG2E), kv[2], kv[4], kv[0], kv[1], wg], axis=-1)
    wt = jnp.concatenate([kv[3], kv[5]], axis=-1).T
    cols = {"k_slc": nq // LANES, "k_win": (nq + kvw) // LANES, "k_cmp": nq + 2 * kvw,
            "v_cmp": nq + 3 * kvw, "gate": (nq + 4 * kvw) // LANES}
    return w.astype(BF16), wt.astype(BF16), cols


def _importance_matrix_np(n_slc, r):
    ratio = NSA_SLC_LEN // NSA_CMP_STRIDE
    span = NSA_CMP_LEN // NSA_CMP_STRIDE
    coef = np.convolve(np.ones(ratio), np.ones(span))
    cmat = np.zeros((n_slc, r), np.float32)
    for j in range(n_slc):
        for o, cf in enumerate(coef):
            c = ratio * j - o
            if c >= 0:
                cmat[j, c] = cf
    return cmat


def _nsa_mixer(h, g, b, s, w_in, cmp_pos, cmp_w1, cmp_w2, tab, bias_causal):
    w, wt, cols = _nsa_layout(w_in)
    z, zt = _norm_matmul(h, g, w, wt)
    z = z.reshape(b, s, -1)
    kvw = NSA_KV * HEAD_DIM
    r = s // NSA_CMP_STRIDE

    def rows_of_16(x):
        x = x.reshape(b, -1, NSA_KV, HEAD_DIM).transpose(2, 0, 1, 3)
        return x.reshape(NSA_KV, b, -1, NSA_CMP_STRIDE * HEAD_DIM)

    kv_c = jnp.stack([z[..., cols["k_cmp"]:cols["k_cmp"] + kvw], z[..., cols["v_cmp"]:cols["v_cmp"] + kvw]])
    nxt = jnp.concatenate([kv_c[:, :, NSA_CMP_STRIDE:], jnp.zeros_like(kv_c[:, :, :NSA_CMP_STRIDE])], axis=2)
    blocks = jnp.stack([rows_of_16(kv_c[0]), rows_of_16(kv_c[1])])
    blocks_next = jnp.stack([rows_of_16(nxt[0]), rows_of_16(nxt[1])])
    flat = NSA_CMP_LEN * HEAD_DIM
    cmp_tok, cmp_feat = _compress(blocks, blocks_next, cmp_w1.astype(BF16),
                                  cmp_pos.reshape(2, 1, flat).astype(BF16), cmp_w2.astype(BF16))

    n_slc = s // NSA_SLC_LEN
    cmat = jnp.asarray(_importance_matrix_np(n_slc, r), BF16)
    o_cmp, sel = _nsa_select(z, cmp_tok[0], cmp_feat[1], cmat, min(NSA_TOP_N, n_slc))
    bias_win = _build_bias_tiles(tab, _causal_bucket_tiles(ATT_T, -(-NSA_WINDOW // ATT_T) + 1, NSA_WINDOW))
    return _nsa_attend(z, zt, sel, o_cmp, bias_causal, bias_win, cols)


def _scaled_qkv(w_in, scale=SCALE):
    return jnp.concatenate([w_in[:, :D_MODEL] * scale, w_in[:, D_MODEL:]], axis=1).astype(BF16)


def kernel(x, rel_bias, norm_gains, final_gain, a_w_in, a_lambda, a_subln, a_w_out, b_w_in, b_w_out,
           c_w_in, c_w_out, d_w_in, d_cmp_pos, d_cmp_w1, d_cmp_w2, d_w_out, ffn_w_gate, ffn_w_up,
           ffn_w_down, moe_router, moe_w_gate, moe_w_up, moe_w_down):
    b, s, d = x.shape
    assert d == D_MODEL and s % (B_PAIRS[-1][0]) == 0 and s >= (N_NEAR + 1) * ATT_T
    assert _t5_bucket_np(np.array([(N_NEAR - 1) * ATT_T + 1]))[0] == NUM_BUCKETS - 1
    n = b * s
    h = x.reshape(n, d)
    tab = rel_bias.T.astype(F32)
    tab_rel = (tab - tab[:, NUM_BUCKETS - 1:]) * LOG2E
    bias_causal = _build_bias_tiles(tab_rel, _sweep_bucket_tiles(ATT_T))

    w = _scaled_qkv(a_w_in[0], SCALE * LOG2E)
    z, zt = _norm_matmul(h, norm_gains[0, 0], w[:, :2 * d], w[:, 2 * d:].T)
    lam_init = 0.8 - 0.6 * math.exp(-0.3 * 0)
    o = _diff_attention(z.reshape(b, s, -1), zt, a_lambda[0].astype(F32), a_subln[0].astype(F32),
                        bias_causal, lam_init)
    h = _proj_residual(o.reshape(n, d), a_w_out[0].astype(BF16), h)
    h = _ffn(h, norm_gains[0, 1], ffn_w_gate[0].astype(BF16), ffn_w_up[0].astype(BF16),
             ffn_w_down[0].astype(BF16))

    z = _norm_matmul(h, norm_gains[1, 0], _scaled_qkv(b_w_in[0])).reshape(b, s, -1)
    outs, lses = [], []
    for _, dil in B_PAIRS:
        bias = _build_bias_tiles(tab, _dilated_bucket_tiles(dil))
        o, lse = _dilated_group(z, bias, dil)
        outs.append(o)
        lses.append(lse)
    h = _dil_combine_proj(outs, lses, b_w_out[0].astype(BF16), h)
    h = _moe(h, norm_gains[1, 1], moe_router[0], moe_w_gate[0].astype(BF16), moe_w_up[0].astype(BF16),
             moe_w_down[0].astype(BF16), final_gain, False)

    w = _scaled_qkv(c_w_in[0])
    z, zt = _norm_matmul(h, norm_gains[2, 0], w[:, :2 * d], w[:, 2 * d:].T)
    o = _stick_attention(z.reshape(b, s, -1), zt)
    h = _proj_residual(o.reshape(n, d), c_w_out[0].astype(BF16), h)
    h = _ffn(h, norm_gains[2, 1], ffn_w_gate[1].astype(BF16), ffn_w_up[1].astype(BF16),
             ffn_w_down[1].astype(BF16))

    o = _nsa_mixer(h, norm_gains[3, 0], b, s, d_w_in[0], d_cmp_pos[0], d_cmp_w1[0], d_cmp_w2[0], tab_rel,
                   bias_causal)
    h = _proj_residual(o.reshape(n, d), d_w_out[0].astype(BF16), h)
    h = _moe(h, norm_gains[3, 1], moe_router[1], moe_w_gate[1].astype(BF16), moe_w_up[1].astype(BF16),
             moe_w_down[1].astype(BF16), final_gain, True)
    return h.reshape(b, s, d)
```

```python
import functools
import math

import numpy as np
import jax
import jax.numpy as jnp
from jax import lax
from jax.experimental import pallas as pl
from jax.experimental.pallas import tpu as pltpu

F32 = jnp.float32
BF16 = jnp.bfloat16

D_MODEL = 1024
HEAD_DIM = 64
N_HEADS = 16
LANES = 128
B_PAIRS = ((128, 1), (512, 4), (2048, 16))
DIL_W = 128
NSA_KV = 4
NSA_GROUP = 4
NSA_CMP_LEN = 32
NSA_CMP_STRIDE = 16
NSA_CMP_HID = 256
NSA_SLC_LEN = 64
NSA_TOP_N = 16
NSA_WINDOW = 512
NUM_BUCKETS = 32
MAX_EXACT = 16
REL_MAX_DIST = 2048
N_EXPERTS = 8
EPS = 1e-6
SCALE = HEAD_DIM ** -0.5
NEG = -1e30
MASKED_BUCKET = NUM_BUCKETS

ATT_T = 256
N_NEAR = 7
ATT_Q_TILES = 2
ATT_GROUP = 2
NSA_SWEEP_GROUP = 2
CMP_ROW_VARIANTS = 4
DEN_ROWS = 16
LOG2E = math.log2(math.e)
SB_T = 256
SB_CUTOFF = -104.0
VMEM_LIMIT = 56 << 20


def _cparams(sem):
    return pltpu.CompilerParams(dimension_semantics=sem, vmem_limit_bytes=VMEM_LIMIT)


def _resident(block_shape, index_map):
    return pl.BlockSpec(block_shape, index_map, pipeline_mode=pl.Buffered(1))


def _nt_dot(a, b):
    return lax.dot_general(a, b, (((1,), (1,)), ((), ())), preferred_element_type=F32)


def _dot(a, b):
    return jnp.dot(a, b, preferred_element_type=F32)


def _split2(x):
    hi = x.astype(BF16)
    lo = (x - hi.astype(F32)).astype(BF16)
    return hi, lo


def _split3(x):
    hi = x.astype(BF16)
    rem = x - hi.astype(F32)
    mid = rem.astype(BF16)
    lo = (rem - mid.astype(F32)).astype(BF16)
    return hi, mid, lo


def _rms(x, g):
    ms = jnp.mean(x * x, axis=-1, keepdims=True)
    return x * lax.rsqrt(ms + EPS) * g


def _sigmoid(x):
    return 1.0 / (1.0 + jnp.exp(-x))


def _eye(n):
    r = lax.broadcasted_iota(jnp.int32, (n, n), 0)
    c = lax.broadcasted_iota(jnp.int32, (n, n), 1)
    return jnp.where(r == c, 1.0, 0.0).astype(BF16)


def _flip(eye, x):
    return _nt_dot(eye, x)


def _pad_rows(x, top):
    zero = jnp.zeros_like(x)
    return jnp.concatenate([x, zero] if top else [zero, x], axis=0)


def _t5_bucket_np(dist):
    n = np.maximum(dist, 0)
    ratio = np.log(np.maximum(n, 1).astype(np.float32) / np.float32(MAX_EXACT)) / np.float32(
        math.log(REL_MAX_DIST / MAX_EXACT))
    large = np.minimum(MAX_EXACT + (ratio * np.float32(NUM_BUCKETS - MAX_EXACT)).astype(np.int32),
                       NUM_BUCKETS - 1)
    return np.where(n < MAX_EXACT, n, large).astype(np.int32)


def _causal_bucket_tiles(t, n_tiles, max_dist=None):
    i = np.arange(t)[None, None, :]
    j = np.arange(t)[None, :, None]
    dist = np.arange(n_tiles)[:, None, None] * t + i - j
    ok = dist >= 0
    if max_dist is not None:
        ok = ok & (dist < max_dist)
    return np.where(ok, _t5_bucket_np(dist), MASKED_BUCKET).astype(np.int32)


def _sweep_bucket_tiles(t):
    far = np.full((1, t, t), NUM_BUCKETS - 1, np.int32)
    future = np.full((1, t, t), MASKED_BUCKET, np.int32)
    return np.concatenate([_causal_bucket_tiles(t, N_NEAR), far, future], axis=0)


def _bias_tile_index(delta):
    return jnp.where(delta < 0, N_NEAR + 1, jnp.minimum(delta, N_NEAR))


def _sweep_bias(bias_ref, col, first_q, nq, first_key, count):
    return jnp.concatenate(
        [jnp.concatenate([bias_ref[col, _bias_tile_index(first_q + a - (first_key + u))] for a in range(nq)], axis=1)
         for u in range(count)], axis=0)


def _dilated_bucket_tiles(dilation):
    w = DIL_W
    steps = w + np.arange(w)[None, :] - np.arange(2 * w)[:, None]
    ok = (steps >= 0) & (steps <= w)
    bkt = np.where(ok, _t5_bucket_np(dilation * steps), MASKED_BUCKET)
    first = np.where(np.arange(2 * w)[:, None] < w, MASKED_BUCKET, bkt)
    return np.stack([first, bkt]).astype(np.int32)


def _bias_build_kernel(tab_ref, bkt_ref, o_ref):
    c = pl.program_id(0)
    bkt = bkt_ref[0]
    out = jnp.full(bkt.shape, NEG, F32)
    for b in range(NUM_BUCKETS):
        out = jnp.where(bkt == b, tab_ref[c, b], out)
    o_ref[0, 0] = out


def _build_bias_tiles(tab, buckets):
    n, r, c = buckets.shape
    return pl.pallas_call(
        _bias_build_kernel,
        out_shape=jax.ShapeDtypeStruct((N_HEADS, n, r, c), F32),
        grid=(N_HEADS, n),
        in_specs=[pl.BlockSpec(memory_space=pltpu.SMEM),
                  pl.BlockSpec((1, r, c), lambda h, i: (i, 0, 0))],
        out_specs=pl.BlockSpec((1, 1, r, c), lambda h, i: (h, i, 0, 0)),
        compiler_params=_cparams(("arbitrary", "arbitrary")),
        name="bias_tiles",
    )(tab, jnp.asarray(buckets))


def _norm_matmul_kernel(x_ref, g_ref, w_ref, *refs, chunk, feature_major):
    xn = _rms(x_ref[...], g_ref[...]).astype(BF16)
    o_ref = refs[1] if feature_major else refs[0]
    nout = o_ref.shape[1]
    for c0 in range(0, nout, chunk):
        c1 = min(c0 + chunk, nout)
        o_ref[:, c0:c1] = _dot(xn, w_ref[:, c0:c1]).astype(o_ref.dtype)
    if feature_major:
        wt_ref, ot_ref = refs[0], refs[2]
        nt = ot_ref.shape[0]
        for c0 in range(0, nt, chunk):
            c1 = min(c0 + chunk, nt)
            ot_ref[c0:c1, :] = _nt_dot(wt_ref[c0:c1, :], xn).astype(ot_ref.dtype)


def _norm_matmul(x, g, w, wt=None, tm=512):
    n, d = x.shape
    nout = w.shape[1]
    in_specs = [pl.BlockSpec((tm, d), lambda i: (i, 0)),
                pl.BlockSpec((1, d), lambda i: (0, 0)),
                pl.BlockSpec((d, nout), lambda i: (0, 0))]
    out_shape = [jax.ShapeDtypeStruct((n, nout), BF16)]
    out_specs = [pl.BlockSpec((tm, nout), lambda i: (i, 0))]
    args = [x, g.reshape(1, d), w]
    if wt is not None:
        nt = wt.shape[0]
        in_specs.append(pl.BlockSpec((nt, d), lambda i: (0, 0)))
        out_shape.append(jax.ShapeDtypeStruct((nt, n), BF16))
        out_specs.append(pl.BlockSpec((nt, tm), lambda i: (0, i)))
        args.append(wt)
    out = pl.pallas_call(
        functools.partial(_norm_matmul_kernel, chunk=512, feature_major=wt is not None),
        out_shape=tuple(out_shape),
        grid=(n // tm,),
        in_specs=in_specs,
        out_specs=tuple(out_specs),
        compiler_params=_cparams(("parallel",)),
        name="norm_matmul",
    )(*args)
    return out if wt is not None else out[0]


def _proj_res_kernel(a_ref, w_ref, h_ref, o_ref):
    o_ref[...] = h_ref[...] + _dot(a_ref[...], w_ref[...])


def _proj_residual(a, w, h, tm=512):
    n, d = h.shape
    k = a.shape[1]
    return pl.pallas_call(
        _proj_res_kernel,
        out_shape=jax.ShapeDtypeStruct((n, d), F32),
        grid=(n // tm,),
        in_specs=[pl.BlockSpec((tm, k), lambda i: (i, 0)),
                  pl.BlockSpec((k, d), lambda i: (0, 0)),
                  pl.BlockSpec((tm, d), lambda i: (i, 0))],
        out_specs=pl.BlockSpec((tm, d), lambda i: (i, 0)),
        compiler_params=_cparams(("parallel",)),
        name="proj_residual",
    )(a, w, h)


def _dil_combine_proj_kernel(o1, o2, o3, l1, l2, l3, w_ref, h_ref, o_ref):
    ls = [l1[...], l2[...], l3[...]]
    m = jnp.maximum(jnp.maximum(ls[0], ls[1]), ls[2])
    es = [jnp.exp(l - m) for l in ls]
    den = es[0] + es[1] + es[2]
    a = (es[0] * o1[...].astype(F32) + es[1] * o2[...].astype(F32) + es[2] * o3[...].astype(F32)) / den
    o_ref[...] = h_ref[...] + _dot(a.astype(BF16), w_ref[...])


def _dil_combine_proj(outs, lses, w, h, tm=512):
    n, d = h.shape
    row = pl.BlockSpec((tm, d), lambda i: (i, 0))
    return pl.pallas_call(
        _dil_combine_proj_kernel,
        out_shape=jax.ShapeDtypeStruct((n, d), F32),
        grid=(n // tm,),
        in_specs=[row] * 6 + [pl.BlockSpec((d, d), lambda i: (0, 0)), row],
        out_specs=row,
        compiler_params=_cparams(("parallel",)),
        name="dilated_combine_proj",
    )(*outs, *lses, w, h)


def _ffn_kernel(h_ref, g_ref, wg_ref, wu_ref, wd_ref, o_ref, xn_ref, acc_ref):
    f = pl.program_id(1)

    @pl.when(f == 0)
    def _():
        xn_ref[...] = _rms(h_ref[...], g_ref[...]).astype(BF16)
        acc_ref[...] = jnp.zeros_like(acc_ref)

    xn = xn_ref[...]
    gate = _dot(xn, wg_ref[...])
    up = _dot(xn, wu_ref[...])
    act = (gate * _sigmoid(gate) * up).astype(BF16)
    acc_ref[...] += _dot(act, wd_ref[...])

    @pl.when(f == pl.num_programs(1) - 1)
    def _():
        o_ref[...] = h_ref[...] + acc_ref[...]


def _ffn(h, g, wg, wu, wd, tm=512, n_f=2):
    n, d = h.shape
    dff = wg.shape[1]
    tf = dff // n_f
    return pl.pallas_call(
        _ffn_kernel,
        out_shape=jax.ShapeDtypeStruct((n, d), F32),
        grid=(n // tm, n_f),
        in_specs=[pl.BlockSpec((tm, d), lambda i, f: (i, 0)),
                  pl.BlockSpec((1, d), lambda i, f: (0, 0)),
                  pl.BlockSpec((d, tf), lambda i, f: (0, f)),
                  pl.BlockSpec((d, tf), lambda i, f: (0, f)),
                  pl.BlockSpec((tf, d), lambda i, f: (f, 0))],
        out_specs=pl.BlockSpec((tm, d), lambda i, f: (i, 0)),
        scratch_shapes=[pltpu.VMEM((tm, d), BF16), pltpu.VMEM((tm, d), F32)],
        compiler_params=_cparams(("parallel", "arbitrary")),
        name="ffn",
    )(h, g.reshape(1, d), wg, wu, wd)


def _moe_kernel(h_ref, g_ref, rh_ref, rl_ref, wg_ref, wu_ref, wd_ref, fg_ref, o_ref,
                xn_ref, comb_ref, acc_ref, *, final_norm):
    e = pl.program_id(1)
    col = lax.broadcasted_iota(jnp.int32, comb_ref.shape, 1)

    @pl.when(e == 0)
    def _():
        xn = _rms(h_ref[...], g_ref[...])
        xh, xl = _split2(xn)
        xn_ref[...] = xh
        logits = _dot(xh, rh_ref[...]) + _dot(xh, rl_ref[...]) + _dot(xl, rh_ref[...])
        colf = col.astype(F32)
        lg = jnp.where(col < N_EXPERTS, logits, NEG)
        m1 = jnp.max(lg, axis=-1, keepdims=True)
        i1 = jnp.min(jnp.where(lg == m1, colf, float(LANES)), axis=-1, keepdims=True)
        lg2 = jnp.where(colf == i1, NEG, lg)
        m2 = jnp.max(lg2, axis=-1, keepdims=True)
        i2 = jnp.min(jnp.where(lg2 == m2, colf, float(LANES)), axis=-1, keepdims=True)
        e2 = jnp.exp(m2 - m1)
        g1 = 1.0 / (1.0 + e2)
        g2 = e2 / (1.0 + e2)
        comb_ref[...] = jnp.where(colf == i1, g1, 0.0) + jnp.where(colf == i2, g2, 0.0)
        acc_ref[...] = jnp.zeros_like(acc_ref)

    xn = xn_ref[...]
    gate = _dot(xn, wg_ref[0])
    up = _dot(xn, wu_ref[0])
    ce = jnp.sum(jnp.where(col == e, comb_ref[...], 0.0), axis=-1, keepdims=True)
    act = (gate * _sigmoid(gate) * up * ce).astype(BF16)
    acc_ref[...] += _dot(act, wd_ref[0])

    @pl.when(e == pl.num_programs(1) - 1)
    def _():
        y = h_ref[...] + acc_ref[...]
        if final_norm:
            y = _rms(y, fg_ref[...])
        o_ref[...] = y


def _moe(h, g, router, wg, wu, wd, final_gain, final_norm, tm=512):
    n, d = h.shape
    ne, _, dfe = wg.shape
    rpad = jnp.zeros((d, LANES), F32).at[:, :ne].set(router)
    rh, rl = _split2(rpad)
    return pl.pallas_call(
        functools.partial(_moe_kernel, final_norm=final_norm),
        out_shape=jax.ShapeDtypeStruct((n, d), F32),
        grid=(n // tm, ne),
        in_specs=[pl.BlockSpec((tm, d), lambda i, e: (i, 0)),
                  pl.BlockSpec((1, d), lambda i, e: (0, 0)),
                  pl.BlockSpec((d, LANES), lambda i, e: (0, 0)),
                  pl.BlockSpec((d, LANES), lambda i, e: (0, 0)),
                  pl.BlockSpec((1, d, dfe), lambda i, e: (e, 0, 0)),
                  pl.BlockSpec((1, d, dfe), lambda i, e: (e, 0, 0)),
                  pl.BlockSpec((1, dfe, d), lambda i, e: (e, 0, 0)),
                  pl.BlockSpec((1, d), lambda i, e: (0, 0))],
        out_specs=pl.BlockSpec((tm, d), lambda i, e: (i, 0)),
        scratch_shapes=[pltpu.VMEM((tm, d), BF16), pltpu.VMEM((tm, LANES), F32),
                        pltpu.VMEM((tm, d), F32)],
        compiler_params=_cparams(("parallel", "arbitrary")),
        name="moe",
    )(h, g.reshape(1, d), rh, rl, wg, wu, wd, final_gain.reshape(1, d))


def _with_ones(vt):
    return jnp.concatenate([vt, jnp.ones((DEN_ROWS, vt.shape[1]), vt.dtype)], axis=0)


def _flash_step(s, vt1, m_ref, acc_ref, idx):
    m_old = m_ref[idx]
    m_new = jnp.maximum(m_old, jnp.max(s, axis=0, keepdims=True))
    p = jnp.exp2(s - m_new).astype(BF16)
    acc_ref[idx] = jnp.exp2(m_old - m_new) * acc_ref[idx] + _dot(vt1, p)
    m_ref[idx] = m_new


def _flash_result(acc_ref, idx):
    acc = acc_ref[idx]
    nf = acc.shape[0] - DEN_ROWS
    return acc[:nf] / acc[nf:nf + 1]


def _far_groups(qi, group):
    return lax.shift_right_logical(jnp.maximum(qi + 1 - N_NEAR, 0), int(math.log2(group)))


def _pipelined_sweep(n_far, n_groups, scores, softmax, sa_ref, sb_ref):
    scores(0, sa_ref)

    def pairs(near):
        def pair(k, carry):
            scores(2 * k + 1, sb_ref)
            softmax(2 * k, sa_ref, near)
            scores(2 * k + 2, sa_ref)
            softmax(2 * k + 1, sb_ref, near)
            return carry
        return pair

    far_pairs = lax.shift_right_logical(n_far, 1)
    lax.fori_loop(0, far_pairs, pairs(False), 0)
    lax.fori_loop(far_pairs, lax.shift_right_logical(n_groups + 1, 1), pairs(True), 0)


def _flash_init(m_ref, acc_ref):
    m_ref[...] = jnp.full(m_ref.shape, NEG, F32)
    acc_ref[...] = jnp.zeros_like(acc_ref)


def _diff_attn_kernel(lam_ref, q_ref, k_ref, vt_ref, bias_ref, g_ref, o_ref, m_ref, acc_ref, sa_ref, sb_ref,
                      *, t, nq, lam_init):
    qi = pl.program_id(2)
    tq = nq * t
    qt = _flip(_eye(LANES), q_ref[...]).astype(BF16)
    rhs = jnp.concatenate([_pad_rows(qt[:HEAD_DIM], True), _pad_rows(qt[HEAD_DIM:], False)], axis=1)
    _flash_init(m_ref, acc_ref)
    gt = ATT_GROUP * t
    first_q = qi * nq
    n_groups = lax.shift_right_logical(first_q + nq - 1 + ATT_GROUP, int(math.log2(ATT_GROUP)))

    def key_start(gi):
        return pl.multiple_of(jnp.minimum(gi, n_groups - 1) * gt, gt)

    def scores(gi, s_ref):
        s_ref[...] = _dot(k_ref[pl.ds(key_start(gi), gt), :], rhs)

    def softmax(gi, s_ref, near):
        vt1 = _with_ones(vt_ref[:, pl.ds(key_start(gi), gt)])
        for mi in range(2):
            s = s_ref[:, mi * tq:(mi + 1) * tq]
            if near:
                s = s + _sweep_bias(bias_ref, mi, first_q, nq, gi * ATT_GROUP, ATT_GROUP)
            _flash_step(s, vt1, m_ref, acc_ref, mi)

    _pipelined_sweep(_far_groups(first_q, ATT_GROUP), n_groups, scores, softmax, sa_ref, sb_ref)

    lam = lam_ref[...]
    lam_full = (jnp.exp(jnp.sum(lam[0:1] * lam[1:2], axis=-1, keepdims=True))
                - jnp.exp(jnp.sum(lam[2:3] * lam[3:4], axis=-1, keepdims=True)) + lam_init)
    o = _flash_result(acc_ref, 0) - lam_full * _flash_result(acc_ref, 1)
    ms = jnp.mean(o * o, axis=0, keepdims=True)
    y = (o * lax.rsqrt(ms + EPS) * g_ref[...] * (1.0 - lam_init)).astype(BF16)
    o_ref[...] = _flip(_eye(tq), y).astype(o_ref.dtype)


def _diff_attention(z, zt, lam, subln, bias_tiles, lam_init):
    b, s, _ = z.shape
    t = ATT_T
    tq = ATT_Q_TILES * t
    nh = N_HEADS // 2
    n_tiles = bias_tiles.shape[1]
    return pl.pallas_call(
        functools.partial(_diff_attn_kernel, t=t, nq=ATT_Q_TILES, lam_init=lam_init),
        out_shape=jax.ShapeDtypeStruct((b, s, D_MODEL), BF16),
        grid=(b, nh, s // tq),
        in_specs=[pl.BlockSpec((4, HEAD_DIM), lambda bi, h, qi: (0, 0)),
                  pl.BlockSpec((None, tq, LANES), lambda bi, h, qi: (bi, qi, h)),
                  _resident((None, s, LANES), lambda bi, h, qi: (bi, 0, nh + h)),
                  _resident((LANES, s), lambda bi, h, qi: (h, bi)),
                  _resident((2, n_tiles, t, t), lambda bi, h, qi: (h, 0, 0, 0)),
                  pl.BlockSpec((LANES, 1), lambda bi, h, qi: (0, 0))],
        out_specs=pl.BlockSpec((None, tq, LANES), lambda bi, h, qi: (bi, qi, h)),
        scratch_shapes=[pltpu.VMEM((2, 1, tq), F32), pltpu.VMEM((2, LANES + DEN_ROWS, tq), F32),
                        pltpu.VMEM((ATT_GROUP * t, 2 * tq), F32), pltpu.VMEM((ATT_GROUP * t, 2 * tq), F32)],
        compiler_params=_cparams(("parallel", "parallel", "arbitrary")),
        name="diff_attention",
    )(lam, z, z, zt, bias_tiles, subln.reshape(LANES, 1))


def _dilated_kernel(q_ref, kp_ref, kc_ref, vp_ref, vc_ref, bias_ref, o_ref, lse_ref):
    n = pl.program_id(2)
    variant = jnp.minimum(n, 1)
    w = q_ref.shape[0]
    eye = _eye(LANES)
    eye_w = _eye(w)
    pairs = range(N_HEADS // 2)
    cols = [slice(hp * LANES, (hp + 1) * LANES) for hp in pairs]
    qts = [_flip(eye, q_ref[:, c]).astype(BF16) for c in cols]
    kcats = [jnp.concatenate([kp_ref[:, c], kc_ref[:, c]], axis=0) for c in cols]
    vts = [_flip(eye, jnp.concatenate([vp_ref[:, c], vc_ref[:, c]], axis=0)).astype(BF16) for c in cols]
    s2s = [_dot(kcats[hp], jnp.concatenate([_pad_rows(qts[hp][:HEAD_DIM], True),
                                            _pad_rows(qts[hp][HEAD_DIM:], False)], axis=1))
           for hp in pairs]
    ps, lses = [], []
    for hp in pairs:
        for sub in range(2):
            s = s2s[hp][:, sub * w:(sub + 1) * w] + bias_ref[2 * hp + sub, variant]
            m = jnp.max(s, axis=0, keepdims=True)
            e = jnp.exp(s - m)
            den = jnp.sum(e, axis=0, keepdims=True)
            ps.append((e * (1.0 / den)).astype(BF16))
            lses.append(jnp.broadcast_to(m + jnp.log(den), (HEAD_DIM, w)))
    outs = [_dot(vts[hp][sub * HEAD_DIM:(sub + 1) * HEAD_DIM], ps[2 * hp + sub])
            for hp in pairs for sub in range(2)]
    for hp in pairs:
        pair_out = jnp.concatenate(outs[2 * hp:2 * hp + 2], axis=0).astype(BF16)
        o_ref[:, cols[hp]] = _flip(eye_w, pair_out).astype(o_ref.dtype)
        pair_lse = jnp.concatenate(lses[2 * hp:2 * hp + 2], axis=0)
        lse_ref[:, cols[hp]] = sum(_flip(eye_w, term) for term in _split3(pair_lse))


def _dilated_group(z, bias, dilation):
    b, s, c3 = z.shape
    w = DIL_W
    m_len = s // dilation
    nb = m_len // w
    zv = z.reshape(b, m_len, dilation * c3)
    blk = lambda sect, prev: pl.BlockSpec(
        (None, w, D_MODEL),
        (lambda bi, r, n: (bi, jnp.maximum(n - 1, 0), 3 * r + sect)) if prev
        else (lambda bi, r, n: (bi, n, 3 * r + sect)))
    out_blk = pl.BlockSpec((None, w, D_MODEL), lambda bi, r, n: (bi, n, r))
    o, lse = pl.pallas_call(
        _dilated_kernel,
        out_shape=(jax.ShapeDtypeStruct((b, m_len, dilation * D_MODEL), BF16),
                   jax.ShapeDtypeStruct((b, m_len, dilation * D_MODEL), F32)),
        grid=(b, dilation, nb),
        in_specs=[blk(0, False), blk(1, True), blk(1, False), blk(2, True), blk(2, False),
                  pl.BlockSpec((N_HEADS, 2, 2 * w, w), lambda bi, r, n: (0, 0, 0, 0))],
        out_specs=(out_blk, out_blk),
        compiler_params=_cparams(("parallel", "parallel", "arbitrary")),
        name=f"dilated_d{dilation}",
    )(zv, zv, zv, zv, zv, bias)
    return o.reshape(b * s, D_MODEL), lse.reshape(b * s, D_MODEL)


def _stick_kernel(q_ref, k_ref, vt_ref, o_ref, acc_ref, carry_ref, *, t):
    qi = pl.program_id(2)
    qt = _flip(_eye(LANES), q_ref[...]).astype(BF16)
    rhs = jnp.concatenate([_pad_rows(qt[:HEAD_DIM], True), _pad_rows(qt[HEAD_DIM:], False)], axis=1)
    row = lax.broadcasted_iota(jnp.int32, (t, t), 0)
    col = lax.broadcasted_iota(jnp.int32, (t, t), 1)
    after = jnp.where(col > row, 1.0, 0.0).astype(BF16)
    before = jnp.concatenate([row < col, row < col], axis=1)
    acc_ref[...] = jnp.zeros_like(acc_ref)
    carry_ref[...] = jnp.zeros_like(carry_ref)

    def block(j, masked):
        start = pl.multiple_of(j * t, t)
        z = _dot(k_ref[pl.ds(start, t), :], rhs)
        log_keep = -(jnp.maximum(z, 0.0) + jnp.log(1.0 + jnp.exp(-jnp.abs(z))))
        if masked:
            log_keep = jnp.where(before, log_keep, 0.0)
        hi, lo = _split2(log_keep)
        later = _dot(after, hi) + _dot(after, lo) + carry_ref[...]
        a = jnp.exp(z + log_keep + later)
        if masked:
            a = jnp.where(before, a, 0.0)
        a = a.astype(BF16)
        for sub in range(2):
            vt = vt_ref[sub * HEAD_DIM:(sub + 1) * HEAD_DIM, pl.ds(start, t)]
            acc_ref[sub] += _dot(vt, a[:, sub * t:(sub + 1) * t])
        carry_ref[...] += jnp.sum(log_keep, axis=0, keepdims=True)

    block(qi, True)

    def cond(c):
        return jnp.logical_and(c[0] >= 0, c[1] >= SB_CUTOFF)

    def body(c):
        block(c[0], False)
        return c[0] - 1, jnp.max(carry_ref[...])

    lax.while_loop(cond, body, (qi - 1, jnp.max(carry_ref[...])))
    out = jnp.concatenate([acc_ref[0], acc_ref[1]], axis=0).astype(BF16)
    o_ref[...] = _flip(_eye(t), out).astype(o_ref.dtype)


def _stick_attention(z, zt):
    b, s, _ = z.shape
    t = SB_T
    nh = N_HEADS // 2
    return pl.pallas_call(
        functools.partial(_stick_kernel, t=t),
        out_shape=jax.ShapeDtypeStruct((b, s, D_MODEL), BF16),
        grid=(b, nh, s // t),
        in_specs=[pl.BlockSpec((None, t, LANES), lambda bi, h, qi: (bi, qi, h)),
                  _resident((None, s, LANES), lambda bi, h, qi: (bi, 0, nh + h)),
                  _resident((LANES, s), lambda bi, h, qi: (h, bi))],
        out_specs=pl.BlockSpec((None, t, LANES), lambda bi, h, qi: (bi, qi, h)),
        scratch_shapes=[pltpu.VMEM((2, HEAD_DIM, t), F32), pltpu.VMEM((1, 2 * t), F32)],
        compiler_params=_cparams(("parallel", "parallel", "arbitrary")),
        name="stick_breaking",
    )(z, z, zt)


def _compress_kernel(a_ref, as_ref, w1_ref, pos_ref, w2_ref, w2t_ref, o_ref, ot_ref):
    half = a_ref.shape[1]
    w1 = w1_ref[...]
    const = _dot(jnp.broadcast_to(pos_ref[...], (8, 2 * half)), w1)[0:1]
    pre = _dot(a_ref[...], w1[:half]) + _dot(as_ref[...], w1[half:]) + const
    hid = 0.5 * pre * (1.0 + jnp.tanh(math.sqrt(2.0 / math.pi) * (pre + 0.044715 * pre * pre * pre)))
    hid = hid.astype(BF16)
    o_ref[...] = _dot(hid, w2_ref[...]).astype(o_ref.dtype)
    ot_ref[...] = _nt_dot(w2t_ref[...], hid).astype(ot_ref.dtype)


def _compress(blocks, blocks_next, w1, pos, w2, tr=256):
    _, nkv, b, r, half = blocks.shape
    tr = min(tr, r)
    w2pad = jnp.concatenate([w2, jnp.zeros_like(w2)], axis=-1)
    w2t = jnp.swapaxes(w2, 1, 2)
    spec = pl.BlockSpec((None, None, None, tr, half), lambda j, hk, bi, ri: (j, hk, bi, ri, 0))
    return pl.pallas_call(
        _compress_kernel,
        out_shape=(jax.ShapeDtypeStruct((2, nkv, b, r, LANES), BF16),
                   jax.ShapeDtypeStruct((2, nkv, b, HEAD_DIM, r), BF16)),
        grid=(2, nkv, b, r // tr),
        in_specs=[spec, spec,
                  pl.BlockSpec((None, 2 * half, NSA_CMP_HID), lambda j, hk, bi, ri: (j, 0, 0)),
                  pl.BlockSpec((None, 1, 2 * half), lambda j, hk, bi, ri: (j, 0, 0)),
                  pl.BlockSpec((None, NSA_CMP_HID, LANES), lambda j, hk, bi, ri: (j, 0, 0)),
                  pl.BlockSpec((None, HEAD_DIM, NSA_CMP_HID), lambda j, hk, bi, ri: (j, 0, 0))],
        out_specs=(pl.BlockSpec((None, None, None, tr, LANES), lambda j, hk, bi, ri: (j, hk, bi, ri, 0)),
                   pl.BlockSpec((None, None, None, HEAD_DIM, tr), lambda j, hk, bi, ri: (j, hk, bi, 0, ri))),
        compiler_params=_cparams(("parallel", "parallel", "parallel", "arbitrary")),
        name="nsa_compress",
    )(blocks, blocks_next, w1, pos, w2pad, w2t)


def _nsa_select_kernel(q_ref, kc_ref, vct_ref, cmat_ref, o_ref, sel_ref, imp_ref, *, t, n_sel):
    qi = pl.program_id(2)
    r = kc_ref.shape[0]
    n_slc = sel_ref.shape[0]
    qt = _flip(_eye(NSA_GROUP * HEAD_DIM), q_ref[...]).astype(BF16)
    tpos = qi * t + lax.broadcasted_iota(jnp.int32, (1, t), 1)
    live = (tpos >= NSA_CMP_LEN - 1).astype(F32)

    def compressed_branch(rows):
        kc = kc_ref[:rows, :]
        vct = vct_ref[:, :rows]
        cidx = lax.broadcasted_iota(jnp.int32, (rows, t), 0)
        seen = NSA_CMP_STRIDE * cidx + (NSA_CMP_LEN - 1) <= tpos
        imp = jnp.zeros((rows, t), F32)
        for g in range(NSA_GROUP):
            rhs = _pad_rows(qt[g * HEAD_DIM:(g + 1) * HEAD_DIM], True)
            sc = jnp.where(seen, _dot(kc, rhs), NEG)
            e = jnp.exp2(sc - jnp.max(sc, axis=0, keepdims=True))
            pc = e * (live / jnp.sum(e, axis=0, keepdims=True))
            o_ref[g * HEAD_DIM:(g + 1) * HEAD_DIM, :] = _dot(vct, pc.astype(BF16)).astype(o_ref.dtype)
            imp = imp + pc
        cmat = cmat_ref[:, :rows]
        imp_ref[...] = sum(_dot(cmat, term) for term in _split3(imp))

    quarter = r // CMP_ROW_VARIANTS
    need = (qi * t + t - NSA_CMP_LEN) // NSA_CMP_STRIDE
    variant = jnp.clip(need // quarter, 0, CMP_ROW_VARIANTS - 1)
    for v in range(CMP_ROW_VARIANTS):
        @pl.when(variant == v)
        def _(v=v):
            compressed_branch((v + 1) * quarter)

    imp_slc = imp_ref[...]
    blk = lax.broadcasted_iota(jnp.int32, (n_slc, t), 0)
    cur = tpos // NSA_SLC_LEN
    forced = (blk == 0) | (blk == cur) | (blk == cur - 1)
    valid = blk * NSA_SLC_LEN <= tpos
    ranked = jnp.logical_and(valid, jnp.logical_not(forced))
    score = jnp.where(ranked, imp_slc, -1.0)
    blkf = blk.astype(F32)
    for _ in range(n_sel - 3):
        top = jnp.max(score, axis=0, keepdims=True)
        first = jnp.min(jnp.where(score == top, blkf, float(n_slc)), axis=0, keepdims=True)
        score = jnp.where(blkf == first, -jnp.inf, score)
    chosen = jnp.logical_or(jnp.logical_and(valid, forced), jnp.logical_and(ranked, score == -jnp.inf))
    sel_ref[...] = jnp.where(chosen, 1.0, 0.0).astype(sel_ref.dtype)


def _nsa_select(z, kc, vct, cmat, n_sel):
    b, s, _ = z.shape
    t = ATT_T
    r = kc.shape[2]
    n_slc = cmat.shape[0]
    gw = NSA_GROUP * HEAD_DIM
    return pl.pallas_call(
        functools.partial(_nsa_select_kernel, t=t, n_sel=n_sel),
        out_shape=(jax.ShapeDtypeStruct((b, NSA_KV, gw, s), BF16),
                   jax.ShapeDtypeStruct((b, NSA_KV, n_slc, s), BF16)),
        grid=(b, NSA_KV, s // t),
        in_specs=[pl.BlockSpec((None, t, gw), lambda bi, hk, qi: (bi, qi, hk)),
                  pl.BlockSpec((None, None, r, LANES), lambda bi, hk, qi: (hk, bi, 0, 0)),
                  pl.BlockSpec((None, None, HEAD_DIM, r), lambda bi, hk, qi: (hk, bi, 0, 0)),
                  pl.BlockSpec((n_slc, r), lambda bi, hk, qi: (0, 0))],
        out_specs=(pl.BlockSpec((None, None, gw, t), lambda bi, hk, qi: (bi, hk, 0, qi)),
                   pl.BlockSpec((None, None, n_slc, t), lambda bi, hk, qi: (bi, hk, 0, qi))),
        scratch_shapes=[pltpu.VMEM((n_slc, t), F32)],
        compiler_params=_cparams(("parallel", "parallel", "arbitrary")),
        name="nsa_select",
    )(z, kc, vct, cmat)


def _nsa_attend_kernel(q_ref, ks_ref, kw_ref, vst_ref, vwt_ref, sel_ref, ocmp_ref, zg_ref, bs_ref, bw_ref,
                       o_ref, m_ref, acc_ref, res_ref, gate_ref, sel32_ref, sa_ref, sb_ref, *, t, nq, n_win):
    hk = pl.program_id(1)
    qi = pl.program_id(2)
    tq = nq * t
    first_q = qi * nq
    per_tile = t // NSA_SLC_LEN
    gw = NSA_GROUP * HEAD_DIM
    qt = _flip(_eye(gw), q_ref[...]).astype(BF16)
    low_half = (hk % 2) == 0
    q_heads = [qt[g * HEAD_DIM:(g + 1) * HEAD_DIM] for g in range(NSA_GROUP)]
    sel32_ref[...] = sel_ref[...].astype(F32)

    _flash_init(m_ref, acc_ref)
    grp = NSA_SWEEP_GROUP
    gt = grp * t
    n_groups = lax.shift_right_logical(first_q + nq - 1 + grp, int(math.log2(grp)))

    def first_tile(gi):
        return jnp.minimum(gi, n_groups - 1) * grp

    n_blk = grp * per_tile
    lane = lax.broadcasted_iota(jnp.int32, (gt, LANES), 1)
    blk_of_key = lax.broadcasted_iota(jnp.int32, (gt, LANES), 0) // NSA_SLC_LEN
    key_lanes = lax.shift_right_logical(lane, int(math.log2(HEAD_DIM))) == hk % 2
    one_hot = jnp.where(lane == blk_of_key + jnp.where(low_half, HEAD_DIM, 0), 1.0, 0.0).astype(BF16)

    def with_extra_rows(extra):
        return jnp.concatenate(
            [jnp.where(low_half, jnp.concatenate([qg, extra], axis=0), jnp.concatenate([extra, qg], axis=0))
             for qg in q_heads], axis=1)

    def scores(gi, s_ref):
        j0 = first_tile(gi)
        keys = jnp.where(key_lanes, ks_ref[pl.ds(pl.multiple_of(j0 * t, t), gt), :], one_hot)
        drop = (sel32_ref[pl.ds(pl.multiple_of(j0 * per_tile, n_blk), n_blk), :] - 1.0) * (-NEG)
        drop = jnp.concatenate([drop, jnp.zeros((HEAD_DIM - n_blk, tq), F32)], axis=0).astype(BF16)
        s_ref[...] = _dot(keys, with_extra_rows(drop))

    def softmax(gi, s_ref, near):
        j0 = first_tile(gi)
        vt1 = _with_ones(vst_ref[:, pl.ds(pl.multiple_of(j0 * t, t), gt)])
        for g in range(NSA_GROUP):
            s = s_ref[:, g * tq:(g + 1) * tq]
            if near:
                s = s + _sweep_bias(bs_ref, g, first_q, nq, gi * grp, grp)
            _flash_step(s, vt1, m_ref, acc_ref, g)

    _pipelined_sweep(_far_groups(first_q, grp), n_groups, scores, softmax, sa_ref, sb_ref)
    for g in range(NSA_GROUP):
        res_ref[g] = _flash_result(acc_ref, g)

    _flash_init(m_ref, acc_ref)
    rhs_win = with_extra_rows(jnp.zeros((HEAD_DIM, tq), BF16))
    w_first = first_q - (n_win - 1)
    for wg in range(-(-(nq + n_win - 1) // grp)):
        j0 = w_first + wg * grp
        start = pl.multiple_of(jnp.maximum(j0, 0) * t, t)
        sa_ref[...] = _dot(kw_ref[pl.ds(start, gt), :], rhs_win)
        vt1 = _with_ones(vwt_ref[:, pl.ds(start, gt)])
        for g in range(NSA_GROUP):
            rows = []
            for u in range(grp):
                tiles = []
                for a in range(nq):
                    delta = first_q + a - (j0 + u)
                    inside = jnp.logical_and(jnp.logical_and(delta >= 0, delta < n_win), j0 + u >= 0)
                    tiles.append(bw_ref[g, jnp.where(inside, delta, n_win)])
                rows.append(jnp.concatenate(tiles, axis=1))
            _flash_step(sa_ref[:, g * tq:(g + 1) * tq] + jnp.concatenate(rows, axis=0), vt1, m_ref, acc_ref, g)

    gate_ref[...] = _sigmoid(_flip(_eye(LANES), zg_ref[...]))
    mixed = []
    for g in range(NSA_GROUP):
        base = 3 * (NSA_GROUP * hk + g)
        gates = [gate_ref[pl.ds(base + c, 1), :] for c in range(3)]
        o_cmp = ocmp_ref[g * HEAD_DIM:(g + 1) * HEAD_DIM, :].astype(F32)
        mixed.append(gates[0] * o_cmp + gates[1] * res_ref[g] + gates[2] * _flash_result(acc_ref, g))
    o_ref[...] = _flip(_eye(tq), jnp.concatenate(mixed, axis=0).astype(BF16)).astype(o_ref.dtype)


def _nsa_attend(z, zt, sel, o_cmp, bias_slc, bias_win, cols):
    b, s, _ = z.shape
    t = ATT_T
    tq = ATT_Q_TILES * t
    n_slc = sel.shape[2]
    gw = NSA_GROUP * HEAD_DIM
    n_win = bias_win.shape[1] - 1
    assert (n_win - 1) % NSA_SWEEP_GROUP == 0 and ATT_Q_TILES % NSA_SWEEP_GROUP == 0
    n_tiles = bias_slc.shape[1]
    c_slc, c_win, c_gate = cols["k_slc"], cols["k_win"], cols["gate"]
    return pl.pallas_call(
        functools.partial(_nsa_attend_kernel, t=t, nq=ATT_Q_TILES, n_win=n_win),
        out_shape=jax.ShapeDtypeStruct((b, s, D_MODEL), BF16),
        grid=(b, NSA_KV, s // tq),
        in_specs=[pl.BlockSpec((None, tq, gw), lambda bi, hk, qi: (bi, qi, hk)),
                  _resident((None, s, LANES), lambda bi, hk, qi: (bi, 0, c_slc + hk // 2)),
                  _resident((None, s, LANES), lambda bi, hk, qi: (bi, 0, c_win + hk // 2)),
                  _resident((HEAD_DIM, s), lambda bi, hk, qi: (hk, bi)),
                  _resident((HEAD_DIM, s), lambda bi, hk, qi: (NSA_KV + hk, bi)),
                  pl.BlockSpec((None, None, n_slc, tq), lambda bi, hk, qi: (bi, hk, 0, qi)),
                  pl.BlockSpec((None, None, gw, tq), lambda bi, hk, qi: (bi, hk, 0, qi)),
                  pl.BlockSpec((None, tq, LANES), lambda bi, hk, qi: (bi, qi, c_gate)),
                  _resident((NSA_GROUP, n_tiles, t, t), lambda bi, hk, qi: (hk, 0, 0, 0)),
                  _resident((NSA_GROUP, n_win + 1, t, t), lambda bi, hk, qi: (hk, 0, 0, 0))],
        out_specs=pl.BlockSpec((None, tq, gw), lambda bi, hk, qi: (bi, qi, hk)),
        scratch_shapes=[pltpu.VMEM((NSA_GROUP, 1, tq), F32),
                        pltpu.VMEM((NSA_GROUP, HEAD_DIM + DEN_ROWS, tq), F32),
                        pltpu.VMEM((NSA_GROUP, HEAD_DIM, tq), F32),
                        pltpu.VMEM((LANES, tq), F32), pltpu.VMEM((n_slc, tq), F32),
                        pltpu.VMEM((NSA_SWEEP_GROUP * t, NSA_GROUP * tq), F32),
                        pltpu.VMEM((NSA_SWEEP_GROUP * t, NSA_GROUP * tq), F32)],
        compiler_params=_cparams(("parallel", "parallel", "arbitrary")),
        name="nsa_attend",
    )(z, z, z, zt, zt, sel, o_cmp, z, bias_slc, bias_win)


def _nsa_layout(w_in):
    d = w_in.shape[0]
    nq = N_HEADS * HEAD_DIM
    kvw = NSA_KV * HEAD_DIM
    kv = [w_in[:, nq + j * kvw: nq + (j + 1) * kvw] for j in range(6)]
    wg = w_in[:, nq + 6 * kvw:]
    wg = jnp.concatenate([wg, jnp.zeros((d, LANES - wg.shape[1]), w_in.dtype)], axis=-1)
    w = jnp.concatenate([w_in[:, :nq] * (SCALE * LOG2E), kv[2], kv[4], kv[0], kv[1], wg], axis=-1)
    wt = jnp.concatenate([kv[3], kv[5]], axis=-1).T
    cols = {"k_slc": nq // LANES, "k_win": (nq + kvw) // LANES, "k_cmp": nq + 2 * kvw,
            "v_cmp": nq + 3 * kvw, "gate": (nq + 4 * kvw) // LANES}
    return w.astype(BF16), wt.astype(BF16), cols


def _importance_matrix_np(n_slc, r):
    ratio = NSA_SLC_LEN // NSA_CMP_STRIDE
    span = NSA_CMP_LEN // NSA_CMP_STRIDE
    coef = np.convolve(np.ones(ratio), np.ones(span))
    cmat = np.zeros((n_slc, r), np.float32)
    for j in range(n_slc):
        for o, cf in enumerate(coef):
            c = ratio * j - o
            if c >= 0:
                cmat[j, c] = cf
    return cmat


def _nsa_mixer(h, g, b, s, w_in, cmp_pos, cmp_w1, cmp_w2, tab, bias_causal):
    w, wt, cols = _nsa_layout(w_in)
    z, zt = _norm_matmul(h, g, w, wt)
    z = z.reshape(b, s, -1)
    kvw = NSA_KV * HEAD_DIM
    r = s // NSA_CMP_STRIDE

    def rows_of_16(x):
        x = x.reshape(b, -1, NSA_KV, HEAD_DIM).transpose(2, 0, 1, 3)
        return x.reshape(NSA_KV, b, -1, NSA_CMP_STRIDE * HEAD_DIM)

    kv_c = jnp.stack([z[..., cols["k_cmp"]:cols["k_cmp"] + kvw], z[..., cols["v_cmp"]:cols["v_cmp"] + kvw]])
    nxt = jnp.concatenate([kv_c[:, :, NSA_CMP_STRIDE:], jnp.zeros_like(kv_c[:, :, :NSA_CMP_STRIDE])], axis=2)
    blocks = jnp.stack([rows_of_16(kv_c[0]), rows_of_16(kv_c[1])])
    blocks_next = jnp.stack([rows_of_16(nxt[0]), rows_of_16(nxt[1])])
    flat = NSA_CMP_LEN * HEAD_DIM
    cmp_tok, cmp_feat = _compress(blocks, blocks_next, cmp_w1.astype(BF16),
                                  cmp_pos.reshape(2, 1, flat).astype(BF16), cmp_w2.astype(BF16))

    n_slc = s // NSA_SLC_LEN
    cmat = jnp.asarray(_importance_matrix_np(n_slc, r), BF16)
    o_cmp, sel = _nsa_select(z, cmp_tok[0], cmp_feat[1], cmat, min(NSA_TOP_N, n_slc))
    win_tiles = _causal_bucket_tiles(ATT_T, -(-NSA_WINDOW // ATT_T) + 1, NSA_WINDOW)
    win_tiles = np.concatenate([win_tiles, np.full((1, ATT_T, ATT_T), MASKED_BUCKET, np.int32)], axis=0)
    bias_win = _build_bias_tiles(tab, win_tiles)
    return _nsa_attend(z, zt, sel, o_cmp, bias_causal, bias_win, cols)


def _scaled_qkv(w_in, scale=SCALE):
    return jnp.concatenate([w_in[:, :D_MODEL] * scale, w_in[:, D_MODEL:]], axis=1).astype(BF16)


def kernel(x, rel_bias, norm_gains, final_gain, a_w_in, a_lambda, a_subln, a_w_out, b_w_in, b_w_out,
           c_w_in, c_w_out, d_w_in, d_cmp_pos, d_cmp_w1, d_cmp_w2, d_w_out, ffn_w_gate, ffn_w_up,
           ffn_w_down, moe_router, moe_w_gate, moe_w_up, moe_w_down):
    b, s, d = x.shape
    assert d == D_MODEL and s % (B_PAIRS[-1][0]) == 0 and s >= (N_NEAR + 1) * ATT_T
    assert _t5_bucket_np(np.array([(N_NEAR - 1) * ATT_T + 1]))[0] == NUM_BUCKETS - 1
    n = b * s
    h = x.reshape(n, d)
    tab = rel_bias.T.astype(F32)
    tab_rel = (tab - tab[:, NUM_BUCKETS - 1:]) * LOG2E
    bias_causal = _build_bias_tiles(tab_rel, _sweep_bucket_tiles(ATT_T))

    w = _scaled_qkv(a_w_in[0], SCALE * LOG2E)
    z, zt = _norm_matmul(h, norm_gains[0, 0], w[:, :2 * d], w[:, 2 * d:].T)
    lam_init = 0.8 - 0.6 * math.exp(-0.3 * 0)
    o = _diff_attention(z.reshape(b, s, -1), zt, a_lambda[0].astype(F32), a_subln[0].astype(F32),
                        bias_causal, lam_init)
    h = _proj_residual(o.reshape(n, d), a_w_out[0].astype(BF16), h)
    h = _ffn(h, norm_gains[0, 1], ffn_w_gate[0].astype(BF16), ffn_w_up[0].astype(BF16),
             ffn_w_down[0].astype(BF16))

    z = _norm_matmul(h, norm_gains[1, 0], _scaled_qkv(b_w_in[0])).reshape(b, s, -1)
    outs, lses = [], []
    for _, dil in B_PAIRS:
        bias = _build_bias_tiles(tab, _dilated_bucket_tiles(dil))
        o, lse = _dilated_group(z, bias, dil)
        outs.append(o)
        lses.append(lse)
    h = _dil_combine_proj(outs, lses, b_w_out[0].astype(BF16), h)
    h = _moe(h, norm_gains[1, 1], moe_router[0], moe_w_gate[0].astype(BF16), moe_w_up[0].astype(BF16),
             moe_w_down[0].astype(BF16), final_gain, False)

    w = _scaled_qkv(c_w_in[0])
    z, zt = _norm_matmul(h, norm_gains[2, 0], w[:, :2 * d], w[:, 2 * d:].T)
    o = _stick_attention(z.reshape(b, s, -1), zt)
    h = _proj_residual(o.reshape(n, d), c_w_out[0].astype(BF16), h)
    h = _ffn(h, norm_gains[2, 1], ffn_w_gate[1].astype(BF16), ffn_w_up[1].astype(BF16),
             ffn_w_down[1].astype(BF16))

    o = _nsa_mixer(h, norm_gains[3, 0], b, s, d_w_in[0], d_cmp_pos[0], d_cmp_w1[0], d_cmp_w2[0], tab_rel,
                   bias_causal)
    h = _proj_residual(o.reshape(n, d), d_w_out[0].astype(BF16), h)
    h = _moe(h, norm_gains[3, 1], moe_router[1], moe_w_gate[1].astype(BF16), moe_w_up[1].astype(BF16),
             moe_w_down[1].astype(BF16), final_gain, True)
    return h.reshape(b, s, d)
```

```python
import functools
import math

import numpy as np
import jax
import jax.numpy as jnp
from jax import lax
from jax.experimental import pallas as pl
from jax.experimental.pallas import tpu as pltpu

F32 = jnp.float32
BF16 = jnp.bfloat16

D_MODEL = 1024
HEAD_DIM = 64
N_HEADS = 16
LANES = 128
B_PAIRS = ((128, 1), (512, 4), (2048, 16))
DIL_W = 128
NSA_KV = 4
NSA_GROUP = 4
NSA_CMP_LEN = 32
NSA_CMP_STRIDE = 16
NSA_CMP_HID = 256
NSA_SLC_LEN = 64
NSA_TOP_N = 16
NSA_WINDOW = 512
NUM_BUCKETS = 32
MAX_EXACT = 16
REL_MAX_DIST = 2048
N_EXPERTS = 8
EPS = 1e-6
SCALE = HEAD_DIM ** -0.5
NEG = -1e30
MASKED_BUCKET = NUM_BUCKETS

ATT_T = 256
N_NEAR = 7
ATT_Q_TILES = 2
ATT_GROUP = 2
NSA_SWEEP_GROUP = 2
CMP_ROW_VARIANTS = 4
DEN_ROWS = 16
LOG2E = math.log2(math.e)
SB_T = 256
SB_CUTOFF = -104.0
VMEM_LIMIT = 56 << 20


def _cparams(sem):
    return pltpu.CompilerParams(dimension_semantics=sem, vmem_limit_bytes=VMEM_LIMIT)


def _resident(block_shape, index_map):
    return pl.BlockSpec(block_shape, index_map, pipeline_mode=pl.Buffered(1))


def _nt_dot(a, b):
    return lax.dot_general(a, b, (((1,), (1,)), ((), ())), preferred_element_type=F32)


def _dot(a, b):
    return jnp.dot(a, b, preferred_element_type=F32)


def _split2(x):
    hi = x.astype(BF16)
    lo = (x - hi.astype(F32)).astype(BF16)
    return hi, lo


def _split3(x):
    hi = x.astype(BF16)
    rem = x - hi.astype(F32)
    mid = rem.astype(BF16)
    lo = (rem - mid.astype(F32)).astype(BF16)
    return hi, mid, lo


def _rms(x, g):
    ms = jnp.mean(x * x, axis=-1, keepdims=True)
    return x * lax.rsqrt(ms + EPS) * g


def _sigmoid(x):
    return 1.0 / (1.0 + jnp.exp(-x))


def _eye(n):
    r = lax.broadcasted_iota(jnp.int32, (n, n), 0)
    c = lax.broadcasted_iota(jnp.int32, (n, n), 1)
    return jnp.where(r == c, 1.0, 0.0).astype(BF16)


def _flip(eye, x):
    return _nt_dot(eye, x)


def _pad_rows(x, top):
    zero = jnp.zeros_like(x)
    return jnp.concatenate([x, zero] if top else [zero, x], axis=0)


def _t5_bucket_np(dist):
    n = np.maximum(dist, 0)
    ratio = np.log(np.maximum(n, 1).astype(np.float32) / np.float32(MAX_EXACT)) / np.float32(
        math.log(REL_MAX_DIST / MAX_EXACT))
    large = np.minimum(MAX_EXACT + (ratio * np.float32(NUM_BUCKETS - MAX_EXACT)).astype(np.int32),
                       NUM_BUCKETS - 1)
    return np.where(n < MAX_EXACT, n, large).astype(np.int32)


def _causal_bucket_tiles(t, n_tiles, max_dist=None):
    i = np.arange(t)[None, None, :]
    j = np.arange(t)[None, :, None]
    dist = np.arange(n_tiles)[:, None, None] * t + i - j
    ok = dist >= 0
    if max_dist is not None:
        ok = ok & (dist < max_dist)
    return np.where(ok, _t5_bucket_np(dist), MASKED_BUCKET).astype(np.int32)


def _sweep_bucket_tiles(t):
    far = np.full((1, t, t), NUM_BUCKETS - 1, np.int32)
    future = np.full((1, t, t), MASKED_BUCKET, np.int32)
    return np.concatenate([_causal_bucket_tiles(t, N_NEAR), far, future], axis=0)


def _bias_tile_index(delta):
    return jnp.where(delta < 0, N_NEAR + 1, jnp.minimum(delta, N_NEAR))


def _sweep_bias(bias_ref, col, first_q, nq, first_key, count):
    return jnp.concatenate(
        [jnp.concatenate([bias_ref[col, _bias_tile_index(first_q + a - (first_key + u))] for a in range(nq)], axis=1)
         for u in range(count)], axis=0)


def _dilated_bucket_tiles(dilation):
    w = DIL_W
    steps = w + np.arange(w)[None, :] - np.arange(2 * w)[:, None]
    ok = (steps >= 0) & (steps <= w)
    bkt = np.where(ok, _t5_bucket_np(dilation * steps), MASKED_BUCKET)
    first = np.where(np.arange(2 * w)[:, None] < w, MASKED_BUCKET, bkt)
    return np.stack([first, bkt]).astype(np.int32)


def _bias_build_kernel(tab_ref, bkt_ref, o_ref):
    c = pl.program_id(0)
    bkt = bkt_ref[0]
    out = jnp.full(bkt.shape, NEG, F32)
    for b in range(NUM_BUCKETS):
        out = jnp.where(bkt == b, tab_ref[c, b], out)
    o_ref[0, 0] = out


def _build_bias_tiles(tab, buckets):
    n, r, c = buckets.shape
    return pl.pallas_call(
        _bias_build_kernel,
        out_shape=jax.ShapeDtypeStruct((N_HEADS, n, r, c), F32),
        grid=(N_HEADS, n),
        in_specs=[pl.BlockSpec(memory_space=pltpu.SMEM),
                  pl.BlockSpec((1, r, c), lambda h, i: (i, 0, 0))],
        out_specs=pl.BlockSpec((1, 1, r, c), lambda h, i: (h, i, 0, 0)),
        compiler_params=_cparams(("arbitrary", "arbitrary")),
        name="bias_tiles",
    )(tab, jnp.asarray(buckets))


def _norm_matmul_kernel(x_ref, g_ref, w_ref, *refs, chunk, feature_major):
    xn = _rms(x_ref[...], g_ref[...]).astype(BF16)
    o_ref = refs[1] if feature_major else refs[0]
    nout = o_ref.shape[1]
    for c0 in range(0, nout, chunk):
        c1 = min(c0 + chunk, nout)
        o_ref[:, c0:c1] = _dot(xn, w_ref[:, c0:c1]).astype(o_ref.dtype)
    if feature_major:
        wt_ref, ot_ref = refs[0], refs[2]
        nt = ot_ref.shape[0]
        for c0 in range(0, nt, chunk):
            c1 = min(c0 + chunk, nt)
            ot_ref[c0:c1, :] = _nt_dot(wt_ref[c0:c1, :], xn).astype(ot_ref.dtype)


def _norm_matmul(x, g, w, wt=None, tm=512):
    n, d = x.shape
    nout = w.shape[1]
    in_specs = [pl.BlockSpec((tm, d), lambda i: (i, 0)),
                pl.BlockSpec((1, d), lambda i: (0, 0)),
                pl.BlockSpec((d, nout), lambda i: (0, 0))]
    out_shape = [jax.ShapeDtypeStruct((n, nout), BF16)]
    out_specs = [pl.BlockSpec((tm, nout), lambda i: (i, 0))]
    args = [x, g.reshape(1, d), w]
    if wt is not None:
        nt = wt.shape[0]
        in_specs.append(pl.BlockSpec((nt, d), lambda i: (0, 0)))
        out_shape.append(jax.ShapeDtypeStruct((nt, n), BF16))
        out_specs.append(pl.BlockSpec((nt, tm), lambda i: (0, i)))
        args.append(wt)
    out = pl.pallas_call(
        functools.partial(_norm_matmul_kernel, chunk=512, feature_major=wt is not None),
        out_shape=tuple(out_shape),
        grid=(n // tm,),
        in_specs=in_specs,
        out_specs=tuple(out_specs),
        compiler_params=_cparams(("parallel",)),
        name="norm_matmul",
    )(*args)
    return out if wt is not None else out[0]


def _norm_matmul_views_kernel(x_ref, g_ref, w_ref, *refs, chunk, dils):
    out_refs, z_ref = refs[:len(dils)], refs[len(dils)]
    xn = _rms(x_ref[...], g_ref[...]).astype(BF16)
    n_col, tm, _ = z_ref.shape
    nout = n_col * LANES
    for c0 in range(0, nout, chunk):
        c1 = min(c0 + chunk, nout)
        zc = _dot(xn, w_ref[:, c0:c1])
        for k in range((c1 - c0) // LANES):
            z_ref[c0 // LANES + k] = zc[:, k * LANES:(k + 1) * LANES]
    for o_ref, d in zip(out_refs, dils):
        for r in range(d):
            for c in range(n_col):
                col = r * nout + c * LANES
                o_ref[:, col:col + LANES] = z_ref[c, pl.ds(r, tm // d, stride=d), :].astype(o_ref.dtype)


def _norm_matmul_views(x, g, w, dils, tm=512):
    n, d_in = x.shape
    nout = w.shape[1]
    return pl.pallas_call(
        functools.partial(_norm_matmul_views_kernel, chunk=512, dils=tuple(dils)),
        out_shape=tuple(jax.ShapeDtypeStruct((n // d, d * nout), BF16) for d in dils),
        grid=(n // tm,),
        in_specs=[pl.BlockSpec((tm, d_in), lambda i: (i, 0)),
                  pl.BlockSpec((1, d_in), lambda i: (0, 0)),
                  pl.BlockSpec((d_in, nout), lambda i: (0, 0))],
        out_specs=tuple(pl.BlockSpec((tm // d, d * nout), lambda i: (i, 0)) for d in dils),
        scratch_shapes=[pltpu.VMEM((nout // LANES, tm, LANES), F32)],
        compiler_params=_cparams(("parallel",)),
        name="norm_matmul_views",
    )(x, g.reshape(1, d_in), w)


def _proj_res_kernel(a_ref, w_ref, h_ref, o_ref):
    o_ref[...] = h_ref[...] + _dot(a_ref[...], w_ref[...])


def _proj_residual(a, w, h, tm=512):
    n, d = h.shape
    k = a.shape[1]
    return pl.pallas_call(
        _proj_res_kernel,
        out_shape=jax.ShapeDtypeStruct((n, d), F32),
        grid=(n // tm,),
        in_specs=[pl.BlockSpec((tm, k), lambda i: (i, 0)),
                  pl.BlockSpec((k, d), lambda i: (0, 0)),
                  pl.BlockSpec((tm, d), lambda i: (i, 0))],
        out_specs=pl.BlockSpec((tm, d), lambda i: (i, 0)),
        compiler_params=_cparams(("parallel",)),
        name="proj_residual",
    )(a, w, h)


def _dil_combine_proj_kernel(o1, o2, o3, l1, l2, l3, w_ref, h_ref, o_ref, nat_ref, *, dils):
    tm = h_ref.shape[0]
    d_model = h_ref.shape[1]

    def in_position_order(view_ref, slot, d):
        if d == 1:
            return view_ref[...].astype(F32)
        for r in range(d):
            for c in range(d_model // LANES):
                col = r * d_model + c * LANES
                nat_ref[slot, c, pl.ds(r, tm // d, stride=d), :] = view_ref[:, col:col + LANES].astype(F32)
        return jnp.concatenate([nat_ref[slot, c] for c in range(d_model // LANES)], axis=1)

    outs = [in_position_order(ref, 2 * gi, d) for gi, (ref, d) in enumerate(zip((o1, o2, o3), dils))]
    ls = [in_position_order(ref, 2 * gi + 1, d) for gi, (ref, d) in enumerate(zip((l1, l2, l3), dils))]
    m = jnp.maximum(jnp.maximum(ls[0], ls[1]), ls[2])
    es = [jnp.exp(l - m) for l in ls]
    den = es[0] + es[1] + es[2]
    a = (es[0] * outs[0] + es[1] * outs[1] + es[2] * outs[2]) / den
    o_ref[...] = h_ref[...] + _dot(a.astype(BF16), w_ref[...])


def _dil_combine_proj(outs, lses, dils, w, h, tm=512):
    n, d = h.shape
    row = pl.BlockSpec((tm, d), lambda i: (i, 0))
    views = [pl.BlockSpec((tm // dil, dil * d), lambda i: (i, 0)) for dil in dils]
    return pl.pallas_call(
        functools.partial(_dil_combine_proj_kernel, dils=tuple(dils)),
        out_shape=jax.ShapeDtypeStruct((n, d), F32),
        grid=(n // tm,),
        in_specs=views + views + [pl.BlockSpec((d, d), lambda i: (0, 0)), row],
        out_specs=row,
        scratch_shapes=[pltpu.VMEM((2 * len(dils), d // LANES, tm, LANES), F32)],
        compiler_params=_cparams(("parallel",)),
        name="dilated_combine_proj",
    )(*outs, *lses, w, h)


def _ffn_kernel(h_ref, g_ref, wg_ref, wu_ref, wd_ref, o_ref, xn_ref, acc_ref):
    f = pl.program_id(1)

    @pl.when(f == 0)
    def _():
        xn_ref[...] = _rms(h_ref[...], g_ref[...]).astype(BF16)
        acc_ref[...] = jnp.zeros_like(acc_ref)

    xn = xn_ref[...]
    gate = _dot(xn, wg_ref[...])
    up = _dot(xn, wu_ref[...])
    act = (gate * _sigmoid(gate) * up).astype(BF16)
    acc_ref[...] += _dot(act, wd_ref[...])

    @pl.when(f == pl.num_programs(1) - 1)
    def _():
        o_ref[...] = h_ref[...] + acc_ref[...]


def _ffn(h, g, wg, wu, wd, tm=1024, n_f=2):
    n, d = h.shape
    dff = wg.shape[1]
    tf = dff // n_f
    return pl.pallas_call(
        _ffn_kernel,
        out_shape=jax.ShapeDtypeStruct((n, d), F32),
        grid=(n // tm, n_f),
        in_specs=[pl.BlockSpec((tm, d), lambda i, f: (i, 0)),
                  pl.BlockSpec((1, d), lambda i, f: (0, 0)),
                  pl.BlockSpec((d, tf), lambda i, f: (0, f)),
                  pl.BlockSpec((d, tf), lambda i, f: (0, f)),
                  pl.BlockSpec((tf, d), lambda i, f: (f, 0))],
        out_specs=pl.BlockSpec((tm, d), lambda i, f: (i, 0)),
        scratch_shapes=[pltpu.VMEM((tm, d), BF16), pltpu.VMEM((tm, d), F32)],
        compiler_params=_cparams(("parallel", "arbitrary")),
        name="ffn",
    )(h, g.reshape(1, d), wg, wu, wd)


def _moe_kernel(h_ref, g_ref, rh_ref, rl_ref, wg_ref, wu_ref, wd_ref, fg_ref, o_ref,
                xn_ref, comb_ref, acc_ref, *, final_norm):
    e = pl.program_id(1)
    col = lax.broadcasted_iota(jnp.int32, comb_ref.shape, 1)

    @pl.when(e == 0)
    def _():
        xn = _rms(h_ref[...], g_ref[...])
        xh, xl = _split2(xn)
        xn_ref[...] = xh
        logits = _dot(xh, rh_ref[...]) + _dot(xh, rl_ref[...]) + _dot(xl, rh_ref[...])
        colf = col.astype(F32)
        lg = jnp.where(col < N_EXPERTS, logits, NEG)
        m1 = jnp.max(lg, axis=-1, keepdims=True)
        i1 = jnp.min(jnp.where(lg == m1, colf, float(LANES)), axis=-1, keepdims=True)
        lg2 = jnp.where(colf == i1, NEG, lg)
        m2 = jnp.max(lg2, axis=-1, keepdims=True)
        i2 = jnp.min(jnp.where(lg2 == m2, colf, float(LANES)), axis=-1, keepdims=True)
        e2 = jnp.exp(m2 - m1)
        g1 = 1.0 / (1.0 + e2)
        g2 = e2 / (1.0 + e2)
        comb_ref[...] = jnp.where(colf == i1, g1, 0.0) + jnp.where(colf == i2, g2, 0.0)
        acc_ref[...] = jnp.zeros_like(acc_ref)

    xn = xn_ref[...]
    gate = _dot(xn, wg_ref[0])
    up = _dot(xn, wu_ref[0])
    ce = jnp.sum(jnp.where(col == e, comb_ref[...], 0.0), axis=-1, keepdims=True)
    act = (gate * _sigmoid(gate) * up * ce).astype(BF16)
    acc_ref[...] += _dot(act, wd_ref[0])

    @pl.when(e == pl.num_programs(1) - 1)
    def _():
        y = h_ref[...] + acc_ref[...]
        if final_norm:
            y = _rms(y, fg_ref[...])
        o_ref[...] = y


def _moe(h, g, router, wg, wu, wd, final_gain, final_norm, tm=1024):
    n, d = h.shape
    ne, _, dfe = wg.shape
    rpad = jnp.zeros((d, LANES), F32).at[:, :ne].set(router)
    rh, rl = _split2(rpad)
    return pl.pallas_call(
        functools.partial(_moe_kernel, final_norm=final_norm),
        out_shape=jax.ShapeDtypeStruct((n, d), F32),
        grid=(n // tm, ne),
        in_specs=[pl.BlockSpec((tm, d), lambda i, e: (i, 0)),
                  pl.BlockSpec((1, d), lambda i, e: (0, 0)),
                  pl.BlockSpec((d, LANES), lambda i, e: (0, 0)),
                  pl.BlockSpec((d, LANES), lambda i, e: (0, 0)),
                  pl.BlockSpec((1, d, dfe), lambda i, e: (e, 0, 0)),
                  pl.BlockSpec((1, d, dfe), lambda i, e: (e, 0, 0)),
                  pl.BlockSpec((1, dfe, d), lambda i, e: (e, 0, 0)),
                  pl.BlockSpec((1, d), lambda i, e: (0, 0))],
        out_specs=pl.BlockSpec((tm, d), lambda i, e: (i, 0)),
        scratch_shapes=[pltpu.VMEM((tm, d), BF16), pltpu.VMEM((tm, LANES), F32),
                        pltpu.VMEM((tm, d), F32)],
        compiler_params=_cparams(("parallel", "arbitrary")),
        name="moe",
    )(h, g.reshape(1, d), rh, rl, wg, wu, wd, final_gain.reshape(1, d))


def _with_ones(vt):
    return jnp.concatenate([vt, jnp.ones((DEN_ROWS, vt.shape[1]), vt.dtype)], axis=0)


def _flash_step(s, vt1, m_ref, acc_ref, idx):
    m_old = m_ref[idx]
    m_new = jnp.maximum(m_old, jnp.max(s, axis=0, keepdims=True))
    p = jnp.exp2(s - m_new).astype(BF16)
    acc_ref[idx] = jnp.exp2(m_old - m_new) * acc_ref[idx] + _dot(vt1, p)
    m_ref[idx] = m_new


def _flash_result(acc_ref, idx):
    acc = acc_ref[idx]
    nf = acc.shape[0] - DEN_ROWS
    return acc[:nf] / acc[nf:nf + 1]


def _far_groups(qi, group):
    return lax.shift_right_logical(jnp.maximum(qi + 1 - N_NEAR, 0), int(math.log2(group)))


def _pipelined_sweep(n_far, n_groups, scores, softmax, sa_ref, sb_ref):
    scores(0, sa_ref)

    def pairs(near):
        def pair(k, carry):
            scores(2 * k + 1, sb_ref)
            softmax(2 * k, sa_ref, near)
            scores(2 * k + 2, sa_ref)
            softmax(2 * k + 1, sb_ref, near)
            return carry
        return pair

    far_pairs = lax.shift_right_logical(n_far, 1)
    lax.fori_loop(0, far_pairs, pairs(False), 0)
    lax.fori_loop(far_pairs, lax.shift_right_logical(n_groups + 1, 1), pairs(True), 0)


def _flash_init(m_ref, acc_ref):
    m_ref[...] = jnp.full(m_ref.shape, NEG, F32)
    acc_ref[...] = jnp.zeros_like(acc_ref)


def _diff_attn_kernel(lam_ref, q_ref, k_ref, vt_ref, bias_ref, g_ref, o_ref, m_ref, acc_ref, sa_ref, sb_ref,
                      *, t, nq, lam_init):
    qi = pl.program_id(2)
    tq = nq * t
    qt = _flip(_eye(LANES), q_ref[...]).astype(BF16)
    rhs = jnp.concatenate([_pad_rows(qt[:HEAD_DIM], True), _pad_rows(qt[HEAD_DIM:], False)], axis=1)
    _flash_init(m_ref, acc_ref)
    gt = ATT_GROUP * t
    first_q = qi * nq
    n_groups = lax.shift_right_logical(first_q + nq - 1 + ATT_GROUP, int(math.log2(ATT_GROUP)))

    def key_start(gi):
        return pl.multiple_of(jnp.minimum(gi, n_groups - 1) * gt, gt)

    def scores(gi, s_ref):
        s_ref[...] = _dot(k_ref[pl.ds(key_start(gi), gt), :], rhs)

    def softmax(gi, s_ref, near):
        vt1 = _with_ones(vt_ref[:, pl.ds(key_start(gi), gt)])
        for mi in range(2):
            s = s_ref[:, mi * tq:(mi + 1) * tq]
            if near:
                s = s + _sweep_bias(bias_ref, mi, first_q, nq, gi * ATT_GROUP, ATT_GROUP)
            _flash_step(s, vt1, m_ref, acc_ref, mi)

    _pipelined_sweep(_far_groups(first_q, ATT_GROUP), n_groups, scores, softmax, sa_ref, sb_ref)

    lam = lam_ref[...]
    lam_full = (jnp.exp(jnp.sum(lam[0:1] * lam[1:2], axis=-1, keepdims=True))
                - jnp.exp(jnp.sum(lam[2:3] * lam[3:4], axis=-1, keepdims=True)) + lam_init)
    o = _flash_result(acc_ref, 0) - lam_full * _flash_result(acc_ref, 1)
    ms = jnp.mean(o * o, axis=0, keepdims=True)
    y = (o * lax.rsqrt(ms + EPS) * g_ref[...] * (1.0 - lam_init)).astype(BF16)
    o_ref[...] = _flip(_eye(tq), y).astype(o_ref.dtype)


def _diff_attention(z, zt, lam, subln, bias_tiles, lam_init):
    b, s, _ = z.shape
    t = ATT_T
    tq = ATT_Q_TILES * t
    nh = N_HEADS // 2
    n_tiles = bias_tiles.shape[1]
    return pl.pallas_call(
        functools.partial(_diff_attn_kernel, t=t, nq=ATT_Q_TILES, lam_init=lam_init),
        out_shape=jax.ShapeDtypeStruct((b, s, D_MODEL), BF16),
        grid=(b, nh, s // tq),
        in_specs=[pl.BlockSpec((4, HEAD_DIM), lambda bi, h, qi: (0, 0)),
                  pl.BlockSpec((None, tq, LANES), lambda bi, h, qi: (bi, qi, h)),
                  _resident((None, s, LANES), lambda bi, h, qi: (bi, 0, nh + h)),
                  _resident((LANES, s), lambda bi, h, qi: (h, bi)),
                  _resident((2, n_tiles, t, t), lambda bi, h, qi: (h, 0, 0, 0)),
                  pl.BlockSpec((LANES, 1), lambda bi, h, qi: (0, 0))],
        out_specs=pl.BlockSpec((None, tq, LANES), lambda bi, h, qi: (bi, qi, h)),
        scratch_shapes=[pltpu.VMEM((2, 1, tq), F32), pltpu.VMEM((2, LANES + DEN_ROWS, tq), F32),
                        pltpu.VMEM((ATT_GROUP * t, 2 * tq), F32), pltpu.VMEM((ATT_GROUP * t, 2 * tq), F32)],
        compiler_params=_cparams(("parallel", "parallel", "arbitrary")),
        name="diff_attention",
    )(lam, z, z, zt, bias_tiles, subln.reshape(LANES, 1))


def _dilated_kernel(q_ref, kp_ref, kc_ref, vp_ref, vc_ref, bias_ref, o_ref, lse_ref):
    n = pl.program_id(2)
    variant = jnp.minimum(n, 1)
    w = q_ref.shape[0]
    eye = _eye(LANES)
    eye_w = _eye(w)
    pairs = range(N_HEADS // 2)
    cols = [slice(hp * LANES, (hp + 1) * LANES) for hp in pairs]
    qts = [_flip(eye, q_ref[:, c]).astype(BF16) for c in cols]
    kcats = [jnp.concatenate([kp_ref[:, c], kc_ref[:, c]], axis=0) for c in cols]
    vts = [_flip(eye, jnp.concatenate([vp_ref[:, c], vc_ref[:, c]], axis=0)).astype(BF16) for c in cols]
    s2s = [_dot(kcats[hp], jnp.concatenate([_pad_rows(qts[hp][:HEAD_DIM], True),
                                            _pad_rows(qts[hp][HEAD_DIM:], False)], axis=1))
           for hp in pairs]
    ps, lses = [], []
    for hp in pairs:
        for sub in range(2):
            s = s2s[hp][:, sub * w:(sub + 1) * w] + bias_ref[2 * hp + sub, variant]
            m = jnp.max(s, axis=0, keepdims=True)
            e = jnp.exp(s - m)
            den = jnp.sum(e, axis=0, keepdims=True)
            ps.append((e * (1.0 / den)).astype(BF16))
            lses.append(jnp.broadcast_to(m + jnp.log(den), (HEAD_DIM, w)))
    outs = [_dot(vts[hp][sub * HEAD_DIM:(sub + 1) * HEAD_DIM], ps[2 * hp + sub])
            for hp in pairs for sub in range(2)]
    for hp in pairs:
        pair_out = jnp.concatenate(outs[2 * hp:2 * hp + 2], axis=0).astype(BF16)
        o_ref[:, cols[hp]] = _flip(eye_w, pair_out).astype(o_ref.dtype)
        pair_lse = jnp.concatenate(lses[2 * hp:2 * hp + 2], axis=0)
        lse_ref[:, cols[hp]] = sum(_flip(eye_w, term) for term in _split3(pair_lse))


def _dilated_group(zv, bias, dilation):
    b, m_len, _ = zv.shape
    w = DIL_W
    nb = m_len // w
    blk = lambda sect, prev: pl.BlockSpec(
        (None, w, D_MODEL),
        (lambda bi, r, n: (bi, jnp.maximum(n - 1, 0), 3 * r + sect)) if prev
        else (lambda bi, r, n: (bi, n, 3 * r + sect)))
    out_blk = pl.BlockSpec((None, w, D_MODEL), lambda bi, r, n: (bi, n, r))
    o, lse = pl.pallas_call(
        _dilated_kernel,
        out_shape=(jax.ShapeDtypeStruct((b, m_len, dilation * D_MODEL), BF16),
                   jax.ShapeDtypeStruct((b, m_len, dilation * D_MODEL), F32)),
        grid=(b, dilation, nb),
        in_specs=[blk(0, False), blk(1, True), blk(1, False), blk(2, True), blk(2, False),
                  pl.BlockSpec((N_HEADS, 2, 2 * w, w), lambda bi, r, n: (0, 0, 0, 0))],
        out_specs=(out_blk, out_blk),
        compiler_params=_cparams(("parallel", "parallel", "arbitrary")),
        name=f"dilated_d{dilation}",
    )(zv, zv, zv, zv, zv, bias)
    return o.reshape(b * m_len, dilation * D_MODEL), lse.reshape(b * m_len, dilation * D_MODEL)


def _stick_kernel(q_ref, k_ref, vt_ref, o_ref, acc_ref, carry_ref, *, t):
    qi = pl.program_id(2)
    qt = _flip(_eye(LANES), q_ref[...]).astype(BF16)
    rhs = jnp.concatenate([_pad_rows(qt[:HEAD_DIM], True), _pad_rows(qt[HEAD_DIM:], False)], axis=1)
    row = lax.broadcasted_iota(jnp.int32, (t, t), 0)
    col = lax.broadcasted_iota(jnp.int32, (t, t), 1)
    after = jnp.where(col > row, 1.0, 0.0).astype(BF16)
    before = jnp.concatenate([row < col, row < col], axis=1)
    acc_ref[...] = jnp.zeros_like(acc_ref)
    carry_ref[...] = jnp.zeros_like(carry_ref)

    def block(j, masked):
        start = pl.multiple_of(j * t, t)
        z = _dot(k_ref[pl.ds(start, t), :], rhs)
        log_keep = -(jnp.maximum(z, 0.0) + jnp.log(1.0 + jnp.exp(-jnp.abs(z))))
        if masked:
            log_keep = jnp.where(before, log_keep, 0.0)
        hi, lo = _split2(log_keep)
        later = _dot(after, hi) + _dot(after, lo) + carry_ref[...]
        a = jnp.exp(z + log_keep + later)
        if masked:
            a = jnp.where(before, a, 0.0)
        a = a.astype(BF16)
        for sub in range(2):
            vt = vt_ref[sub * HEAD_DIM:(sub + 1) * HEAD_DIM, pl.ds(start, t)]
            acc_ref[sub] += _dot(vt, a[:, sub * t:(sub + 1) * t])
        carry_ref[...] += jnp.sum(log_keep, axis=0, keepdims=True)

    block(qi, True)

    def cond(c):
        return jnp.logical_and(c[0] >= 0, c[1] >= SB_CUTOFF)

    def body(c):
        block(c[0], False)
        return c[0] - 1, jnp.max(carry_ref[...])

    lax.while_loop(cond, body, (qi - 1, jnp.max(carry_ref[...])))
    out = jnp.concatenate([acc_ref[0], acc_ref[1]], axis=0).astype(BF16)
    o_ref[...] = _flip(_eye(t), out).astype(o_ref.dtype)


def _stick_attention(z, zt):
    b, s, _ = z.shape
    t = SB_T
    nh = N_HEADS // 2
    return pl.pallas_call(
        functools.partial(_stick_kernel, t=t),
        out_shape=jax.ShapeDtypeStruct((b, s, D_MODEL), BF16),
        grid=(b, nh, s // t),
        in_specs=[pl.BlockSpec((None, t, LANES), lambda bi, h, qi: (bi, qi, h)),
                  _resident((None, s, LANES), lambda bi, h, qi: (bi, 0, nh + h)),
                  _resident((LANES, s), lambda bi, h, qi: (h, bi))],
        out_specs=pl.BlockSpec((None, t, LANES), lambda bi, h, qi: (bi, qi, h)),
        scratch_shapes=[pltpu.VMEM((2, HEAD_DIM, t), F32), pltpu.VMEM((1, 2 * t), F32)],
        compiler_params=_cparams(("parallel", "parallel", "arbitrary")),
        name="stick_breaking",
    )(z, z, zt)


def _compress_kernel(a_ref, as_ref, w1_ref, pos_ref, w2_ref, w2t_ref, o_ref, ot_ref):
    half = a_ref.shape[1]
    w1 = w1_ref[...]
    const = _dot(jnp.broadcast_to(pos_ref[...], (8, 2 * half)), w1)[0:1]
    pre = _dot(a_ref[...], w1[:half]) + _dot(as_ref[...], w1[half:]) + const
    hid = 0.5 * pre * (1.0 + jnp.tanh(math.sqrt(2.0 / math.pi) * (pre + 0.044715 * pre * pre * pre)))
    hid = hid.astype(BF16)
    o_ref[...] = _dot(hid, w2_ref[...]).astype(o_ref.dtype)
    ot_ref[...] = _nt_dot(w2t_ref[...], hid).astype(ot_ref.dtype)


def _compress(blocks, blocks_next, w1, pos, w2, tr=256):
    _, nkv, b, r, half = blocks.shape
    tr = min(tr, r)
    w2pad = jnp.concatenate([w2, jnp.zeros_like(w2)], axis=-1)
    w2t = jnp.swapaxes(w2, 1, 2)
    spec = pl.BlockSpec((None, None, None, tr, half), lambda j, hk, bi, ri: (j, hk, bi, ri, 0))
    return pl.pallas_call(
        _compress_kernel,
        out_shape=(jax.ShapeDtypeStruct((2, nkv, b, r, LANES), BF16),
                   jax.ShapeDtypeStruct((2, nkv, b, HEAD_DIM, r), BF16)),
        grid=(2, nkv, b, r // tr),
        in_specs=[spec, spec,
                  pl.BlockSpec((None, 2 * half, NSA_CMP_HID), lambda j, hk, bi, ri: (j, 0, 0)),
                  pl.BlockSpec((None, 1, 2 * half), lambda j, hk, bi, ri: (j, 0, 0)),
                  pl.BlockSpec((None, NSA_CMP_HID, LANES), lambda j, hk, bi, ri: (j, 0, 0)),
                  pl.BlockSpec((None, HEAD_DIM, NSA_CMP_HID), lambda j, hk, bi, ri: (j, 0, 0))],
        out_specs=(pl.BlockSpec((None, None, None, tr, LANES), lambda j, hk, bi, ri: (j, hk, bi, ri, 0)),
                   pl.BlockSpec((None, None, None, HEAD_DIM, tr), lambda j, hk, bi, ri: (j, hk, bi, 0, ri))),
        compiler_params=_cparams(("parallel", "parallel", "parallel", "arbitrary")),
        name="nsa_compress",
    )(blocks, blocks_next, w1, pos, w2pad, w2t)


def _nsa_select_kernel(q_ref, kc_ref, vct_ref, cmat_ref, o_ref, sel_ref, *, t, n_sel):
    qi = pl.program_id(2)
    r = kc_ref.shape[0]
    n_slc = sel_ref.shape[0]
    qt = _flip(_eye(NSA_GROUP * HEAD_DIM), q_ref[...]).astype(BF16)
    tpos = qi * t + lax.broadcasted_iota(jnp.int32, (1, t), 1)
    live = (tpos >= NSA_CMP_LEN - 1).astype(F32)

    def compressed_branch(rows):
        kc = kc_ref[:rows, :]
        vct = vct_ref[:, :rows]
        cidx = lax.broadcasted_iota(jnp.int32, (rows, t), 0)
        seen = NSA_CMP_STRIDE * cidx + (NSA_CMP_LEN - 1) <= tpos
        imp = jnp.zeros((rows, t), F32)
        for g in range(NSA_GROUP):
            rhs = _pad_rows(qt[g * HEAD_DIM:(g + 1) * HEAD_DIM], True)
            sc = jnp.where(seen, _dot(kc, rhs), NEG)
            e = jnp.exp2(sc - jnp.max(sc, axis=0, keepdims=True))
            pc = e * (live / jnp.sum(e, axis=0, keepdims=True))
            o_ref[g * HEAD_DIM:(g + 1) * HEAD_DIM, :] = _dot(vct, pc.astype(BF16)).astype(o_ref.dtype)
            imp = imp + pc
        n_blk = rows // (NSA_SLC_LEN // NSA_CMP_STRIDE)
        cmat = cmat_ref[:n_blk, :rows]
        imp_slc = sum(_dot(cmat, term) for term in _split3(imp))
        blk = lax.broadcasted_iota(jnp.int32, (n_blk, t), 0)
        cur = tpos // NSA_SLC_LEN
        forced = (blk == 0) | (blk == cur) | (blk == cur - 1)
        valid = blk * NSA_SLC_LEN <= tpos
        ranked = jnp.logical_and(valid, jnp.logical_not(forced))
        score = jnp.where(ranked, imp_slc, -1.0)
        blkf = blk.astype(F32)
        for _ in range(n_sel - 3):
            top = jnp.max(score, axis=0, keepdims=True)
            first = jnp.min(jnp.where(score == top, blkf, float(n_blk)), axis=0, keepdims=True)
            score = jnp.where(blkf == first, -jnp.inf, score)
        chosen = jnp.logical_or(jnp.logical_and(valid, forced), jnp.logical_and(ranked, score == -jnp.inf))
        sel_ref[:n_blk, :] = jnp.where(chosen, 1.0, 0.0).astype(sel_ref.dtype)
        if n_blk < n_slc:
            sel_ref[n_blk:, :] = jnp.zeros((n_slc - n_blk, t), sel_ref.dtype)

    quarter = r // CMP_ROW_VARIANTS
    need = (qi * t + t - NSA_CMP_LEN) // NSA_CMP_STRIDE
    variant = jnp.clip(need // quarter, 0, CMP_ROW_VARIANTS - 1)
    for v in range(CMP_ROW_VARIANTS):
        @pl.when(variant == v)
        def _(v=v):
            compressed_branch((v + 1) * quarter)


def _nsa_select(z, kc, vct, cmat, n_sel):
    b, s, _ = z.shape
    t = ATT_T
    r = kc.shape[2]
    n_slc = cmat.shape[0]
    gw = NSA_GROUP * HEAD_DIM
    assert r % CMP_ROW_VARIANTS == 0 and (r // CMP_ROW_VARIANTS) % 16 == 0 and t % 16 == 0
    return pl.pallas_call(
        functools.partial(_nsa_select_kernel, t=t, n_sel=n_sel),
        out_shape=(jax.ShapeDtypeStruct((b, NSA_KV, gw, s), BF16),
                   jax.ShapeDtypeStruct((b, NSA_KV, n_slc, s), BF16)),
        grid=(b, NSA_KV, s // t),
        in_specs=[pl.BlockSpec((None, t, gw), lambda bi, hk, qi: (bi, qi, hk)),
                  pl.BlockSpec((None, None, r, LANES), lambda bi, hk, qi: (hk, bi, 0, 0)),
                  pl.BlockSpec((None, None, HEAD_DIM, r), lambda bi, hk, qi: (hk, bi, 0, 0)),
                  pl.BlockSpec((n_slc, r), lambda bi, hk, qi: (0, 0))],
        out_specs=(pl.BlockSpec((None, None, gw, t), lambda bi, hk, qi: (bi, hk, 0, qi)),
                   pl.BlockSpec((None, None, n_slc, t), lambda bi, hk, qi: (bi, hk, 0, qi))),
        compiler_params=_cparams(("parallel", "parallel", "arbitrary")),
        name="nsa_select",
    )(z, kc, vct, cmat)


def _nsa_attend_kernel(q_ref, ks_ref, kw_ref, vst_ref, vwt_ref, sel_ref, ocmp_ref, zg_ref, bs_ref, bw_ref,
                       o_ref, m_ref, acc_ref, res_ref, gate_ref, sel32_ref, sa_ref, sb_ref, *, t, nq, n_win):
    hk = pl.program_id(1)
    qi = pl.program_id(2)
    tq = nq * t
    first_q = qi * nq
    per_tile = t // NSA_SLC_LEN
    gw = NSA_GROUP * HEAD_DIM
    qt = _flip(_eye(gw), q_ref[...]).astype(BF16)
    low_half = (hk % 2) == 0
    q_heads = [qt[g * HEAD_DIM:(g + 1) * HEAD_DIM] for g in range(NSA_GROUP)]
    sel32_ref[...] = sel_ref[...].astype(F32)

    _flash_init(m_ref, acc_ref)
    grp = NSA_SWEEP_GROUP
    gt = grp * t
    n_groups = lax.shift_right_logical(first_q + nq - 1 + grp, int(math.log2(grp)))

    def first_tile(gi):
        return jnp.minimum(gi, n_groups - 1) * grp

    n_blk = grp * per_tile
    lane = lax.broadcasted_iota(jnp.int32, (gt, LANES), 1)
    blk_of_key = lax.broadcasted_iota(jnp.int32, (gt, LANES), 0) // NSA_SLC_LEN
    key_lanes = lax.shift_right_logical(lane, int(math.log2(HEAD_DIM))) == hk % 2
    one_hot = jnp.where(lane == blk_of_key + jnp.where(low_half, HEAD_DIM, 0), 1.0, 0.0).astype(BF16)

    def with_extra_rows(extra):
        return jnp.concatenate(
            [jnp.where(low_half, jnp.concatenate([qg, extra], axis=0), jnp.concatenate([extra, qg], axis=0))
             for qg in q_heads], axis=1)

    def scores(gi, s_ref):
        j0 = first_tile(gi)
        keys = jnp.where(key_lanes, ks_ref[pl.ds(pl.multiple_of(j0 * t, t), gt), :], one_hot)
        drop = (sel32_ref[pl.ds(pl.multiple_of(j0 * per_tile, n_blk), n_blk), :] - 1.0) * (-NEG)
        drop = jnp.concatenate([drop, jnp.zeros((HEAD_DIM - n_blk, tq), F32)], axis=0).astype(BF16)
        s_ref[...] = _dot(keys, with_extra_rows(drop))

    def softmax(gi, s_ref, near):
        j0 = first_tile(gi)
        vt1 = _with_ones(vst_ref[:, pl.ds(pl.multiple_of(j0 * t, t), gt)])
        for g in range(NSA_GROUP):
            s = s_ref[:, g * tq:(g + 1) * tq]
            if near:
                s = s + _sweep_bias(bs_ref, g, first_q, nq, gi * grp, grp)
            _flash_step(s, vt1, m_ref, acc_ref, g)

    _pipelined_sweep(_far_groups(first_q, grp), n_groups, scores, softmax, sa_ref, sb_ref)
    for g in range(NSA_GROUP):
        res_ref[g] = _flash_result(acc_ref, g)

    _flash_init(m_ref, acc_ref)
    rhs_win = with_extra_rows(jnp.zeros((HEAD_DIM, tq), BF16))
    w_first = first_q - (n_win - 1)
    for wg in range(-(-(nq + n_win - 1) // grp)):
        j0 = w_first + wg * grp
        start = pl.multiple_of(jnp.maximum(j0, 0) * t, t)
        sa_ref[...] = _dot(kw_ref[pl.ds(start, gt), :], rhs_win)
        vt1 = _with_ones(vwt_ref[:, pl.ds(start, gt)])
        for g in range(NSA_GROUP):
            rows = []
            for u in range(grp):
                tiles = []
                for a in range(nq):
                    delta = first_q + a - (j0 + u)
                    inside = jnp.logical_and(jnp.logical_and(delta >= 0, delta < n_win), j0 + u >= 0)
                    tiles.append(bw_ref[g, jnp.where(inside, delta, n_win)])
                rows.append(jnp.concatenate(tiles, axis=1))
            _flash_step(sa_ref[:, g * tq:(g + 1) * tq] + jnp.concatenate(rows, axis=0), vt1, m_ref, acc_ref, g)

    gate_ref[...] = _sigmoid(_flip(_eye(LANES), zg_ref[...]))
    mixed = []
    for g in range(NSA_GROUP):
        base = 3 * (NSA_GROUP * hk + g)
        gates = [gate_ref[pl.ds(base + c, 1), :] for c in range(3)]
        o_cmp = ocmp_ref[g * HEAD_DIM:(g + 1) * HEAD_DIM, :].astype(F32)
        mixed.append(gates[0] * o_cmp + gates[1] * res_ref[g] + gates[2] * _flash_result(acc_ref, g))
    o_ref[...] = _flip(_eye(tq), jnp.concatenate(mixed, axis=0).astype(BF16)).astype(o_ref.dtype)


def _nsa_attend(z, zt, sel, o_cmp, bias_slc, bias_win, cols):
    b, s, _ = z.shape
    t = ATT_T
    tq = ATT_Q_TILES * t
    n_slc = sel.shape[2]
    gw = NSA_GROUP * HEAD_DIM
    n_win = bias_win.shape[1] - 1
    assert (n_win - 1) % NSA_SWEEP_GROUP == 0 and ATT_Q_TILES % NSA_SWEEP_GROUP == 0
    n_tiles = bias_slc.shape[1]
    c_slc, c_win, c_gate = cols["k_slc"], cols["k_win"], cols["gate"]
    return pl.pallas_call(
        functools.partial(_nsa_attend_kernel, t=t, nq=ATT_Q_TILES, n_win=n_win),
        out_shape=jax.ShapeDtypeStruct((b, s, D_MODEL), BF16),
        grid=(b, NSA_KV, s // tq),
        in_specs=[pl.BlockSpec((None, tq, gw), lambda bi, hk, qi: (bi, qi, hk)),
                  _resident((None, s, LANES), lambda bi, hk, qi: (bi, 0, c_slc + hk // 2)),
                  _resident((None, s, LANES), lambda bi, hk, qi: (bi, 0, c_win + hk // 2)),
                  _resident((HEAD_DIM, s), lambda bi, hk, qi: (hk, bi)),
                  _resident((HEAD_DIM, s), lambda bi, hk, qi: (NSA_KV + hk, bi)),
                  pl.BlockSpec((None, None, n_slc, tq), lambda bi, hk, qi: (bi, hk, 0, qi)),
                  pl.BlockSpec((None, None, gw, tq), lambda bi, hk, qi: (bi, hk, 0, qi)),
                  pl.BlockSpec((None, tq, LANES), lambda bi, hk, qi: (bi, qi, c_gate)),
                  _resident((NSA_GROUP, n_tiles, t, t), lambda bi, hk, qi: (hk, 0, 0, 0)),
                  _resident((NSA_GROUP, n_win + 1, t, t), lambda bi, hk, qi: (hk, 0, 0, 0))],
        out_specs=pl.BlockSpec((None, tq, gw), lambda bi, hk, qi: (bi, qi, hk)),
        scratch_shapes=[pltpu.VMEM((NSA_GROUP, 1, tq), F32),
                        pltpu.VMEM((NSA_GROUP, HEAD_DIM + DEN_ROWS, tq), F32),
                        pltpu.VMEM((NSA_GROUP, HEAD_DIM, tq), F32),
                        pltpu.VMEM((LANES, tq), F32), pltpu.VMEM((n_slc, tq), F32),
                        pltpu.VMEM((NSA_SWEEP_GROUP * t, NSA_GROUP * tq), F32),
                        pltpu.VMEM((NSA_SWEEP_GROUP * t, NSA_GROUP * tq), F32)],
        compiler_params=_cparams(("parallel", "parallel", "arbitrary")),
        name="nsa_attend",
    )(z, z, z, zt, zt, sel, o_cmp, z, bias_slc, bias_win)


def _nsa_layout(w_in):
    d = w_in.shape[0]
    nq = N_HEADS * HEAD_DIM
    kvw = NSA_KV * HEAD_DIM
    kv = [w_in[:, nq + j * kvw: nq + (j + 1) * kvw] for j in range(6)]
    wg = w_in[:, nq + 6 * kvw:]
    wg = jnp.concatenate([wg, jnp.zeros((d, LANES - wg.shape[1]), w_in.dtype)], axis=-1)
    w = jnp.concatenate([w_in[:, :nq] * (SCALE * LOG2E), kv[2], kv[4], kv[0], kv[1], wg], axis=-1)
    wt = jnp.concatenate([kv[3], kv[5]], axis=-1).T
    cols = {"k_slc": nq // LANES, "k_win": (nq + kvw) // LANES, "k_cmp": nq + 2 * kvw,
            "v_cmp": nq + 3 * kvw, "gate": (nq + 4 * kvw) // LANES}
    return w.astype(BF16), wt.astype(BF16), cols


def _importance_matrix_np(n_slc, r):
    ratio = NSA_SLC_LEN // NSA_CMP_STRIDE
    span = NSA_CMP_LEN // NSA_CMP_STRIDE
    coef = np.convolve(np.ones(ratio), np.ones(span))
    cmat = np.zeros((n_slc, r), np.float32)
    for j in range(n_slc):
        for o, cf in enumerate(coef):
            c = ratio * j - o
            if c >= 0:
                cmat[j, c] = cf
    return cmat


def _nsa_mixer(h, g, b, s, w_in, cmp_pos, cmp_w1, cmp_w2, tab, bias_causal):
    w, wt, cols = _nsa_layout(w_in)
    z, zt = _norm_matmul(h, g, w, wt)
    z = z.reshape(b, s, -1)
    kvw = NSA_KV * HEAD_DIM
    r = s // NSA_CMP_STRIDE

    def rows_of_16(x):
        x = x.reshape(b, -1, NSA_KV, HEAD_DIM).transpose(2, 0, 1, 3)
        return x.reshape(NSA_KV, b, -1, NSA_CMP_STRIDE * HEAD_DIM)

    kv_c = jnp.stack([z[..., cols["k_cmp"]:cols["k_cmp"] + kvw], z[..., cols["v_cmp"]:cols["v_cmp"] + kvw]])
    nxt = jnp.concatenate([kv_c[:, :, NSA_CMP_STRIDE:], jnp.zeros_like(kv_c[:, :, :NSA_CMP_STRIDE])], axis=2)
    blocks = jnp.stack([rows_of_16(kv_c[0]), rows_of_16(kv_c[1])])
    blocks_next = jnp.stack([rows_of_16(nxt[0]), rows_of_16(nxt[1])])
    flat = NSA_CMP_LEN * HEAD_DIM
    cmp_tok, cmp_feat = _compress(blocks, blocks_next, cmp_w1.astype(BF16),
                                  cmp_pos.reshape(2, 1, flat).astype(BF16), cmp_w2.astype(BF16))

    n_slc = s // NSA_SLC_LEN
    cmat = jnp.asarray(_importance_matrix_np(n_slc, r), BF16)
    o_cmp, sel = _nsa_select(z, cmp_tok[0], cmp_feat[1], cmat, min(NSA_TOP_N, n_slc))
    win_tiles = _causal_bucket_tiles(ATT_T, -(-NSA_WINDOW // ATT_T) + 1, NSA_WINDOW)
    win_tiles = np.concatenate([win_tiles, np.full((1, ATT_T, ATT_T), MASKED_BUCKET, np.int32)], axis=0)
    bias_win = _build_bias_tiles(tab, win_tiles)
    return _nsa_attend(z, zt, sel, o_cmp, bias_causal, bias_win, cols)


def _scaled_qkv(w_in, scale=SCALE):
    return jnp.concatenate([w_in[:, :D_MODEL] * scale, w_in[:, D_MODEL:]], axis=1).astype(BF16)


def kernel(x, rel_bias, norm_gains, final_gain, a_w_in, a_lambda, a_subln, a_w_out, b_w_in, b_w_out,
           c_w_in, c_w_out, d_w_in, d_cmp_pos, d_cmp_w1, d_cmp_w2, d_w_out, ffn_w_gate, ffn_w_up,
           ffn_w_down, moe_router, moe_w_gate, moe_w_up, moe_w_down):
    b, s, d = x.shape
    assert d == D_MODEL and s % (B_PAIRS[-1][0]) == 0 and s >= (N_NEAR + 1) * ATT_T
    assert _t5_bucket_np(np.array([(N_NEAR - 1) * ATT_T + 1]))[0] == NUM_BUCKETS - 1
    n = b * s
    h = x.reshape(n, d)
    tab = rel_bias.T.astype(F32)
    tab_rel = (tab - tab[:, NUM_BUCKETS - 1:]) * LOG2E
    bias_causal = _build_bias_tiles(tab_rel, _sweep_bucket_tiles(ATT_T))

    w = _scaled_qkv(a_w_in[0], SCALE * LOG2E)
    z, zt = _norm_matmul(h, norm_gains[0, 0], w[:, :2 * d], w[:, 2 * d:].T)
    lam_init = 0.8 - 0.6 * math.exp(-0.3 * 0)
    o = _diff_attention(z.reshape(b, s, -1), zt, a_lambda[0].astype(F32), a_subln[0].astype(F32),
                        bias_causal, lam_init)
    h = _proj_residual(o.reshape(n, d), a_w_out[0].astype(BF16), h)
    h = _ffn(h, norm_gains[0, 1], ffn_w_gate[0].astype(BF16), ffn_w_up[0].astype(BF16),
             ffn_w_down[0].astype(BF16))

    dils = [dil for _, dil in B_PAIRS]
    views = _norm_matmul_views(h, norm_gains[1, 0], _scaled_qkv(b_w_in[0]), dils)
    outs, lses = [], []
    for zv, dil in zip(views, dils):
        bias = _build_bias_tiles(tab, _dilated_bucket_tiles(dil))
        o, lse = _dilated_group(zv.reshape(b, s // dil, -1), bias, dil)
        outs.append(o)
        lses.append(lse)
    h = _dil_combine_proj(outs, lses, dils, b_w_out[0].astype(BF16), h)
    h = _moe(h, norm_gains[1, 1], moe_router[0], moe_w_gate[0].astype(BF16), moe_w_up[0].astype(BF16),
             moe_w_down[0].astype(BF16), final_gain, False)

    w = _scaled_qkv(c_w_in[0])
    z, zt = _norm_matmul(h, norm_gains[2, 0], w[:, :2 * d], w[:, 2 * d:].T)
    o = _stick_attention(z.reshape(b, s, -1), zt)
    h = _proj_residual(o.reshape(n, d), c_w_out[0].astype(BF16), h)
    h = _ffn(h, norm_gains[2, 1], ffn_w_gate[1].astype(BF16), ffn_w_up[1].astype(BF16),
             ffn_w_down[1].astype(BF16))

    o = _nsa_mixer(h, norm_gains[3, 0], b, s, d_w_in[0], d_cmp_pos[0], d_cmp_w1[0], d_cmp_w2[0], tab_rel,
                   bias_causal)
    h = _proj_residual(o.reshape(n, d), d_w_out[0].astype(BF16), h)
    h = _moe(h, norm_gains[3, 1], moe_router[1], moe_w_gate[1].astype(BF16), moe_w_up[1].astype(BF16),
             moe_w_down[1].astype(BF16), final_gain, True)
    return h.reshape(b, s, d)
```

```python
import functools
import math

import numpy as np
import jax
import jax.numpy as jnp
from jax import lax
from jax.experimental import pallas as pl
from jax.experimental.pallas import tpu as pltpu

F32 = jnp.float32
BF16 = jnp.bfloat16

D_MODEL = 1024
HEAD_DIM = 64
N_HEADS = 16
LANES = 128
B_PAIRS = ((128, 1), (512, 4), (2048, 16))
DIL_W = 128
NSA_KV = 4
NSA_GROUP = 4
NSA_CMP_LEN = 32
NSA_CMP_STRIDE = 16
NSA_CMP_HID = 256
NSA_SLC_LEN = 64
NSA_TOP_N = 16
NSA_WINDOW = 512
NUM_BUCKETS = 32
MAX_EXACT = 16
REL_MAX_DIST = 2048
N_EXPERTS = 8
EPS = 1e-6
SCALE = HEAD_DIM ** -0.5
NEG = -1e30
MASKED_BUCKET = NUM_BUCKETS

ATT_T = 256
N_NEAR = 7
ATT_Q_TILES = 2
ATT_GROUP = 2
NSA_SWEEP_GROUP = 2
CMP_ROW_VARIANTS = 4
DEN_ROWS = 16
LOG2E = math.log2(math.e)
SB_T = 256
SB_Q_TILES = 2
SB_CUTOFF = -104.0
VMEM_LIMIT = 56 << 20


def _cparams(sem):
    return pltpu.CompilerParams(dimension_semantics=sem, vmem_limit_bytes=VMEM_LIMIT)


def _resident(block_shape, index_map):
    return pl.BlockSpec(block_shape, index_map, pipeline_mode=pl.Buffered(1))


def _nt_dot(a, b):
    return lax.dot_general(a, b, (((1,), (1,)), ((), ())), preferred_element_type=F32)


def _dot(a, b):
    return jnp.dot(a, b, preferred_element_type=F32)


def _split2(x):
    hi = x.astype(BF16)
    lo = (x - hi.astype(F32)).astype(BF16)
    return hi, lo


def _split3(x):
    hi = x.astype(BF16)
    rem = x - hi.astype(F32)
    mid = rem.astype(BF16)
    lo = (rem - mid.astype(F32)).astype(BF16)
    return hi, mid, lo


def _rms(x, g):
    ms = jnp.mean(x * x, axis=-1, keepdims=True)
    return x * lax.rsqrt(ms + EPS) * g


def _sigmoid(x):
    return 1.0 / (1.0 + jnp.exp(-x))


def _eye(n):
    r = lax.broadcasted_iota(jnp.int32, (n, n), 0)
    c = lax.broadcasted_iota(jnp.int32, (n, n), 1)
    return jnp.where(r == c, 1.0, 0.0).astype(BF16)


def _flip(eye, x):
    return _nt_dot(eye, x)


def _pad_rows(x, top):
    zero = jnp.zeros_like(x)
    return jnp.concatenate([x, zero] if top else [zero, x], axis=0)


def _t5_bucket_np(dist):
    n = np.maximum(dist, 0)
    ratio = np.log(np.maximum(n, 1).astype(np.float32) / np.float32(MAX_EXACT)) / np.float32(
        math.log(REL_MAX_DIST / MAX_EXACT))
    large = np.minimum(MAX_EXACT + (ratio * np.float32(NUM_BUCKETS - MAX_EXACT)).astype(np.int32),
                       NUM_BUCKETS - 1)
    return np.where(n < MAX_EXACT, n, large).astype(np.int32)


def _causal_bucket_tiles(t, n_tiles, max_dist=None):
    i = np.arange(t)[None, None, :]
    j = np.arange(t)[None, :, None]
    dist = np.arange(n_tiles)[:, None, None] * t + i - j
    ok = dist >= 0
    if max_dist is not None:
        ok = ok & (dist < max_dist)
    return np.where(ok, _t5_bucket_np(dist), MASKED_BUCKET).astype(np.int32)


def _sweep_bucket_tiles(t):
    far = np.full((1, t, t), NUM_BUCKETS - 1, np.int32)
    future = np.full((1, t, t), MASKED_BUCKET, np.int32)
    return np.concatenate([_causal_bucket_tiles(t, N_NEAR), far, future], axis=0)


def _bias_tile_index(delta):
    return jnp.where(delta < 0, N_NEAR + 1, jnp.minimum(delta, N_NEAR))


def _sweep_bias(bias_ref, col, first_q, nq, first_key, count):
    return jnp.concatenate(
        [jnp.concatenate([bias_ref[col, _bias_tile_index(first_q + a - (first_key + u))] for a in range(nq)], axis=1)
         for u in range(count)], axis=0)


def _dilated_bucket_tiles(dilation):
    w = DIL_W
    steps = w + np.arange(w)[None, :] - np.arange(2 * w)[:, None]
    ok = (steps >= 0) & (steps <= w)
    bkt = np.where(ok, _t5_bucket_np(dilation * steps), MASKED_BUCKET)
    first = np.where(np.arange(2 * w)[:, None] < w, MASKED_BUCKET, bkt)
    return np.stack([first, bkt]).astype(np.int32)


def _bias_build_kernel(tab_ref, bkt_ref, o_ref):
    c = pl.program_id(0)
    bkt = bkt_ref[0]
    out = jnp.full(bkt.shape, NEG, F32)
    for b in range(NUM_BUCKETS):
        out = jnp.where(bkt == b, tab_ref[c, b], out)
    o_ref[0, 0] = out


def _build_bias_tiles(tab, buckets):
    n, r, c = buckets.shape
    return pl.pallas_call(
        _bias_build_kernel,
        out_shape=jax.ShapeDtypeStruct((N_HEADS, n, r, c), F32),
        grid=(N_HEADS, n),
        in_specs=[pl.BlockSpec(memory_space=pltpu.SMEM),
                  pl.BlockSpec((1, r, c), lambda h, i: (i, 0, 0))],
        out_specs=pl.BlockSpec((1, 1, r, c), lambda h, i: (h, i, 0, 0)),
        compiler_params=_cparams(("arbitrary", "arbitrary")),
        name="bias_tiles",
    )(tab, jnp.asarray(buckets))


def _norm_matmul_kernel(x_ref, g_ref, w_ref, *refs, chunk, feature_major):
    xn = _rms(x_ref[...], g_ref[...]).astype(BF16)
    o_ref = refs[1] if feature_major else refs[0]
    nout = o_ref.shape[1]
    for c0 in range(0, nout, chunk):
        c1 = min(c0 + chunk, nout)
        o_ref[:, c0:c1] = _dot(xn, w_ref[:, c0:c1]).astype(o_ref.dtype)
    if feature_major:
        wt_ref, ot_ref = refs[0], refs[2]
        nt = ot_ref.shape[0]
        for c0 in range(0, nt, chunk):
            c1 = min(c0 + chunk, nt)
            ot_ref[c0:c1, :] = _nt_dot(wt_ref[c0:c1, :], xn).astype(ot_ref.dtype)


def _norm_matmul(x, g, w, wt=None, tm=512):
    n, d = x.shape
    nout = w.shape[1]
    in_specs = [pl.BlockSpec((tm, d), lambda i: (i, 0)),
                pl.BlockSpec((1, d), lambda i: (0, 0)),
                pl.BlockSpec((d, nout), lambda i: (0, 0))]
    out_shape = [jax.ShapeDtypeStruct((n, nout), BF16)]
    out_specs = [pl.BlockSpec((tm, nout), lambda i: (i, 0))]
    args = [x, g.reshape(1, d), w]
    if wt is not None:
        nt = wt.shape[0]
        in_specs.append(pl.BlockSpec((nt, d), lambda i: (0, 0)))
        out_shape.append(jax.ShapeDtypeStruct((nt, n), BF16))
        out_specs.append(pl.BlockSpec((nt, tm), lambda i: (0, i)))
        args.append(wt)
    out = pl.pallas_call(
        functools.partial(_norm_matmul_kernel, chunk=512, feature_major=wt is not None),
        out_shape=tuple(out_shape),
        grid=(n // tm,),
        in_specs=in_specs,
        out_specs=tuple(out_specs),
        compiler_params=_cparams(("parallel",)),
        name="norm_matmul",
    )(*args)
    return out if wt is not None else out[0]


def _norm_matmul_views_kernel(x_ref, g_ref, w_ref, *refs, chunk, dils):
    out_refs, z_ref = refs[:len(dils)], refs[len(dils)]
    xn = _rms(x_ref[...], g_ref[...]).astype(BF16)
    n_col, tm, _ = z_ref.shape
    nout = n_col * LANES
    for c0 in range(0, nout, chunk):
        c1 = min(c0 + chunk, nout)
        zc = _dot(xn, w_ref[:, c0:c1])
        for k in range((c1 - c0) // LANES):
            z_ref[c0 // LANES + k] = zc[:, k * LANES:(k + 1) * LANES]
    for o_ref, d in zip(out_refs, dils):
        for r in range(d):
            for c in range(n_col):
                col = r * nout + c * LANES
                o_ref[:, col:col + LANES] = z_ref[c, pl.ds(r, tm // d, stride=d), :].astype(o_ref.dtype)


def _norm_matmul_views(x, g, w, dils, tm=512):
    n, d_in = x.shape
    nout = w.shape[1]
    return pl.pallas_call(
        functools.partial(_norm_matmul_views_kernel, chunk=512, dils=tuple(dils)),
        out_shape=tuple(jax.ShapeDtypeStruct((n // d, d * nout), BF16) for d in dils),
        grid=(n // tm,),
        in_specs=[pl.BlockSpec((tm, d_in), lambda i: (i, 0)),
                  pl.BlockSpec((1, d_in), lambda i: (0, 0)),
                  pl.BlockSpec((d_in, nout), lambda i: (0, 0))],
        out_specs=tuple(pl.BlockSpec((tm // d, d * nout), lambda i: (i, 0)) for d in dils),
        scratch_shapes=[pltpu.VMEM((nout // LANES, tm, LANES), F32)],
        compiler_params=_cparams(("parallel",)),
        name="norm_matmul_views",
    )(x, g.reshape(1, d_in), w)


def _proj_res_kernel(a_ref, w_ref, h_ref, o_ref):
    o_ref[...] = h_ref[...] + _dot(a_ref[...], w_ref[...])


def _proj_residual(a, w, h, tm=512):
    n, d = h.shape
    k = a.shape[1]
    return pl.pallas_call(
        _proj_res_kernel,
        out_shape=jax.ShapeDtypeStruct((n, d), F32),
        grid=(n // tm,),
        in_specs=[pl.BlockSpec((tm, k), lambda i: (i, 0)),
                  pl.BlockSpec((k, d), lambda i: (0, 0)),
                  pl.BlockSpec((tm, d), lambda i: (i, 0))],
        out_specs=pl.BlockSpec((tm, d), lambda i: (i, 0)),
        compiler_params=_cparams(("parallel",)),
        name="proj_residual",
    )(a, w, h)


def _dil_combine_proj_kernel(o1, o2, o3, l1, l2, l3, w_ref, h_ref, o_ref, nat_ref, *, dils):
    tm = h_ref.shape[0]
    d_model = h_ref.shape[1]

    def in_position_order(view_ref, slot, d):
        if d == 1:
            return view_ref[...].astype(F32)
        for r in range(d):
            for c in range(d_model // LANES):
                col = r * d_model + c * LANES
                nat_ref[slot, c, pl.ds(r, tm // d, stride=d), :] = view_ref[:, col:col + LANES].astype(F32)
        return jnp.concatenate([nat_ref[slot, c] for c in range(d_model // LANES)], axis=1)

    outs = [in_position_order(ref, 2 * gi, d) for gi, (ref, d) in enumerate(zip((o1, o2, o3), dils))]
    ls = [in_position_order(ref, 2 * gi + 1, d) for gi, (ref, d) in enumerate(zip((l1, l2, l3), dils))]
    m = jnp.maximum(jnp.maximum(ls[0], ls[1]), ls[2])
    es = [jnp.exp(l - m) for l in ls]
    den = es[0] + es[1] + es[2]
    a = (es[0] * outs[0] + es[1] * outs[1] + es[2] * outs[2]) / den
    o_ref[...] = h_ref[...] + _dot(a.astype(BF16), w_ref[...])


def _dil_combine_proj(outs, lses, dils, w, h, tm=512):
    n, d = h.shape
    row = pl.BlockSpec((tm, d), lambda i: (i, 0))
    views = [pl.BlockSpec((tm // dil, dil * d), lambda i: (i, 0)) for dil in dils]
    return pl.pallas_call(
        functools.partial(_dil_combine_proj_kernel, dils=tuple(dils)),
        out_shape=jax.ShapeDtypeStruct((n, d), F32),
        grid=(n // tm,),
        in_specs=views + views + [pl.BlockSpec((d, d), lambda i: (0, 0)), row],
        out_specs=row,
        scratch_shapes=[pltpu.VMEM((2 * len(dils), d // LANES, tm, LANES), F32)],
        compiler_params=_cparams(("parallel",)),
        name="dilated_combine_proj",
    )(*outs, *lses, w, h)


def _ffn_kernel(h_ref, g_ref, wg_ref, wu_ref, wd_ref, o_ref, xn_ref, acc_ref):
    f = pl.program_id(1)

    @pl.when(f == 0)
    def _():
        xn_ref[...] = _rms(h_ref[...], g_ref[...]).astype(BF16)
        acc_ref[...] = jnp.zeros_like(acc_ref)

    xn = xn_ref[...]
    gate = _dot(xn, wg_ref[...])
    up = _dot(xn, wu_ref[...])
    act = (gate * _sigmoid(gate) * up).astype(BF16)
    acc_ref[...] += _dot(act, wd_ref[...])

    @pl.when(f == pl.num_programs(1) - 1)
    def _():
        o_ref[...] = h_ref[...] + acc_ref[...]


def _ffn(h, g, wg, wu, wd, tm=1024, n_f=2):
    n, d = h.shape
    dff = wg.shape[1]
    tf = dff // n_f
    return pl.pallas_call(
        _ffn_kernel,
        out_shape=jax.ShapeDtypeStruct((n, d), F32),
        grid=(n // tm, n_f),
        in_specs=[pl.BlockSpec((tm, d), lambda i, f: (i, 0)),
                  pl.BlockSpec((1, d), lambda i, f: (0, 0)),
                  pl.BlockSpec((d, tf), lambda i, f: (0, f)),
                  pl.BlockSpec((d, tf), lambda i, f: (0, f)),
                  pl.BlockSpec((tf, d), lambda i, f: (f, 0))],
        out_specs=pl.BlockSpec((tm, d), lambda i, f: (i, 0)),
        scratch_shapes=[pltpu.VMEM((tm, d), BF16), pltpu.VMEM((tm, d), F32)],
        compiler_params=_cparams(("parallel", "arbitrary")),
        name="ffn",
    )(h, g.reshape(1, d), wg, wu, wd)


def _moe_kernel(h_ref, g_ref, rh_ref, rl_ref, wg_ref, wu_ref, wd_ref, fg_ref, o_ref,
                xn_ref, comb_ref, acc_ref, *, final_norm):
    e = pl.program_id(1)
    col = lax.broadcasted_iota(jnp.int32, comb_ref.shape, 1)

    @pl.when(e == 0)
    def _():
        xn = _rms(h_ref[...], g_ref[...])
        xh, xl = _split2(xn)
        xn_ref[...] = xh
        logits = _dot(xh, rh_ref[...]) + _dot(xh, rl_ref[...]) + _dot(xl, rh_ref[...])
        colf = col.astype(F32)
        lg = jnp.where(col < N_EXPERTS, logits, NEG)
        m1 = jnp.max(lg, axis=-1, keepdims=True)
        i1 = jnp.min(jnp.where(lg == m1, colf, float(LANES)), axis=-1, keepdims=True)
        lg2 = jnp.where(colf == i1, NEG, lg)
        m2 = jnp.max(lg2, axis=-1, keepdims=True)
        i2 = jnp.min(jnp.where(lg2 == m2, colf, float(LANES)), axis=-1, keepdims=True)
        e2 = jnp.exp(m2 - m1)
        g1 = 1.0 / (1.0 + e2)
        g2 = e2 / (1.0 + e2)
        comb_ref[...] = jnp.where(colf == i1, g1, 0.0) + jnp.where(colf == i2, g2, 0.0)
        acc_ref[...] = jnp.zeros_like(acc_ref)

    xn = xn_ref[...]
    gate = _dot(xn, wg_ref[0])
    up = _dot(xn, wu_ref[0])
    ce = jnp.sum(jnp.where(col == e, comb_ref[...], 0.0), axis=-1, keepdims=True)
    act = (gate * _sigmoid(gate) * up * ce).astype(BF16)
    acc_ref[...] += _dot(act, wd_ref[0])

    @pl.when(e == pl.num_programs(1) - 1)
    def _():
        y = h_ref[...] + acc_ref[...]
        if final_norm:
            y = _rms(y, fg_ref[...])
        o_ref[...] = y


def _moe(h, g, router, wg, wu, wd, final_gain, final_norm, tm=1024):
    n, d = h.shape
    ne, _, dfe = wg.shape
    rpad = jnp.zeros((d, LANES), F32).at[:, :ne].set(router)
    rh, rl = _split2(rpad)
    return pl.pallas_call(
        functools.partial(_moe_kernel, final_norm=final_norm),
        out_shape=jax.ShapeDtypeStruct((n, d), F32),
        grid=(n // tm, ne),
        in_specs=[pl.BlockSpec((tm, d), lambda i, e: (i, 0)),
                  pl.BlockSpec((1, d), lambda i, e: (0, 0)),
                  pl.BlockSpec((d, LANES), lambda i, e: (0, 0)),
                  pl.BlockSpec((d, LANES), lambda i, e: (0, 0)),
                  pl.BlockSpec((1, d, dfe), lambda i, e: (e, 0, 0)),
                  pl.BlockSpec((1, d, dfe), lambda i, e: (e, 0, 0)),
                  pl.BlockSpec((1, dfe, d), lambda i, e: (e, 0, 0)),
                  pl.BlockSpec((1, d), lambda i, e: (0, 0))],
        out_specs=pl.BlockSpec((tm, d), lambda i, e: (i, 0)),
        scratch_shapes=[pltpu.VMEM((tm, d), BF16), pltpu.VMEM((tm, LANES), F32),
                        pltpu.VMEM((tm, d), F32)],
        compiler_params=_cparams(("parallel", "arbitrary")),
        name="moe",
    )(h, g.reshape(1, d), rh, rl, wg, wu, wd, final_gain.reshape(1, d))


def _with_ones(vt):
    return jnp.concatenate([vt, jnp.ones((DEN_ROWS, vt.shape[1]), vt.dtype)], axis=0)


def _flash_step(s, vt1, m_ref, acc_ref, idx):
    m_old = m_ref[idx]
    m_new = jnp.maximum(m_old, jnp.max(s, axis=0, keepdims=True))
    p = jnp.exp2(s - m_new).astype(BF16)
    acc_ref[idx] = jnp.exp2(m_old - m_new) * acc_ref[idx] + _dot(vt1, p)
    m_ref[idx] = m_new


def _flash_result(acc_ref, idx):
    acc = acc_ref[idx]
    nf = acc.shape[0] - DEN_ROWS
    return acc[:nf] / acc[nf:nf + 1]


def _far_groups(qi, group):
    return lax.shift_right_logical(jnp.maximum(qi + 1 - N_NEAR, 0), int(math.log2(group)))


def _pipelined_sweep(n_far, n_groups, scores, softmax, sa_ref, sb_ref):
    scores(0, sa_ref)

    def pairs(near):
        def pair(k, carry):
            scores(2 * k + 1, sb_ref)
            softmax(2 * k, sa_ref, near)
            scores(2 * k + 2, sa_ref)
            softmax(2 * k + 1, sb_ref, near)
            return carry
        return pair

    far_pairs = lax.shift_right_logical(n_far, 1)
    lax.fori_loop(0, far_pairs, pairs(False), 0)
    lax.fori_loop(far_pairs, lax.shift_right_logical(n_groups + 1, 1), pairs(True), 0)


def _flash_init(m_ref, acc_ref):
    m_ref[...] = jnp.full(m_ref.shape, NEG, F32)
    acc_ref[...] = jnp.zeros_like(acc_ref)


def _diff_attn_kernel(lam_ref, q_ref, k_ref, vt_ref, bias_ref, g_ref, o_ref, m_ref, acc_ref, sa_ref, sb_ref,
                      *, t, nq, lam_init):
    qi = pl.program_id(2)
    tq = nq * t
    qt = _flip(_eye(LANES), q_ref[...]).astype(BF16)
    rhs = jnp.concatenate([_pad_rows(qt[:HEAD_DIM], True), _pad_rows(qt[HEAD_DIM:], False)], axis=1)
    _flash_init(m_ref, acc_ref)
    gt = ATT_GROUP * t
    first_q = qi * nq
    n_groups = lax.shift_right_logical(first_q + nq - 1 + ATT_GROUP, int(math.log2(ATT_GROUP)))

    def key_start(gi):
        return pl.multiple_of(jnp.minimum(gi, n_groups - 1) * gt, gt)

    def scores(gi, s_ref):
        s_ref[...] = _dot(k_ref[pl.ds(key_start(gi), gt), :], rhs)

    def softmax(gi, s_ref, near):
        vt1 = _with_ones(vt_ref[:, pl.ds(key_start(gi), gt)])
        for mi in range(2):
            s = s_ref[:, mi * tq:(mi + 1) * tq]
            if near:
                s = s + _sweep_bias(bias_ref, mi, first_q, nq, gi * ATT_GROUP, ATT_GROUP)
            _flash_step(s, vt1, m_ref, acc_ref, mi)

    _pipelined_sweep(_far_groups(first_q, ATT_GROUP), n_groups, scores, softmax, sa_ref, sb_ref)

    lam = lam_ref[...]
    lam_full = (jnp.exp(jnp.sum(lam[0:1] * lam[1:2], axis=-1, keepdims=True))
                - jnp.exp(jnp.sum(lam[2:3] * lam[3:4], axis=-1, keepdims=True)) + lam_init)
    o = _flash_result(acc_ref, 0) - lam_full * _flash_result(acc_ref, 1)
    ms = jnp.mean(o * o, axis=0, keepdims=True)
    y = (o * lax.rsqrt(ms + EPS) * g_ref[...] * (1.0 - lam_init)).astype(BF16)
    o_ref[...] = _flip(_eye(tq), y).astype(o_ref.dtype)


def _diff_attention(z, zt, lam, subln, bias_tiles, lam_init):
    b, s, _ = z.shape
    t = ATT_T
    tq = ATT_Q_TILES * t
    nh = N_HEADS // 2
    n_tiles = bias_tiles.shape[1]
    return pl.pallas_call(
        functools.partial(_diff_attn_kernel, t=t, nq=ATT_Q_TILES, lam_init=lam_init),
        out_shape=jax.ShapeDtypeStruct((b, s, D_MODEL), BF16),
        grid=(b, nh, s // tq),
        in_specs=[pl.BlockSpec((4, HEAD_DIM), lambda bi, h, qi: (0, 0)),
                  pl.BlockSpec((None, tq, LANES), lambda bi, h, qi: (bi, qi, h)),
                  _resident((None, s, LANES), lambda bi, h, qi: (bi, 0, nh + h)),
                  _resident((LANES, s), lambda bi, h, qi: (h, bi)),
                  _resident((2, n_tiles, t, t), lambda bi, h, qi: (h, 0, 0, 0)),
                  pl.BlockSpec((LANES, 1), lambda bi, h, qi: (0, 0))],
        out_specs=pl.BlockSpec((None, tq, LANES), lambda bi, h, qi: (bi, qi, h)),
        scratch_shapes=[pltpu.VMEM((2, 1, tq), F32), pltpu.VMEM((2, LANES + DEN_ROWS, tq), F32),
                        pltpu.VMEM((ATT_GROUP * t, 2 * tq), F32), pltpu.VMEM((ATT_GROUP * t, 2 * tq), F32)],
        compiler_params=_cparams(("parallel", "parallel", "arbitrary")),
        name="diff_attention",
    )(lam, z, z, zt, bias_tiles, subln.reshape(LANES, 1))


def _dilated_kernel(q_ref, kp_ref, kc_ref, vp_ref, vc_ref, bias_ref, o_ref, lse_ref):
    n = pl.program_id(2)
    variant = jnp.minimum(n, 1)
    w = q_ref.shape[0]
    eye = _eye(LANES)
    eye_w = _eye(w)
    pairs = range(N_HEADS // 2)
    cols = [slice(hp * LANES, (hp + 1) * LANES) for hp in pairs]
    qts = [_flip(eye, q_ref[:, c]).astype(BF16) for c in cols]
    kcats = [jnp.concatenate([kp_ref[:, c], kc_ref[:, c]], axis=0) for c in cols]
    vts = [_flip(eye, jnp.concatenate([vp_ref[:, c], vc_ref[:, c]], axis=0)).astype(BF16) for c in cols]
    s2s = [_dot(kcats[hp], jnp.concatenate([_pad_rows(qts[hp][:HEAD_DIM], True),
                                            _pad_rows(qts[hp][HEAD_DIM:], False)], axis=1))
           for hp in pairs]
    ps, lses = [], []
    for hp in pairs:
        for sub in range(2):
            s = s2s[hp][:, sub * w:(sub + 1) * w] + bias_ref[2 * hp + sub, variant]
            m = jnp.max(s, axis=0, keepdims=True)
            e = jnp.exp(s - m)
            den = jnp.sum(e, axis=0, keepdims=True)
            ps.append((e * (1.0 / den)).astype(BF16))
            lses.append(jnp.broadcast_to(m + jnp.log(den), (HEAD_DIM, w)))
    outs = [_dot(vts[hp][sub * HEAD_DIM:(sub + 1) * HEAD_DIM], ps[2 * hp + sub])
            for hp in pairs for sub in range(2)]
    for hp in pairs:
        pair_out = jnp.concatenate(outs[2 * hp:2 * hp + 2], axis=0).astype(BF16)
        o_ref[:, cols[hp]] = _flip(eye_w, pair_out).astype(o_ref.dtype)
        pair_lse = jnp.concatenate(lses[2 * hp:2 * hp + 2], axis=0)
        lse_ref[:, cols[hp]] = sum(_flip(eye_w, term) for term in _split3(pair_lse))


def _dilated_group(zv, bias, dilation):
    b, m_len, _ = zv.shape
    w = DIL_W
    nb = m_len // w
    blk = lambda sect, prev: pl.BlockSpec(
        (None, w, D_MODEL),
        (lambda bi, r, n: (bi, jnp.maximum(n - 1, 0), 3 * r + sect)) if prev
        else (lambda bi, r, n: (bi, n, 3 * r + sect)))
    out_blk = pl.BlockSpec((None, w, D_MODEL), lambda bi, r, n: (bi, n, r))
    o, lse = pl.pallas_call(
        _dilated_kernel,
        out_shape=(jax.ShapeDtypeStruct((b, m_len, dilation * D_MODEL), BF16),
                   jax.ShapeDtypeStruct((b, m_len, dilation * D_MODEL), F32)),
        grid=(b, dilation, nb),
        in_specs=[blk(0, False), blk(1, True), blk(1, False), blk(2, True), blk(2, False),
                  pl.BlockSpec((N_HEADS, 2, 2 * w, w), lambda bi, r, n: (0, 0, 0, 0))],
        out_specs=(out_blk, out_blk),
        compiler_params=_cparams(("parallel", "parallel", "arbitrary")),
        name=f"dilated_d{dilation}",
    )(zv, zv, zv, zv, zv, bias)
    return o.reshape(b * m_len, dilation * D_MODEL), lse.reshape(b * m_len, dilation * D_MODEL)


def _stick_kernel(q_ref, k_ref, vt_ref, o_ref, acc_ref, carry_ref, *, t, nq):
    qi = pl.program_id(2)
    tq = nq * t
    first_q = qi * nq
    qt = _flip(_eye(LANES), q_ref[...]).astype(BF16)
    rhs = jnp.concatenate([_pad_rows(qt[:HEAD_DIM], True), _pad_rows(qt[HEAD_DIM:], False)], axis=1)
    row = lax.broadcasted_iota(jnp.int32, (t, t), 0)
    col = lax.broadcasted_iota(jnp.int32, (t, t), 1)
    after = jnp.where(col > row, 1.0, 0.0).astype(BF16)
    key_off = lax.broadcasted_iota(jnp.int32, (t, 2 * tq), 0)
    q_off = lax.broadcasted_iota(jnp.int32, (t, 2 * tq), 1) % tq
    acc_ref[...] = jnp.zeros_like(acc_ref)
    carry_ref[...] = jnp.zeros_like(carry_ref)

    def block(j, masked):
        start = pl.multiple_of(j * t, t)
        z = _dot(k_ref[pl.ds(start, t), :], rhs)
        log_keep = -(jnp.maximum(z, 0.0) + jnp.log(1.0 + jnp.exp(-jnp.abs(z))))
        if masked:
            before = key_off + (j - first_q) * t < q_off
            log_keep = jnp.where(before, log_keep, 0.0)
        hi, lo = _split2(log_keep)
        later = _dot(after, hi) + _dot(after, lo) + carry_ref[...]
        a = jnp.exp(z + log_keep + later)
        if masked:
            a = jnp.where(before, a, 0.0)
        a = a.astype(BF16)
        for sub in range(2):
            vt = vt_ref[sub * HEAD_DIM:(sub + 1) * HEAD_DIM, pl.ds(start, t)]
            acc_ref[sub] += _dot(vt, a[:, sub * tq:(sub + 1) * tq])
        carry_ref[...] += jnp.sum(log_keep, axis=0, keepdims=True)

    for a_tile in reversed(range(nq)):
        block(first_q + a_tile, True)

    def cond(c):
        return jnp.logical_and(c[0] >= 0, c[1] >= SB_CUTOFF)

    def body(c):
        block(c[0], False)
        return c[0] - 1, jnp.max(carry_ref[...])

    lax.while_loop(cond, body, (first_q - 1, jnp.max(carry_ref[...])))
    out = jnp.concatenate([acc_ref[0], acc_ref[1]], axis=0).astype(BF16)
    o_ref[...] = _flip(_eye(tq), out).astype(o_ref.dtype)


def _stick_attention(z, zt):
    b, s, _ = z.shape
    t = SB_T
    tq = SB_Q_TILES * t
    nh = N_HEADS // 2
    return pl.pallas_call(
        functools.partial(_stick_kernel, t=t, nq=SB_Q_TILES),
        out_shape=jax.ShapeDtypeStruct((b, s, D_MODEL), BF16),
        grid=(b, nh, s // tq),
        in_specs=[pl.BlockSpec((None, tq, LANES), lambda bi, h, qi: (bi, qi, h)),
                  _resident((None, s, LANES), lambda bi, h, qi: (bi, 0, nh + h)),
                  _resident((LANES, s), lambda bi, h, qi: (h, bi))],
        out_specs=pl.BlockSpec((None, tq, LANES), lambda bi, h, qi: (bi, qi, h)),
        scratch_shapes=[pltpu.VMEM((2, HEAD_DIM, tq), F32), pltpu.VMEM((1, 2 * tq), F32)],
        compiler_params=_cparams(("parallel", "parallel", "arbitrary")),
        name="stick_breaking",
    )(z, z, zt)


def _compress_kernel(a_ref, as_ref, w1_ref, pos_ref, w2_ref, w2t_ref, o_ref, ot_ref):
    half = a_ref.shape[1]
    w1 = w1_ref[...]
    const = _dot(jnp.broadcast_to(pos_ref[...], (8, 2 * half)), w1)[0:1]
    pre = _dot(a_ref[...], w1[:half]) + _dot(as_ref[...], w1[half:]) + const
    hid = 0.5 * pre * (1.0 + jnp.tanh(math.sqrt(2.0 / math.pi) * (pre + 0.044715 * pre * pre * pre)))
    hid = hid.astype(BF16)
    o_ref[...] = _dot(hid, w2_ref[...]).astype(o_ref.dtype)
    ot_ref[...] = _nt_dot(w2t_ref[...], hid).astype(ot_ref.dtype)


def _compress(blocks, blocks_next, w1, pos, w2, tr=256):
    _, nkv, b, r, half = blocks.shape
    tr = min(tr, r)
    w2pad = jnp.concatenate([w2, jnp.zeros_like(w2)], axis=-1)
    w2t = jnp.swapaxes(w2, 1, 2)
    spec = pl.BlockSpec((None, None, None, tr, half), lambda j, hk, bi, ri: (j, hk, bi, ri, 0))
    return pl.pallas_call(
        _compress_kernel,
        out_shape=(jax.ShapeDtypeStruct((2, nkv, b, r, LANES), BF16),
                   jax.ShapeDtypeStruct((2, nkv, b, HEAD_DIM, r), BF16)),
        grid=(2, nkv, b, r // tr),
        in_specs=[spec, spec,
                  pl.BlockSpec((None, 2 * half, NSA_CMP_HID), lambda j, hk, bi, ri: (j, 0, 0)),
                  pl.BlockSpec((None, 1, 2 * half), lambda j, hk, bi, ri: (j, 0, 0)),
                  pl.BlockSpec((None, NSA_CMP_HID, LANES), lambda j, hk, bi, ri: (j, 0, 0)),
                  pl.BlockSpec((None, HEAD_DIM, NSA_CMP_HID), lambda j, hk, bi, ri: (j, 0, 0))],
        out_specs=(pl.BlockSpec((None, None, None, tr, LANES), lambda j, hk, bi, ri: (j, hk, bi, ri, 0)),
                   pl.BlockSpec((None, None, None, HEAD_DIM, tr), lambda j, hk, bi, ri: (j, hk, bi, 0, ri))),
        compiler_params=_cparams(("parallel", "parallel", "parallel", "arbitrary")),
        name="nsa_compress",
    )(blocks, blocks_next, w1, pos, w2pad, w2t)


def _nsa_select_kernel(q_ref, kc_ref, vct_ref, cmat_ref, o_ref, sel_ref, *, t, n_sel):
    qi = pl.program_id(2)
    r = kc_ref.shape[0]
    n_slc = sel_ref.shape[0]
    qt = _flip(_eye(NSA_GROUP * HEAD_DIM), q_ref[...]).astype(BF16)
    tpos = qi * t + lax.broadcasted_iota(jnp.int32, (1, t), 1)
    live = (tpos >= NSA_CMP_LEN - 1).astype(F32)

    def compressed_branch(rows):
        kc = kc_ref[:rows, :]
        vct = vct_ref[:, :rows]
        cidx = lax.broadcasted_iota(jnp.int32, (rows, t), 0)
        seen = NSA_CMP_STRIDE * cidx + (NSA_CMP_LEN - 1) <= tpos
        imp = jnp.zeros((rows, t), F32)
        for g in range(NSA_GROUP):
            rhs = _pad_rows(qt[g * HEAD_DIM:(g + 1) * HEAD_DIM], True)
            sc = jnp.where(seen, _dot(kc, rhs), NEG)
            e = jnp.exp2(sc - jnp.max(sc, axis=0, keepdims=True))
            pc = e * (live / jnp.sum(e, axis=0, keepdims=True))
            o_ref[g * HEAD_DIM:(g + 1) * HEAD_DIM, :] = _dot(vct, pc.astype(BF16)).astype(o_ref.dtype)
            imp = imp + pc
        n_blk = rows // (NSA_SLC_LEN // NSA_CMP_STRIDE)
        cmat = cmat_ref[:n_blk, :rows]
        imp_slc = sum(_dot(cmat, term) for term in _split3(imp))
        blk = lax.broadcasted_iota(jnp.int32, (n_blk, t), 0)
        cur = tpos // NSA_SLC_LEN
        forced = (blk == 0) | (blk == cur) | (blk == cur - 1)
        valid = blk * NSA_SLC_LEN <= tpos
        ranked = jnp.logical_and(valid, jnp.logical_not(forced))
        score = jnp.where(ranked, imp_slc, -1.0)
        blkf = blk.astype(F32)
        for _ in range(n_sel - 3):
            top = jnp.max(score, axis=0, keepdims=True)
            first = jnp.min(jnp.where(score == top, blkf, float(n_blk)), axis=0, keepdims=True)
            score = jnp.where(blkf == first, -jnp.inf, score)
        chosen = jnp.logical_or(jnp.logical_and(valid, forced), jnp.logical_and(ranked, score == -jnp.inf))
        sel_ref[:n_blk, :] = jnp.where(chosen, 1.0, 0.0).astype(sel_ref.dtype)
        if n_blk < n_slc:
            sel_ref[n_blk:, :] = jnp.zeros((n_slc - n_blk, t), sel_ref.dtype)

    quarter = r // CMP_ROW_VARIANTS
    need = (qi * t + t - NSA_CMP_LEN) // NSA_CMP_STRIDE
    variant = jnp.clip(need // quarter, 0, CMP_ROW_VARIANTS - 1)
    for v in range(CMP_ROW_VARIANTS):
        @pl.when(variant == v)
        def _(v=v):
            compressed_branch((v + 1) * quarter)


def _nsa_select(z, kc, vct, cmat, n_sel):
    b, s, _ = z.shape
    t = ATT_T
    r = kc.shape[2]
    n_slc = cmat.shape[0]
    gw = NSA_GROUP * HEAD_DIM
    assert r % CMP_ROW_VARIANTS == 0 and (r // CMP_ROW_VARIANTS) % 16 == 0 and t % 16 == 0
    return pl.pallas_call(
        functools.partial(_nsa_select_kernel, t=t, n_sel=n_sel),
        out_shape=(jax.ShapeDtypeStruct((b, NSA_KV, gw, s), BF16),
                   jax.ShapeDtypeStruct((b, NSA_KV, n_slc, s), BF16)),
        grid=(b, NSA_KV, s // t),
        in_specs=[pl.BlockSpec((None, t, gw), lambda bi, hk, qi: (bi, qi, hk)),
                  pl.BlockSpec((None, None, r, LANES), lambda bi, hk, qi: (hk, bi, 0, 0)),
                  pl.BlockSpec((None, None, HEAD_DIM, r), lambda bi, hk, qi: (hk, bi, 0, 0)),
                  pl.BlockSpec((n_slc, r), lambda bi, hk, qi: (0, 0))],
        out_specs=(pl.BlockSpec((None, None, gw, t), lambda bi, hk, qi: (bi, hk, 0, qi)),
                   pl.BlockSpec((None, None, n_slc, t), lambda bi, hk, qi: (bi, hk, 0, qi))),
        compiler_params=_cparams(("parallel", "parallel", "arbitrary")),
        name="nsa_select",
    )(z, kc, vct, cmat)


def _nsa_attend_kernel(q_ref, ks_ref, kw_ref, vst_ref, vwt_ref, sel_ref, ocmp_ref, zg_ref, bs_ref, bw_ref,
                       o_ref, m_ref, acc_ref, res_ref, gate_ref, sel32_ref, sa_ref, sb_ref, *, t, nq, n_win):
    hk = pl.program_id(1)
    qi = pl.program_id(2)
    tq = nq * t
    first_q = qi * nq
    per_tile = t // NSA_SLC_LEN
    gw = NSA_GROUP * HEAD_DIM
    qt = _flip(_eye(gw), q_ref[...]).astype(BF16)
    low_half = (hk % 2) == 0
    q_heads = [qt[g * HEAD_DIM:(g + 1) * HEAD_DIM] for g in range(NSA_GROUP)]
    sel32_ref[...] = sel_ref[...].astype(F32)

    _flash_init(m_ref, acc_ref)
    grp = NSA_SWEEP_GROUP
    gt = grp * t
    n_groups = lax.shift_right_logical(first_q + nq - 1 + grp, int(math.log2(grp)))

    def first_tile(gi):
        return jnp.minimum(gi, n_groups - 1) * grp

    n_blk = grp * per_tile
    lane = lax.broadcasted_iota(jnp.int32, (gt, LANES), 1)
    blk_of_key = lax.broadcasted_iota(jnp.int32, (gt, LANES), 0) // NSA_SLC_LEN
    key_lanes = lax.shift_right_logical(lane, int(math.log2(HEAD_DIM))) == hk % 2
    one_hot = jnp.where(lane == blk_of_key + jnp.where(low_half, HEAD_DIM, 0), 1.0, 0.0).astype(BF16)

    def with_extra_rows(extra):
        return jnp.concatenate(
            [jnp.where(low_half, jnp.concatenate([qg, extra], axis=0), jnp.concatenate([extra, qg], axis=0))
             for qg in q_heads], axis=1)

    def scores(gi, s_ref):
        j0 = first_tile(gi)
        keys = jnp.where(key_lanes, ks_ref[pl.ds(pl.multiple_of(j0 * t, t), gt), :], one_hot)
        drop = (sel32_ref[pl.ds(pl.multiple_of(j0 * per_tile, n_blk), n_blk), :] - 1.0) * (-NEG)
        drop = jnp.concatenate([drop, jnp.zeros((HEAD_DIM - n_blk, tq), F32)], axis=0).astype(BF16)
        s_ref[...] = _dot(keys, with_extra_rows(drop))

    def softmax(gi, s_ref, near):
        j0 = first_tile(gi)
        vt1 = _with_ones(vst_ref[:, pl.ds(pl.multiple_of(j0 * t, t), gt)])
        for g in range(NSA_GROUP):
            s = s_ref[:, g * tq:(g + 1) * tq]
            if near:
                s = s + _sweep_bias(bs_ref, g, first_q, nq, gi * grp, grp)
            _flash_step(s, vt1, m_ref, acc_ref, g)

    _pipelined_sweep(_far_groups(first_q, grp), n_groups, scores, softmax, sa_ref, sb_ref)
    for g in range(NSA_GROUP):
        res_ref[g] = _flash_result(acc_ref, g)

    _flash_init(m_ref, acc_ref)
    rhs_win = with_extra_rows(jnp.zeros((HEAD_DIM, tq), BF16))
    w_first = first_q - (n_win - 1)
    for wg in range(-(-(nq + n_win - 1) // grp)):
        j0 = w_first + wg * grp
        start = pl.multiple_of(jnp.maximum(j0, 0) * t, t)
        sa_ref[...] = _dot(kw_ref[pl.ds(start, gt), :], rhs_win)
        vt1 = _with_ones(vwt_ref[:, pl.ds(start, gt)])
        for g in range(NSA_GROUP):
            rows = []
            for u in range(grp):
                tiles = []
                for a in range(nq):
                    delta = first_q + a - (j0 + u)
                    inside = jnp.logical_and(jnp.logical_and(delta >= 0, delta < n_win), j0 + u >= 0)
                    tiles.append(bw_ref[g, jnp.where(inside, delta, n_win)])
                rows.append(jnp.concatenate(tiles, axis=1))
            _flash_step(sa_ref[:, g * tq:(g + 1) * tq] + jnp.concatenate(rows, axis=0), vt1, m_ref, acc_ref, g)

    gate_ref[...] = _sigmoid(_flip(_eye(LANES), zg_ref[...]))
    mixed = []
    for g in range(NSA_GROUP):
        base = 3 * (NSA_GROUP * hk + g)
        gates = [gate_ref[pl.ds(base + c, 1), :] for c in range(3)]
        o_cmp = ocmp_ref[g * HEAD_DIM:(g + 1) * HEAD_DIM, :].astype(F32)
        mixed.append(gates[0] * o_cmp + gates[1] * res_ref[g] + gates[2] * _flash_result(acc_ref, g))
    o_ref[...] = _flip(_eye(tq), jnp.concatenate(mixed, axis=0).astype(BF16)).astype(o_ref.dtype)


def _nsa_attend(z, zt, sel, o_cmp, bias_slc, bias_win, cols):
    b, s, _ = z.shape
    t = ATT_T
    tq = ATT_Q_TILES * t
    n_slc = sel.shape[2]
    gw = NSA_GROUP * HEAD_DIM
    n_win = bias_win.shape[1] - 1
    assert (n_win - 1) % NSA_SWEEP_GROUP == 0 and ATT_Q_TILES % NSA_SWEEP_GROUP == 0
    n_tiles = bias_slc.shape[1]
    c_slc, c_win, c_gate = cols["k_slc"], cols["k_win"], cols["gate"]
    return pl.pallas_call(
        functools.partial(_nsa_attend_kernel, t=t, nq=ATT_Q_TILES, n_win=n_win),
        out_shape=jax.ShapeDtypeStruct((b, s, D_MODEL), BF16),
        grid=(b, NSA_KV, s // tq),
        in_specs=[pl.BlockSpec((None, tq, gw), lambda bi, hk, qi: (bi, qi, hk)),
                  _resident((None, s, LANES), lambda bi, hk, qi: (bi, 0, c_slc + hk // 2)),
                  _resident((None, s, LANES), lambda bi, hk, qi: (bi, 0, c_win + hk // 2)),
                  _resident((HEAD_DIM, s), lambda bi, hk, qi: (hk, bi)),
                  _resident((HEAD_DIM, s), lambda bi, hk, qi: (NSA_KV + hk, bi)),
                  pl.BlockSpec((None, None, n_slc, tq), lambda bi, hk, qi: (bi, hk, 0, qi)),
                  pl.BlockSpec((None, None, gw, tq), lambda bi, hk, qi: (bi, hk, 0, qi)),
                  pl.BlockSpec((None, tq, LANES), lambda bi, hk, qi: (bi, qi, c_gate)),
                  _resident((NSA_GROUP, n_tiles, t, t), lambda bi, hk, qi: (hk, 0, 0, 0)),
                  _resident((NSA_GROUP, n_win + 1, t, t), lambda bi, hk, qi: (hk, 0, 0, 0))],
        out_specs=pl.BlockSpec((None, tq, gw), lambda bi, hk, qi: (bi, qi, hk)),
        scratch_shapes=[pltpu.VMEM((NSA_GROUP, 1, tq), F32),
                        pltpu.VMEM((NSA_GROUP, HEAD_DIM + DEN_ROWS, tq), F32),
                        pltpu.VMEM((NSA_GROUP, HEAD_DIM, tq), F32),
                        pltpu.VMEM((LANES, tq), F32), pltpu.VMEM((n_slc, tq), F32),
                        pltpu.VMEM((NSA_SWEEP_GROUP * t, NSA_GROUP * tq), F32),
                        pltpu.VMEM((NSA_SWEEP_GROUP * t, NSA_GROUP * tq), F32)],
        compiler_params=_cparams(("parallel", "parallel", "arbitrary")),
        name="nsa_attend",
    )(z, z, z, zt, zt, sel, o_cmp, z, bias_slc, bias_win)


def _nsa_layout(w_in):
    d = w_in.shape[0]
    nq = N_HEADS * HEAD_DIM
    kvw = NSA_KV * HEAD_DIM
    kv = [w_in[:, nq + j * kvw: nq + (j + 1) * kvw] for j in range(6)]
    wg = w_in[:, nq + 6 * kvw:]
    wg = jnp.concatenate([wg, jnp.zeros((d, LANES - wg.shape[1]), w_in.dtype)], axis=-1)
    w = jnp.concatenate([w_in[:, :nq] * (SCALE * LOG2E), kv[2], kv[4], kv[0], kv[1], wg], axis=-1)
    wt = jnp.concatenate([kv[3], kv[5]], axis=-1).T
    cols = {"k_slc": nq // LANES, "k_win": (nq + kvw) // LANES, "k_cmp": nq + 2 * kvw,
            "v_cmp": nq + 3 * kvw, "gate": (nq + 4 * kvw) // LANES}
    return w.astype(BF16), wt.astype(BF16), cols


def _importance_matrix_np(n_slc, r):
    ratio = NSA_SLC_LEN // NSA_CMP_STRIDE
    span = NSA_CMP_LEN // NSA_CMP_STRIDE
    coef = np.convolve(np.ones(ratio), np.ones(span))
    cmat = np.zeros((n_slc, r), np.float32)
    for j in range(n_slc):
        for o, cf in enumerate(coef):
            c = ratio * j - o
            if c >= 0:
                cmat[j, c] = cf
    return cmat


def _nsa_mixer(h, g, b, s, w_in, cmp_pos, cmp_w1, cmp_w2, tab, bias_causal):
    w, wt, cols = _nsa_layout(w_in)
    z, zt = _norm_matmul(h, g, w, wt)
    z = z.reshape(b, s, -1)
    kvw = NSA_KV * HEAD_DIM
    r = s // NSA_CMP_STRIDE

    def rows_of_16(x):
        x = x.reshape(b, -1, NSA_KV, HEAD_DIM).transpose(2, 0, 1, 3)
        return x.reshape(NSA_KV, b, -1, NSA_CMP_STRIDE * HEAD_DIM)

    kv_c = jnp.stack([z[..., cols["k_cmp"]:cols["k_cmp"] + kvw], z[..., cols["v_cmp"]:cols["v_cmp"] + kvw]])
    nxt = jnp.concatenate([kv_c[:, :, NSA_CMP_STRIDE:], jnp.zeros_like(kv_c[:, :, :NSA_CMP_STRIDE])], axis=2)
    blocks = jnp.stack([rows_of_16(kv_c[0]), rows_of_16(kv_c[1])])
    blocks_next = jnp.stack([rows_of_16(nxt[0]), rows_of_16(nxt[1])])
    flat = NSA_CMP_LEN * HEAD_DIM
    cmp_tok, cmp_feat = _compress(blocks, blocks_next, cmp_w1.astype(BF16),
                                  cmp_pos.reshape(2, 1, flat).astype(BF16), cmp_w2.astype(BF16))

    n_slc = s // NSA_SLC_LEN
    cmat = jnp.asarray(_importance_matrix_np(n_slc, r), BF16)
    o_cmp, sel = _nsa_select(z, cmp_tok[0], cmp_feat[1], cmat, min(NSA_TOP_N, n_slc))
    win_tiles = _causal_bucket_tiles(ATT_T, -(-NSA_WINDOW // ATT_T) + 1, NSA_WINDOW)
    win_tiles = np.concatenate([win_tiles, np.full((1, ATT_T, ATT_T), MASKED_BUCKET, np.int32)], axis=0)
    bias_win = _build_bias_tiles(tab, win_tiles)
    return _nsa_attend(z, zt, sel, o_cmp, bias_causal, bias_win, cols)


def _scaled_qkv(w_in, scale=SCALE):
    return jnp.concatenate([w_in[:, :D_MODEL] * scale, w_in[:, D_MODEL:]], axis=1).astype(BF16)


def kernel(x, rel_bias, norm_gains, final_gain, a_w_in, a_lambda, a_subln, a_w_out, b_w_in, b_w_out,
           c_w_in, c_w_out, d_w_in, d_cmp_pos, d_cmp_w1, d_cmp_w2, d_w_out, ffn_w_gate, ffn_w_up,
           ffn_w_down, moe_router, moe_w_gate, moe_w_up, moe_w_down):
    b, s, d = x.shape
    assert d == D_MODEL and s % (B_PAIRS[-1][0]) == 0 and s >= (N_NEAR + 1) * ATT_T
    assert _t5_bucket_np(np.array([(N_NEAR - 1) * ATT_T + 1]))[0] == NUM_BUCKETS - 1
    n = b * s
    h = x.reshape(n, d)
    tab = rel_bias.T.astype(F32)
    tab_rel = (tab - tab[:, NUM_BUCKETS - 1:]) * LOG2E
    bias_causal = _build_bias_tiles(tab_rel, _sweep_bucket_tiles(ATT_T))

    w = _scaled_qkv(a_w_in[0], SCALE * LOG2E)
    z, zt = _norm_matmul(h, norm_gains[0, 0], w[:, :2 * d], w[:, 2 * d:].T)
    lam_init = 0.8 - 0.6 * math.exp(-0.3 * 0)
    o = _diff_attention(z.reshape(b, s, -1), zt, a_lambda[0].astype(F32), a_subln[0].astype(F32),
                        bias_causal, lam_init)
    h = _proj_residual(o.reshape(n, d), a_w_out[0].astype(BF16), h)
    h = _ffn(h, norm_gains[0, 1], ffn_w_gate[0].astype(BF16), ffn_w_up[0].astype(BF16),
             ffn_w_down[0].astype(BF16))

    dils = [dil for _, dil in B_PAIRS]
    views = _norm_matmul_views(h, norm_gains[1, 0], _scaled_qkv(b_w_in[0]), dils)
    outs, lses = [], []
    for zv, dil in zip(views, dils):
        bias = _build_bias_tiles(tab, _dilated_bucket_tiles(dil))
        o, lse = _dilated_group(zv.reshape(b, s // dil, -1), bias, dil)
        outs.append(o)
        lses.append(lse)
    h = _dil_combine_proj(outs, lses, dils, b_w_out[0].astype(BF16), h)
    h = _moe(h, norm_gains[1, 1], moe_router[0], moe_w_gate[0].astype(BF16), moe_w_up[0].astype(BF16),
             moe_w_down[0].astype(BF16), final_gain, False)

    w = _scaled_qkv(c_w_in[0])
    z, zt = _norm_matmul(h, norm_gains[2, 0], w[:, :2 * d], w[:, 2 * d:].T)
    o = _stick_attention(z.reshape(b, s, -1), zt)
    h = _proj_residual(o.reshape(n, d), c_w_out[0].astype(BF16), h)
    h = _ffn(h, norm_gains[2, 1], ffn_w_gate[1].astype(BF16), ffn_w_up[1].astype(BF16),
             ffn_w_down[1].astype(BF16))

    o = _nsa_mixer(h, norm_gains[3, 0], b, s, d_w_in[0], d_cmp_pos[0], d_cmp_w1[0], d_cmp_w2[0], tab_rel,
                   bias_causal)
    h = _proj_residual(o.reshape(n, d), d_w_out[0].astype(BF16), h)
    h = _moe(h, norm_gains[3, 1], moe_router[1], moe_w_gate[1].astype(BF16), moe_w_up[1].astype(BF16),
             moe_w_down[1].astype(BF16), final_gain, True)
    return h.reshape(b, s, d)
```

```python
import functools
import math

import numpy as np
import jax
import jax.numpy as jnp
from jax import lax
from jax.experimental import pallas as pl
from jax.experimental.pallas import tpu as pltpu

F32 = jnp.float32
BF16 = jnp.bfloat16

D_MODEL = 1024
HEAD_DIM = 64
N_HEADS = 16
LANES = 128
B_PAIRS = ((128, 1), (512, 4), (2048, 16))
DIL_W = 128
NSA_KV = 4
NSA_GROUP = 4
NSA_CMP_LEN = 32
NSA_CMP_STRIDE = 16
NSA_CMP_HID = 256
NSA_SLC_LEN = 64
NSA_TOP_N = 16
NSA_WINDOW = 512
NUM_BUCKETS = 32
MAX_EXACT = 16
REL_MAX_DIST = 2048
N_EXPERTS = 8
EPS = 1e-6
SCALE = HEAD_DIM ** -0.5
NEG = -1e30
MASKED_BUCKET = NUM_BUCKETS

ATT_T = 256
N_NEAR = 7
ATT_Q_TILES = 2
ATT_GROUP = 2
NSA_SWEEP_GROUP = 2
CMP_ROW_VARIANTS = 4
DEN_ROWS = 16
LOG2E = math.log2(math.e)
SB_T = 256
SB_Q_TILES = 2
SB_CUTOFF = -104.0
VMEM_LIMIT = 56 << 20


def _cparams(sem):
    return pltpu.CompilerParams(dimension_semantics=sem, vmem_limit_bytes=VMEM_LIMIT)


def _resident(block_shape, index_map):
    return pl.BlockSpec(block_shape, index_map, pipeline_mode=pl.Buffered(1))


def _nt_dot(a, b):
    return lax.dot_general(a, b, (((1,), (1,)), ((), ())), preferred_element_type=F32)


def _dot(a, b):
    return jnp.dot(a, b, preferred_element_type=F32)


def _split2(x):
    hi = x.astype(BF16)
    lo = (x - hi.astype(F32)).astype(BF16)
    return hi, lo


def _split3(x):
    hi = x.astype(BF16)
    rem = x - hi.astype(F32)
    mid = rem.astype(BF16)
    lo = (rem - mid.astype(F32)).astype(BF16)
    return hi, mid, lo


def _rms(x, g):
    ms = jnp.mean(x * x, axis=-1, keepdims=True)
    return x * lax.rsqrt(ms + EPS) * g


def _sigmoid(x):
    return 1.0 / (1.0 + jnp.exp(-x))


def _eye(n):
    r = lax.broadcasted_iota(jnp.int32, (n, n), 0)
    c = lax.broadcasted_iota(jnp.int32, (n, n), 1)
    return jnp.where(r == c, 1.0, 0.0).astype(BF16)


def _flip(eye, x):
    return _nt_dot(eye, x)


def _pad_rows(x, top):
    zero = jnp.zeros_like(x)
    return jnp.concatenate([x, zero] if top else [zero, x], axis=0)


def _t5_bucket_np(dist):
    n = np.maximum(dist, 0)
    ratio = np.log(np.maximum(n, 1).astype(np.float32) / np.float32(MAX_EXACT)) / np.float32(
        math.log(REL_MAX_DIST / MAX_EXACT))
    large = np.minimum(MAX_EXACT + (ratio * np.float32(NUM_BUCKETS - MAX_EXACT)).astype(np.int32),
                       NUM_BUCKETS - 1)
    return np.where(n < MAX_EXACT, n, large).astype(np.int32)


def _causal_bucket_tiles(t, n_tiles, max_dist=None):
    i = np.arange(t)[None, None, :]
    j = np.arange(t)[None, :, None]
    dist = np.arange(n_tiles)[:, None, None] * t + i - j
    ok = dist >= 0
    if max_dist is not None:
        ok = ok & (dist < max_dist)
    return np.where(ok, _t5_bucket_np(dist), MASKED_BUCKET).astype(np.int32)


def _sweep_bucket_tiles(t):
    far = np.full((1, t, t), NUM_BUCKETS - 1, np.int32)
    future = np.full((1, t, t), MASKED_BUCKET, np.int32)
    return np.concatenate([_causal_bucket_tiles(t, N_NEAR), far, future], axis=0)


def _bias_tile_index(delta):
    return jnp.where(delta < 0, N_NEAR + 1, jnp.minimum(delta, N_NEAR))


def _sweep_bias(bias_ref, col, first_q, nq, first_key, count):
    return jnp.concatenate(
        [jnp.concatenate([bias_ref[col, _bias_tile_index(first_q + a - (first_key + u))] for a in range(nq)], axis=1)
         for u in range(count)], axis=0)


def _dilated_bucket_tiles(dilation):
    w = DIL_W
    steps = w + np.arange(w)[None, :] - np.arange(2 * w)[:, None]
    ok = (steps >= 0) & (steps <= w)
    bkt = np.where(ok, _t5_bucket_np(dilation * steps), MASKED_BUCKET)
    first = np.where(np.arange(2 * w)[:, None] < w, MASKED_BUCKET, bkt)
    return np.stack([first, bkt]).astype(np.int32)


def _bias_build_kernel(tab_ref, bkt_ref, o_ref):
    c = pl.program_id(0)
    bkt = bkt_ref[0]
    out = jnp.full(bkt.shape, NEG, F32)
    for b in range(NUM_BUCKETS):
        out = jnp.where(bkt == b, tab_ref[c, b], out)
    o_ref[0, 0] = out


def _build_bias_tiles(tab, buckets):
    n, r, c = buckets.shape
    return pl.pallas_call(
        _bias_build_kernel,
        out_shape=jax.ShapeDtypeStruct((N_HEADS, n, r, c), F32),
        grid=(N_HEADS, n),
        in_specs=[pl.BlockSpec(memory_space=pltpu.SMEM),
                  pl.BlockSpec((1, r, c), lambda h, i: (i, 0, 0))],
        out_specs=pl.BlockSpec((1, 1, r, c), lambda h, i: (h, i, 0, 0)),
        compiler_params=_cparams(("arbitrary", "arbitrary")),
        name="bias_tiles",
    )(tab, jnp.asarray(buckets))


def _norm_matmul_kernel(x_ref, g_ref, w_ref, *refs, chunk, feature_major):
    xn = _rms(x_ref[...], g_ref[...]).astype(BF16)
    o_ref = refs[1] if feature_major else refs[0]
    nout = o_ref.shape[1]
    for c0 in range(0, nout, chunk):
        c1 = min(c0 + chunk, nout)
        o_ref[:, c0:c1] = _dot(xn, w_ref[:, c0:c1]).astype(o_ref.dtype)
    if feature_major:
        wt_ref, ot_ref = refs[0], refs[2]
        nt = ot_ref.shape[0]
        for c0 in range(0, nt, chunk):
            c1 = min(c0 + chunk, nt)
            ot_ref[c0:c1, :] = _nt_dot(wt_ref[c0:c1, :], xn).astype(ot_ref.dtype)


def _norm_matmul(x, g, w, wt=None, tm=512):
    n, d = x.shape
    nout = w.shape[1]
    in_specs = [pl.BlockSpec((tm, d), lambda i: (i, 0)),
                pl.BlockSpec((1, d), lambda i: (0, 0)),
                pl.BlockSpec((d, nout), lambda i: (0, 0))]
    out_shape = [jax.ShapeDtypeStruct((n, nout), BF16)]
    out_specs = [pl.BlockSpec((tm, nout), lambda i: (i, 0))]
    args = [x, g.reshape(1, d), w]
    if wt is not None:
        nt = wt.shape[0]
        in_specs.append(pl.BlockSpec((nt, d), lambda i: (0, 0)))
        out_shape.append(jax.ShapeDtypeStruct((nt, n), BF16))
        out_specs.append(pl.BlockSpec((nt, tm), lambda i: (0, i)))
        args.append(wt)
    out = pl.pallas_call(
        functools.partial(_norm_matmul_kernel, chunk=512, feature_major=wt is not None),
        out_shape=tuple(out_shape),
        grid=(n // tm,),
        in_specs=in_specs,
        out_specs=tuple(out_specs),
        compiler_params=_cparams(("parallel",)),
        name="norm_matmul",
    )(*args)
    return out if wt is not None else out[0]


def _norm_matmul_views_kernel(x_ref, g_ref, w_ref, *refs, chunk, dils):
    out_refs, z_ref = refs[:len(dils)], refs[len(dils)]
    xn = _rms(x_ref[...], g_ref[...]).astype(BF16)
    n_col, tm, _ = z_ref.shape
    nout = n_col * LANES
    for c0 in range(0, nout, chunk):
        c1 = min(c0 + chunk, nout)
        zc = _dot(xn, w_ref[:, c0:c1])
        for k in range((c1 - c0) // LANES):
            z_ref[c0 // LANES + k] = zc[:, k * LANES:(k + 1) * LANES]
    for o_ref, d in zip(out_refs, dils):
        for r in range(d):
            for c in range(n_col):
                col = r * nout + c * LANES
                o_ref[:, col:col + LANES] = z_ref[c, pl.ds(r, tm // d, stride=d), :].astype(o_ref.dtype)


def _norm_matmul_views(x, g, w, dils, tm=512):
    n, d_in = x.shape
    nout = w.shape[1]
    return pl.pallas_call(
        functools.partial(_norm_matmul_views_kernel, chunk=512, dils=tuple(dils)),
        out_shape=tuple(jax.ShapeDtypeStruct((n // d, d * nout), BF16) for d in dils),
        grid=(n // tm,),
        in_specs=[pl.BlockSpec((tm, d_in), lambda i: (i, 0)),
                  pl.BlockSpec((1, d_in), lambda i: (0, 0)),
                  pl.BlockSpec((d_in, nout), lambda i: (0, 0))],
        out_specs=tuple(pl.BlockSpec((tm // d, d * nout), lambda i: (i, 0)) for d in dils),
        scratch_shapes=[pltpu.VMEM((nout // LANES, tm, LANES), F32)],
        compiler_params=_cparams(("parallel",)),
        name="norm_matmul_views",
    )(x, g.reshape(1, d_in), w)


def _proj_res_kernel(a_ref, w_ref, h_ref, o_ref):
    o_ref[...] = h_ref[...] + _dot(a_ref[...], w_ref[...])


def _proj_residual(a, w, h, tm=512):
    n, d = h.shape
    k = a.shape[1]
    return pl.pallas_call(
        _proj_res_kernel,
        out_shape=jax.ShapeDtypeStruct((n, d), F32),
        grid=(n // tm,),
        in_specs=[pl.BlockSpec((tm, k), lambda i: (i, 0)),
                  pl.BlockSpec((k, d), lambda i: (0, 0)),
                  pl.BlockSpec((tm, d), lambda i: (i, 0))],
        out_specs=pl.BlockSpec((tm, d), lambda i: (i, 0)),
        compiler_params=_cparams(("parallel",)),
        name="proj_residual",
    )(a, w, h)


def _dil_combine_proj_kernel(o1, o2, o3, l1, l2, l3, w_ref, h_ref, o_ref, nat_ref, *, dils):
    tm = h_ref.shape[0]
    d_model = h_ref.shape[1]

    def in_position_order(view_ref, slot, d):
        if d == 1:
            return view_ref[...].astype(F32)
        for r in range(d):
            for c in range(d_model // LANES):
                col = r * d_model + c * LANES
                nat_ref[slot, c, pl.ds(r, tm // d, stride=d), :] = view_ref[:, col:col + LANES].astype(F32)
        return jnp.concatenate([nat_ref[slot, c] for c in range(d_model // LANES)], axis=1)

    outs = [in_position_order(ref, 2 * gi, d) for gi, (ref, d) in enumerate(zip((o1, o2, o3), dils))]
    ls = [in_position_order(ref, 2 * gi + 1, d) for gi, (ref, d) in enumerate(zip((l1, l2, l3), dils))]
    m = jnp.maximum(jnp.maximum(ls[0], ls[1]), ls[2])
    es = [jnp.exp(l - m) for l in ls]
    den = es[0] + es[1] + es[2]
    a = (es[0] * outs[0] + es[1] * outs[1] + es[2] * outs[2]) / den
    o_ref[...] = h_ref[...] + _dot(a.astype(BF16), w_ref[...])


def _dil_combine_proj(outs, lses, dils, w, h, tm=512):
    n, d = h.shape
    row = pl.BlockSpec((tm, d), lambda i: (i, 0))
    views = [pl.BlockSpec((tm // dil, dil * d), lambda i: (i, 0)) for dil in dils]
    return pl.pallas_call(
        functools.partial(_dil_combine_proj_kernel, dils=tuple(dils)),
        out_shape=jax.ShapeDtypeStruct((n, d), F32),
        grid=(n // tm,),
        in_specs=views + views + [pl.BlockSpec((d, d), lambda i: (0, 0)), row],
        out_specs=row,
        scratch_shapes=[pltpu.VMEM((2 * len(dils), d // LANES, tm, LANES), F32)],
        compiler_params=_cparams(("parallel",)),
        name="dilated_combine_proj",
    )(*outs, *lses, w, h)


def _ffn_kernel(h_ref, g_ref, wg_ref, wu_ref, wd_ref, o_ref, xn_ref, acc_ref):
    f = pl.program_id(1)

    @pl.when(f == 0)
    def _():
        xn_ref[...] = _rms(h_ref[...], g_ref[...]).astype(BF16)
        acc_ref[...] = jnp.zeros_like(acc_ref)

    xn = xn_ref[...]
    gate = _dot(xn, wg_ref[...])
    up = _dot(xn, wu_ref[...])
    act = (gate * _sigmoid(gate) * up).astype(BF16)
    acc_ref[...] += _dot(act, wd_ref[...])

    @pl.when(f == pl.num_programs(1) - 1)
    def _():
        o_ref[...] = h_ref[...] + acc_ref[...]


def _ffn(h, g, wg, wu, wd, tm=1024, n_f=2):
    n, d = h.shape
    dff = wg.shape[1]
    tf = dff // n_f
    return pl.pallas_call(
        _ffn_kernel,
        out_shape=jax.ShapeDtypeStruct((n, d), F32),
        grid=(n // tm, n_f),
        in_specs=[pl.BlockSpec((tm, d), lambda i, f: (i, 0)),
                  pl.BlockSpec((1, d), lambda i, f: (0, 0)),
                  pl.BlockSpec((d, tf), lambda i, f: (0, f)),
                  pl.BlockSpec((d, tf), lambda i, f: (0, f)),
                  pl.BlockSpec((tf, d), lambda i, f: (f, 0))],
        out_specs=pl.BlockSpec((tm, d), lambda i, f: (i, 0)),
        scratch_shapes=[pltpu.VMEM((tm, d), BF16), pltpu.VMEM((tm, d), F32)],
        compiler_params=_cparams(("parallel", "arbitrary")),
        name="ffn",
    )(h, g.reshape(1, d), wg, wu, wd)


def _moe_kernel(h_ref, g_ref, rh_ref, rl_ref, wg_ref, wu_ref, wd_ref, fg_ref, o_ref,
                xn_ref, comb_ref, acc_ref, *, final_norm):
    e = pl.program_id(1)
    col = lax.broadcasted_iota(jnp.int32, comb_ref.shape, 1)

    @pl.when(e == 0)
    def _():
        xn = _rms(h_ref[...], g_ref[...])
        xh, xl = _split2(xn)
        xn_ref[...] = xh
        logits = _dot(xh, rh_ref[...]) + _dot(xh, rl_ref[...]) + _dot(xl, rh_ref[...])
        colf = col.astype(F32)
        lg = jnp.where(col < N_EXPERTS, logits, NEG)
        m1 = jnp.max(lg, axis=-1, keepdims=True)
        i1 = jnp.min(jnp.where(lg == m1, colf, float(LANES)), axis=-1, keepdims=True)
        lg2 = jnp.where(colf == i1, NEG, lg)
        m2 = jnp.max(lg2, axis=-1, keepdims=True)
        i2 = jnp.min(jnp.where(lg2 == m2, colf, float(LANES)), axis=-1, keepdims=True)
        e2 = jnp.exp(m2 - m1)
        g1 = 1.0 / (1.0 + e2)
        g2 = e2 / (1.0 + e2)
        comb_ref[...] = jnp.where(colf == i1, g1, 0.0) + jnp.where(colf == i2, g2, 0.0)
        acc_ref[...] = jnp.zeros_like(acc_ref)

    xn = xn_ref[...]
    gate = _dot(xn, wg_ref[0])
    up = _dot(xn, wu_ref[0])
    ce = jnp.sum(jnp.where(col == e, comb_ref[...], 0.0), axis=-1, keepdims=True)
    act = (gate * _sigmoid(gate) * up * ce).astype(BF16)
    acc_ref[...] += _dot(act, wd_ref[0])

    @pl.when(e == pl.num_programs(1) - 1)
    def _():
        y = h_ref[...] + acc_ref[...]
        if final_norm:
            y = _rms(y, fg_ref[...])
        o_ref[...] = y


def _moe(h, g, router, wg, wu, wd, final_gain, final_norm, tm=1024):
    n, d = h.shape
    ne, _, dfe = wg.shape
    rpad = jnp.zeros((d, LANES), F32).at[:, :ne].set(router)
    rh, rl = _split2(rpad)
    return pl.pallas_call(
        functools.partial(_moe_kernel, final_norm=final_norm),
        out_shape=jax.ShapeDtypeStruct((n, d), F32),
        grid=(n // tm, ne),
        in_specs=[pl.BlockSpec((tm, d), lambda i, e: (i, 0)),
                  pl.BlockSpec((1, d), lambda i, e: (0, 0)),
                  pl.BlockSpec((d, LANES), lambda i, e: (0, 0)),
                  pl.BlockSpec((d, LANES), lambda i, e: (0, 0)),
                  pl.BlockSpec((1, d, dfe), lambda i, e: (e, 0, 0)),
                  pl.BlockSpec((1, d, dfe), lambda i, e: (e, 0, 0)),
                  pl.BlockSpec((1, dfe, d), lambda i, e: (e, 0, 0)),
                  pl.BlockSpec((1, d), lambda i, e: (0, 0))],
        out_specs=pl.BlockSpec((tm, d), lambda i, e: (i, 0)),
        scratch_shapes=[pltpu.VMEM((tm, d), BF16), pltpu.VMEM((tm, LANES), F32),
                        pltpu.VMEM((tm, d), F32)],
        compiler_params=_cparams(("parallel", "arbitrary")),
        name="moe",
    )(h, g.reshape(1, d), rh, rl, wg, wu, wd, final_gain.reshape(1, d))


def _with_ones(vt):
    return jnp.concatenate([vt, jnp.ones((DEN_ROWS, vt.shape[1]), vt.dtype)], axis=0)


def _flash_step(s, vt1, m_ref, acc_ref, idx):
    m_old = m_ref[idx]
    m_new = jnp.maximum(m_old, jnp.max(s, axis=0, keepdims=True))
    p = jnp.exp2(s - m_new).astype(BF16)
    acc_ref[idx] = jnp.exp2(m_old - m_new) * acc_ref[idx] + _dot(vt1, p)
    m_ref[idx] = m_new


def _flash_result(acc_ref, idx):
    acc = acc_ref[idx]
    nf = acc.shape[0] - DEN_ROWS
    return acc[:nf] / acc[nf:nf + 1]


def _far_groups(qi, group):
    return lax.shift_right_logical(jnp.maximum(qi + 1 - N_NEAR, 0), int(math.log2(group)))


def _pipelined_sweep(n_far, n_groups, scores, softmax, sa_ref, sb_ref):
    scores(0, sa_ref)

    def pairs(near):
        def pair(k, carry):
            scores(2 * k + 1, sb_ref)
            softmax(2 * k, sa_ref, near)
            scores(2 * k + 2, sa_ref)
            softmax(2 * k + 1, sb_ref, near)
            return carry
        return pair

    far_pairs = lax.shift_right_logical(n_far, 1)
    full_pairs = lax.shift_right_logical(n_groups, 1)
    lax.fori_loop(0, far_pairs, pairs(False), 0)
    lax.fori_loop(far_pairs, full_pairs, pairs(True), 0)

    @pl.when((n_groups & 1) == 1)
    def _():
        softmax(n_groups - 1, sa_ref, True)


def _flash_init(m_ref, acc_ref):
    m_ref[...] = jnp.full(m_ref.shape, NEG, F32)
    acc_ref[...] = jnp.zeros_like(acc_ref)


def _diff_attn_kernel(lam_ref, q_ref, k_ref, vt_ref, bias_ref, g_ref, o_ref, m_ref, acc_ref, sa_ref, sb_ref,
                      *, t, nq, lam_init):
    qi = pl.program_id(2)
    tq = nq * t
    qt = _flip(_eye(LANES), q_ref[...]).astype(BF16)
    rhs = jnp.concatenate([_pad_rows(qt[:HEAD_DIM], True), _pad_rows(qt[HEAD_DIM:], False)], axis=1)
    _flash_init(m_ref, acc_ref)
    gt = ATT_GROUP * t
    first_q = qi * nq
    n_groups = lax.shift_right_logical(first_q + nq - 1 + ATT_GROUP, int(math.log2(ATT_GROUP)))

    def key_start(gi):
        return pl.multiple_of(jnp.minimum(gi, n_groups - 1) * gt, gt)

    def scores(gi, s_ref):
        s_ref[...] = _dot(k_ref[pl.ds(key_start(gi), gt), :], rhs)

    def softmax(gi, s_ref, near):
        vt1 = _with_ones(vt_ref[:, pl.ds(key_start(gi), gt)])
        for mi in range(2):
            s = s_ref[:, mi * tq:(mi + 1) * tq]
            if near:
                s = s + _sweep_bias(bias_ref, mi, first_q, nq, gi * ATT_GROUP, ATT_GROUP)
            _flash_step(s, vt1, m_ref, acc_ref, mi)

    _pipelined_sweep(_far_groups(first_q, ATT_GROUP), n_groups, scores, softmax, sa_ref, sb_ref)

    lam = lam_ref[...]
    lam_full = (jnp.exp(jnp.sum(lam[0:1] * lam[1:2], axis=-1, keepdims=True))
                - jnp.exp(jnp.sum(lam[2:3] * lam[3:4], axis=-1, keepdims=True)) + lam_init)
    o = _flash_result(acc_ref, 0) - lam_full * _flash_result(acc_ref, 1)
    ms = jnp.mean(o * o, axis=0, keepdims=True)
    y = (o * lax.rsqrt(ms + EPS) * g_ref[...] * (1.0 - lam_init)).astype(BF16)
    o_ref[...] = _flip(_eye(tq), y).astype(o_ref.dtype)


def _diff_attention(z, zt, lam, subln, bias_tiles, lam_init):
    b, s, _ = z.shape
    t = ATT_T
    tq = ATT_Q_TILES * t
    nh = N_HEADS // 2
    n_tiles = bias_tiles.shape[1]
    return pl.pallas_call(
        functools.partial(_diff_attn_kernel, t=t, nq=ATT_Q_TILES, lam_init=lam_init),
        out_shape=jax.ShapeDtypeStruct((b, s, D_MODEL), BF16),
        grid=(b, nh, s // tq),
        in_specs=[pl.BlockSpec((4, HEAD_DIM), lambda bi, h, qi: (0, 0)),
                  pl.BlockSpec((None, tq, LANES), lambda bi, h, qi: (bi, qi, h)),
                  _resident((None, s, LANES), lambda bi, h, qi: (bi, 0, nh + h)),
                  _resident((LANES, s), lambda bi, h, qi: (h, bi)),
                  _resident((2, n_tiles, t, t), lambda bi, h, qi: (h, 0, 0, 0)),
                  pl.BlockSpec((LANES, 1), lambda bi, h, qi: (0, 0))],
        out_specs=pl.BlockSpec((None, tq, LANES), lambda bi, h, qi: (bi, qi, h)),
        scratch_shapes=[pltpu.VMEM((2, 1, tq), F32), pltpu.VMEM((2, LANES + DEN_ROWS, tq), F32),
                        pltpu.VMEM((ATT_GROUP * t, 2 * tq), F32), pltpu.VMEM((ATT_GROUP * t, 2 * tq), F32)],
        compiler_params=_cparams(("parallel", "parallel", "arbitrary")),
        name="diff_attention",
    )(lam, z, z, zt, bias_tiles, subln.reshape(LANES, 1))


def _dilated_kernel(q_ref, kp_ref, kc_ref, vp_ref, vc_ref, bias_ref, o_ref, lse_ref):
    n = pl.program_id(2)
    variant = jnp.minimum(n, 1)
    w = q_ref.shape[0]
    eye = _eye(LANES)
    eye_w = _eye(w)
    pairs = range(N_HEADS // 2)
    cols = [slice(hp * LANES, (hp + 1) * LANES) for hp in pairs]
    qts = [_flip(eye, q_ref[:, c]).astype(BF16) for c in cols]
    kcats = [jnp.concatenate([kp_ref[:, c], kc_ref[:, c]], axis=0) for c in cols]
    vts = [_flip(eye, jnp.concatenate([vp_ref[:, c], vc_ref[:, c]], axis=0)).astype(BF16) for c in cols]
    s2s = [_dot(kcats[hp], jnp.concatenate([_pad_rows(qts[hp][:HEAD_DIM], True),
                                            _pad_rows(qts[hp][HEAD_DIM:], False)], axis=1))
           for hp in pairs]
    ps, lses = [], []
    for hp in pairs:
        for sub in range(2):
            s = s2s[hp][:, sub * w:(sub + 1) * w] + bias_ref[2 * hp + sub, variant]
            m = jnp.max(s, axis=0, keepdims=True)
            e = jnp.exp(s - m)
            den = jnp.sum(e, axis=0, keepdims=True)
            ps.append((e * (1.0 / den)).astype(BF16))
            lses.append(jnp.broadcast_to(m + jnp.log(den), (HEAD_DIM, w)))
    outs = [_dot(vts[hp][sub * HEAD_DIM:(sub + 1) * HEAD_DIM], ps[2 * hp + sub])
            for hp in pairs for sub in range(2)]
    for hp in pairs:
        pair_out = jnp.concatenate(outs[2 * hp:2 * hp + 2], axis=0).astype(BF16)
        o_ref[:, cols[hp]] = _flip(eye_w, pair_out).astype(o_ref.dtype)
        pair_lse = jnp.concatenate(lses[2 * hp:2 * hp + 2], axis=0)
        lse_ref[:, cols[hp]] = sum(_flip(eye_w, term) for term in _split3(pair_lse))


def _dilated_group(zv, bias, dilation):
    b, m_len, _ = zv.shape
    w = DIL_W
    nb = m_len // w
    blk = lambda sect, prev: pl.BlockSpec(
        (None, w, D_MODEL),
        (lambda bi, r, n: (bi, jnp.maximum(n - 1, 0), 3 * r + sect)) if prev
        else (lambda bi, r, n: (bi, n, 3 * r + sect)))
    out_blk = pl.BlockSpec((None, w, D_MODEL), lambda bi, r, n: (bi, n, r))
    o, lse = pl.pallas_call(
        _dilated_kernel,
        out_shape=(jax.ShapeDtypeStruct((b, m_len, dilation * D_MODEL), BF16),
                   jax.ShapeDtypeStruct((b, m_len, dilation * D_MODEL), F32)),
        grid=(b, dilation, nb),
        in_specs=[blk(0, False), blk(1, True), blk(1, False), blk(2, True), blk(2, False),
                  pl.BlockSpec((N_HEADS, 2, 2 * w, w), lambda bi, r, n: (0, 0, 0, 0))],
        out_specs=(out_blk, out_blk),
        compiler_params=_cparams(("parallel", "parallel", "arbitrary")),
        name=f"dilated_d{dilation}",
    )(zv, zv, zv, zv, zv, bias)
    return o.reshape(b * m_len, dilation * D_MODEL), lse.reshape(b * m_len, dilation * D_MODEL)


def _stick_kernel(q_ref, k_ref, vt_ref, o_ref, acc_ref, carry_ref, *, t, nq):
    qi = pl.program_id(2)
    tq = nq * t
    first_q = qi * nq
    qt = _flip(_eye(LANES), q_ref[...]).astype(BF16)
    rhs = jnp.concatenate([_pad_rows(qt[:HEAD_DIM], True), _pad_rows(qt[HEAD_DIM:], False)], axis=1)
    row = lax.broadcasted_iota(jnp.int32, (t, t), 0)
    col = lax.broadcasted_iota(jnp.int32, (t, t), 1)
    after = jnp.where(col > row, 1.0, 0.0).astype(BF16)
    key_off = lax.broadcasted_iota(jnp.int32, (t, 2 * tq), 0)
    q_off = lax.broadcasted_iota(jnp.int32, (t, 2 * tq), 1) % tq
    acc_ref[...] = jnp.zeros_like(acc_ref)
    carry_ref[...] = jnp.zeros_like(carry_ref)

    def block(j, masked):
        start = pl.multiple_of(j * t, t)
        z = _dot(k_ref[pl.ds(start, t), :], rhs)
        log_keep = -(jnp.maximum(z, 0.0) + jnp.log(1.0 + jnp.exp(-jnp.abs(z))))
        if masked:
            before = key_off + (j - first_q) * t < q_off
            log_keep = jnp.where(before, log_keep, 0.0)
        hi, lo = _split2(log_keep)
        later = _dot(after, hi) + _dot(after, lo) + carry_ref[...]
        a = jnp.exp(z + log_keep + later)
        if masked:
            a = jnp.where(before, a, 0.0)
        a = a.astype(BF16)
        for sub in range(2):
            vt = vt_ref[sub * HEAD_DIM:(sub + 1) * HEAD_DIM, pl.ds(start, t)]
            acc_ref[sub] += _dot(vt, a[:, sub * tq:(sub + 1) * tq])
        carry_ref[...] += jnp.sum(log_keep, axis=0, keepdims=True)

    for a_tile in reversed(range(nq)):
        block(first_q + a_tile, True)

    def cond(c):
        return jnp.logical_and(c[0] >= 0, c[1] >= SB_CUTOFF)

    def body(c):
        block(c[0], False)
        return c[0] - 1, jnp.max(carry_ref[...])

    lax.while_loop(cond, body, (first_q - 1, jnp.max(carry_ref[...])))
    out = jnp.concatenate([acc_ref[0], acc_ref[1]], axis=0).astype(BF16)
    o_ref[...] = _flip(_eye(tq), out).astype(o_ref.dtype)


def _stick_attention(z, zt):
    b, s, _ = z.shape
    t = SB_T
    tq = SB_Q_TILES * t
    nh = N_HEADS // 2
    return pl.pallas_call(
        functools.partial(_stick_kernel, t=t, nq=SB_Q_TILES),
        out_shape=jax.ShapeDtypeStruct((b, s, D_MODEL), BF16),
        grid=(b, nh, s // tq),
        in_specs=[pl.BlockSpec((None, tq, LANES), lambda bi, h, qi: (bi, qi, h)),
                  _resident((None, s, LANES), lambda bi, h, qi: (bi, 0, nh + h)),
                  _resident((LANES, s), lambda bi, h, qi: (h, bi))],
        out_specs=pl.BlockSpec((None, tq, LANES), lambda bi, h, qi: (bi, qi, h)),
        scratch_shapes=[pltpu.VMEM((2, HEAD_DIM, tq), F32), pltpu.VMEM((1, 2 * tq), F32)],
        compiler_params=_cparams(("parallel", "parallel", "arbitrary")),
        name="stick_breaking",
    )(z, z, zt)


def _compress_kernel(a_ref, as_ref, w1_ref, pos_ref, w2_ref, w2t_ref, o_ref, ot_ref):
    half = a_ref.shape[1]
    w1 = w1_ref[...]
    const = _dot(jnp.broadcast_to(pos_ref[...], (8, 2 * half)), w1)[0:1]
    pre = _dot(a_ref[...], w1[:half]) + _dot(as_ref[...], w1[half:]) + const
    hid = 0.5 * pre * (1.0 + jnp.tanh(math.sqrt(2.0 / math.pi) * (pre + 0.044715 * pre * pre * pre)))
    hid = hid.astype(BF16)
    o_ref[...] = _dot(hid, w2_ref[...]).astype(o_ref.dtype)
    ot_ref[...] = _nt_dot(w2t_ref[...], hid).astype(ot_ref.dtype)


def _compress(blocks, blocks_next, w1, pos, w2, tr=256):
    _, nkv, b, r, half = blocks.shape
    tr = min(tr, r)
    w2pad = jnp.concatenate([w2, jnp.zeros_like(w2)], axis=-1)
    w2t = jnp.swapaxes(w2, 1, 2)
    spec = pl.BlockSpec((None, None, None, tr, half), lambda j, hk, bi, ri: (j, hk, bi, ri, 0))
    return pl.pallas_call(
        _compress_kernel,
        out_shape=(jax.ShapeDtypeStruct((2, nkv, b, r, LANES), BF16),
                   jax.ShapeDtypeStruct((2, nkv, b, HEAD_DIM, r), BF16)),
        grid=(2, nkv, b, r // tr),
        in_specs=[spec, spec,
                  pl.BlockSpec((None, 2 * half, NSA_CMP_HID), lambda j, hk, bi, ri: (j, 0, 0)),
                  pl.BlockSpec((None, 1, 2 * half), lambda j, hk, bi, ri: (j, 0, 0)),
                  pl.BlockSpec((None, NSA_CMP_HID, LANES), lambda j, hk, bi, ri: (j, 0, 0)),
                  pl.BlockSpec((None, HEAD_DIM, NSA_CMP_HID), lambda j, hk, bi, ri: (j, 0, 0))],
        out_specs=(pl.BlockSpec((None, None, None, tr, LANES), lambda j, hk, bi, ri: (j, hk, bi, ri, 0)),
                   pl.BlockSpec((None, None, None, HEAD_DIM, tr), lambda j, hk, bi, ri: (j, hk, bi, 0, ri))),
        compiler_params=_cparams(("parallel", "parallel", "parallel", "arbitrary")),
        name="nsa_compress",
    )(blocks, blocks_next, w1, pos, w2pad, w2t)


def _nsa_select_kernel(q_ref, kc_ref, vct_ref, cmat_ref, o_ref, sel_ref, *, t, n_sel):
    qi = pl.program_id(2)
    r = kc_ref.shape[0]
    n_slc = sel_ref.shape[0]
    qt = _flip(_eye(NSA_GROUP * HEAD_DIM), q_ref[...]).astype(BF16)
    tpos = qi * t + lax.broadcasted_iota(jnp.int32, (1, t), 1)
    live = (tpos >= NSA_CMP_LEN - 1).astype(F32)

    def compressed_branch(rows):
        kc = kc_ref[:rows, :]
        vct = vct_ref[:, :rows]
        cidx = lax.broadcasted_iota(jnp.int32, (rows, t), 0)
        seen = NSA_CMP_STRIDE * cidx + (NSA_CMP_LEN - 1) <= tpos
        imp = jnp.zeros((rows, t), F32)
        for g in range(NSA_GROUP):
            rhs = _pad_rows(qt[g * HEAD_DIM:(g + 1) * HEAD_DIM], True)
            sc = jnp.where(seen, _dot(kc, rhs), NEG)
            e = jnp.exp2(sc - jnp.max(sc, axis=0, keepdims=True))
            pc = e * (live / jnp.sum(e, axis=0, keepdims=True))
            o_ref[g * HEAD_DIM:(g + 1) * HEAD_DIM, :] = _dot(vct, pc.astype(BF16)).astype(o_ref.dtype)
            imp = imp + pc
        n_blk = rows // (NSA_SLC_LEN // NSA_CMP_STRIDE)
        cmat = cmat_ref[:n_blk, :rows]
        imp_slc = sum(_dot(cmat, term) for term in _split3(imp))
        blk = lax.broadcasted_iota(jnp.int32, (n_blk, t), 0)
        cur = tpos // NSA_SLC_LEN
        forced = (blk == 0) | (blk == cur) | (blk == cur - 1)
        valid = blk * NSA_SLC_LEN <= tpos
        ranked = jnp.logical_and(valid, jnp.logical_not(forced))
        score = jnp.where(ranked, imp_slc, -1.0)
        blkf = blk.astype(F32)
        for _ in range(n_sel - 3):
            top = jnp.max(score, axis=0, keepdims=True)
            first = jnp.min(jnp.where(score == top, blkf, float(n_blk)), axis=0, keepdims=True)
            score = jnp.where(blkf == first, -jnp.inf, score)
        chosen = jnp.logical_or(jnp.logical_and(valid, forced), jnp.logical_and(ranked, score == -jnp.inf))
        sel_ref[:n_blk, :] = jnp.where(chosen, 1.0, 0.0).astype(sel_ref.dtype)
        if n_blk < n_slc:
            sel_ref[n_blk:, :] = jnp.zeros((n_slc - n_blk, t), sel_ref.dtype)

    quarter = r // CMP_ROW_VARIANTS
    need = (qi * t + t - NSA_CMP_LEN) // NSA_CMP_STRIDE
    variant = jnp.clip(need // quarter, 0, CMP_ROW_VARIANTS - 1)
    for v in range(CMP_ROW_VARIANTS):
        @pl.when(variant == v)
        def _(v=v):
            compressed_branch((v + 1) * quarter)


def _nsa_select(z, kc, vct, cmat, n_sel):
    b, s, _ = z.shape
    t = ATT_T
    r = kc.shape[2]
    n_slc = cmat.shape[0]
    gw = NSA_GROUP * HEAD_DIM
    assert r % CMP_ROW_VARIANTS == 0 and (r // CMP_ROW_VARIANTS) % 16 == 0 and t % 16 == 0
    return pl.pallas_call(
        functools.partial(_nsa_select_kernel, t=t, n_sel=n_sel),
        out_shape=(jax.ShapeDtypeStruct((b, NSA_KV, gw, s), BF16),
                   jax.ShapeDtypeStruct((b, NSA_KV, n_slc, s), BF16)),
        grid=(b, NSA_KV, s // t),
        in_specs=[pl.BlockSpec((None, t, gw), lambda bi, hk, qi: (bi, qi, hk)),
                  pl.BlockSpec((None, None, r, LANES), lambda bi, hk, qi: (hk, bi, 0, 0)),
                  pl.BlockSpec((None, None, HEAD_DIM, r), lambda bi, hk, qi: (hk, bi, 0, 0)),
                  pl.BlockSpec((n_slc, r), lambda bi, hk, qi: (0, 0))],
        out_specs=(pl.BlockSpec((None, None, gw, t), lambda bi, hk, qi: (bi, hk, 0, qi)),
                   pl.BlockSpec((None, None, n_slc, t), lambda bi, hk, qi: (bi, hk, 0, qi))),
        compiler_params=_cparams(("parallel", "parallel", "arbitrary")),
        name="nsa_select",
    )(z, kc, vct, cmat)


def _nsa_attend_kernel(q_ref, ks_ref, kw_ref, vst_ref, vwt_ref, sel_ref, ocmp_ref, zg_ref, bs_ref, bw_ref,
                       o_ref, m_ref, acc_ref, res_ref, gate_ref, sel32_ref, sa_ref, sb_ref, *, t, nq, n_win):
    hk = pl.program_id(1)
    qi = pl.program_id(2)
    tq = nq * t
    first_q = qi * nq
    per_tile = t // NSA_SLC_LEN
    gw = NSA_GROUP * HEAD_DIM
    qt = _flip(_eye(gw), q_ref[...]).astype(BF16)
    low_half = (hk % 2) == 0
    q_heads = [qt[g * HEAD_DIM:(g + 1) * HEAD_DIM] for g in range(NSA_GROUP)]
    sel32_ref[...] = sel_ref[...].astype(F32)

    _flash_init(m_ref, acc_ref)
    grp = NSA_SWEEP_GROUP
    gt = grp * t
    n_groups = lax.shift_right_logical(first_q + nq - 1 + grp, int(math.log2(grp)))

    def first_tile(gi):
        return jnp.minimum(gi, n_groups - 1) * grp

    n_blk = grp * per_tile
    lane = lax.broadcasted_iota(jnp.int32, (gt, LANES), 1)
    blk_of_key = lax.broadcasted_iota(jnp.int32, (gt, LANES), 0) // NSA_SLC_LEN
    key_lanes = lax.shift_right_logical(lane, int(math.log2(HEAD_DIM))) == hk % 2
    one_hot = jnp.where(lane == blk_of_key + jnp.where(low_half, HEAD_DIM, 0), 1.0, 0.0).astype(BF16)

    def with_extra_rows(extra):
        return jnp.concatenate(
            [jnp.where(low_half, jnp.concatenate([qg, extra], axis=0), jnp.concatenate([extra, qg], axis=0))
             for qg in q_heads], axis=1)

    def scores(gi, s_ref):
        j0 = first_tile(gi)
        keys = jnp.where(key_lanes, ks_ref[pl.ds(pl.multiple_of(j0 * t, t), gt), :], one_hot)
        drop = (sel32_ref[pl.ds(pl.multiple_of(j0 * per_tile, n_blk), n_blk), :] - 1.0) * (-NEG)
        drop = jnp.concatenate([drop, jnp.zeros((HEAD_DIM - n_blk, tq), F32)], axis=0).astype(BF16)
        s_ref[...] = _dot(keys, with_extra_rows(drop))

    def softmax(gi, s_ref, near):
        j0 = first_tile(gi)
        vt1 = _with_ones(vst_ref[:, pl.ds(pl.multiple_of(j0 * t, t), gt)])
        for g in range(NSA_GROUP):
            s = s_ref[:, g * tq:(g + 1) * tq]
            if near:
                s = s + _sweep_bias(bs_ref, g, first_q, nq, gi * grp, grp)
            _flash_step(s, vt1, m_ref, acc_ref, g)

    _pipelined_sweep(_far_groups(first_q, grp), n_groups, scores, softmax, sa_ref, sb_ref)
    for g in range(NSA_GROUP):
        res_ref[g] = _flash_result(acc_ref, g)

    _flash_init(m_ref, acc_ref)
    rhs_win = with_extra_rows(jnp.zeros((HEAD_DIM, tq), BF16))
    w_first = first_q - (n_win - 1)
    for wg in range(-(-(nq + n_win - 1) // grp)):
        j0 = w_first + wg * grp
        start = pl.multiple_of(jnp.maximum(j0, 0) * t, t)
        sa_ref[...] = _dot(kw_ref[pl.ds(start, gt), :], rhs_win)
        vt1 = _with_ones(vwt_ref[:, pl.ds(start, gt)])
        for g in range(NSA_GROUP):
            rows = []
            for u in range(grp):
                tiles = []
                for a in range(nq):
                    delta = first_q + a - (j0 + u)
                    inside = jnp.logical_and(jnp.logical_and(delta >= 0, delta < n_win), j0 + u >= 0)
                    tiles.append(bw_ref[g, jnp.where(inside, delta, n_win)])
                rows.append(jnp.concatenate(tiles, axis=1))
            _flash_step(sa_ref[:, g * tq:(g + 1) * tq] + jnp.concatenate(rows, axis=0), vt1, m_ref, acc_ref, g)

    gate_ref[...] = _sigmoid(_flip(_eye(LANES), zg_ref[...]))
    mixed = []
    for g in range(NSA_GROUP):
        base = 3 * (NSA_GROUP * hk + g)
        gates = [gate_ref[pl.ds(base + c, 1), :] for c in range(3)]
        o_cmp = ocmp_ref[g * HEAD_DIM:(g + 1) * HEAD_DIM, :].astype(F32)
        mixed.append(gates[0] * o_cmp + gates[1] * res_ref[g] + gates[2] * _flash_result(acc_ref, g))
    o_ref[...] = _flip(_eye(tq), jnp.concatenate(mixed, axis=0).astype(BF16)).astype(o_ref.dtype)


def _nsa_attend(z, zt, sel, o_cmp, bias_slc, bias_win, cols):
    b, s, _ = z.shape
    t = ATT_T
    tq = ATT_Q_TILES * t
    n_slc = sel.shape[2]
    gw = NSA_GROUP * HEAD_DIM
    n_win = bias_win.shape[1] - 1
    assert (n_win - 1) % NSA_SWEEP_GROUP == 0 and ATT_Q_TILES % NSA_SWEEP_GROUP == 0
    n_tiles = bias_slc.shape[1]
    c_slc, c_win, c_gate = cols["k_slc"], cols["k_win"], cols["gate"]
    return pl.pallas_call(
        functools.partial(_nsa_attend_kernel, t=t, nq=ATT_Q_TILES, n_win=n_win),
        out_shape=jax.ShapeDtypeStruct((b, s, D_MODEL), BF16),
        grid=(b, NSA_KV, s // tq),
        in_specs=[pl.BlockSpec((None, tq, gw), lambda bi, hk, qi: (bi, qi, hk)),
                  _resident((None, s, LANES), lambda bi, hk, qi: (bi, 0, c_slc + hk // 2)),
                  _resident((None, s, LANES), lambda bi, hk, qi: (bi, 0, c_win + hk // 2)),
                  _resident((HEAD_DIM, s), lambda bi, hk, qi: (hk, bi)),
                  _resident((HEAD_DIM, s), lambda bi, hk, qi: (NSA_KV + hk, bi)),
                  pl.BlockSpec((None, None, n_slc, tq), lambda bi, hk, qi: (bi, hk, 0, qi)),
                  pl.BlockSpec((None, None, gw, tq), lambda bi, hk, qi: (bi, hk, 0, qi)),
                  pl.BlockSpec((None, tq, LANES), lambda bi, hk, qi: (bi, qi, c_gate)),
                  _resident((NSA_GROUP, n_tiles, t, t), lambda bi, hk, qi: (hk, 0, 0, 0)),
                  _resident((NSA_GROUP, n_win + 1, t, t), lambda bi, hk, qi: (hk, 0, 0, 0))],
        out_specs=pl.BlockSpec((None, tq, gw), lambda bi, hk, qi: (bi, qi, hk)),
        scratch_shapes=[pltpu.VMEM((NSA_GROUP, 1, tq), F32),
                        pltpu.VMEM((NSA_GROUP, HEAD_DIM + DEN_ROWS, tq), F32),
                        pltpu.VMEM((NSA_GROUP, HEAD_DIM, tq), F32),
                        pltpu.VMEM((LANES, tq), F32), pltpu.VMEM((n_slc, tq), F32),
                        pltpu.VMEM((NSA_SWEEP_GROUP * t, NSA_GROUP * tq), F32),
                        pltpu.VMEM((NSA_SWEEP_GROUP * t, NSA_GROUP * tq), F32)],
        compiler_params=_cparams(("parallel", "parallel", "arbitrary")),
        name="nsa_attend",
    )(z, z, z, zt, zt, sel, o_cmp, z, bias_slc, bias_win)


def _nsa_layout(w_in):
    d = w_in.shape[0]
    nq = N_HEADS * HEAD_DIM
    kvw = NSA_KV * HEAD_DIM
    kv = [w_in[:, nq + j * kvw: nq + (j + 1) * kvw] for j in range(6)]
    wg = w_in[:, nq + 6 * kvw:]
    wg = jnp.concatenate([wg, jnp.zeros((d, LANES - wg.shape[1]), w_in.dtype)], axis=-1)
    w = jnp.concatenate([w_in[:, :nq] * (SCALE * LOG2E), kv[2], kv[4], kv[0], kv[1], wg], axis=-1)
    wt = jnp.concatenate([kv[3], kv[5]], axis=-1).T
    cols = {"k_slc": nq // LANES, "k_win": (nq + kvw) // LANES, "k_cmp": nq + 2 * kvw,
            "v_cmp": nq + 3 * kvw, "gate": (nq + 4 * kvw) // LANES}
    return w.astype(BF16), wt.astype(BF16), cols


def _importance_matrix_np(n_slc, r):
    ratio = NSA_SLC_LEN // NSA_CMP_STRIDE
    span = NSA_CMP_LEN // NSA_CMP_STRIDE
    coef = np.convolve(np.ones(ratio), np.ones(span))
    cmat = np.zeros((n_slc, r), np.float32)
    for j in range(n_slc):
        for o, cf in enumerate(coef):
            c = ratio * j - o
            if c >= 0:
                cmat[j, c] = cf
    return cmat


def _nsa_mixer(h, g, b, s, w_in, cmp_pos, cmp_w1, cmp_w2, tab, bias_causal):
    w, wt, cols = _nsa_layout(w_in)
    z, zt = _norm_matmul(h, g, w, wt)
    z = z.reshape(b, s, -1)
    kvw = NSA_KV * HEAD_DIM
    r = s // NSA_CMP_STRIDE

    def rows_of_16(x):
        x = x.reshape(b, -1, NSA_KV, HEAD_DIM).transpose(2, 0, 1, 3)
        return x.reshape(NSA_KV, b, -1, NSA_CMP_STRIDE * HEAD_DIM)

    kv_c = jnp.stack([z[..., cols["k_cmp"]:cols["k_cmp"] + kvw], z[..., cols["v_cmp"]:cols["v_cmp"] + kvw]])
    nxt = jnp.concatenate([kv_c[:, :, NSA_CMP_STRIDE:], jnp.zeros_like(kv_c[:, :, :NSA_CMP_STRIDE])], axis=2)
    blocks = jnp.stack([rows_of_16(kv_c[0]), rows_of_16(kv_c[1])])
    blocks_next = jnp.stack([rows_of_16(nxt[0]), rows_of_16(nxt[1])])
    flat = NSA_CMP_LEN * HEAD_DIM
    cmp_tok, cmp_feat = _compress(blocks, blocks_next, cmp_w1.astype(BF16),
                                  cmp_pos.reshape(2, 1, flat).astype(BF16), cmp_w2.astype(BF16))

    n_slc = s // NSA_SLC_LEN
    cmat = jnp.asarray(_importance_matrix_np(n_slc, r), BF16)
    o_cmp, sel = _nsa_select(z, cmp_tok[0], cmp_feat[1], cmat, min(NSA_TOP_N, n_slc))
    win_tiles = _causal_bucket_tiles(ATT_T, -(-NSA_WINDOW // ATT_T) + 1, NSA_WINDOW)
    win_tiles = np.concatenate([win_tiles, np.full((1, ATT_T, ATT_T), MASKED_BUCKET, np.int32)], axis=0)
    bias_win = _build_bias_tiles(tab, win_tiles)
    return _nsa_attend(z, zt, sel, o_cmp, bias_causal, bias_win, cols)


def _scaled_qkv(w_in, scale=SCALE):
    return jnp.concatenate([w_in[:, :D_MODEL] * scale, w_in[:, D_MODEL:]], axis=1).astype(BF16)


def kernel(x, rel_bias, norm_gains, final_gain, a_w_in, a_lambda, a_subln, a_w_out, b_w_in, b_w_out,
           c_w_in, c_w_out, d_w_in, d_cmp_pos, d_cmp_w1, d_cmp_w2, d_w_out, ffn_w_gate, ffn_w_up,
           ffn_w_down, moe_router, moe_w_gate, moe_w_up, moe_w_down):
    b, s, d = x.shape
    assert d == D_MODEL and s % (B_PAIRS[-1][0]) == 0 and s >= (N_NEAR + 1) * ATT_T
    assert _t5_bucket_np(np.array([(N_NEAR - 1) * ATT_T + 1]))[0] == NUM_BUCKETS - 1
    n = b * s
    h = x.reshape(n, d)
    tab = rel_bias.T.astype(F32)
    tab_rel = (tab - tab[:, NUM_BUCKETS - 1:]) * LOG2E
    bias_causal = _build_bias_tiles(tab_rel, _sweep_bucket_tiles(ATT_T))

    w = _scaled_qkv(a_w_in[0], SCALE * LOG2E)
    z, zt = _norm_matmul(h, norm_gains[0, 0], w[:, :2 * d], w[:, 2 * d:].T)
    lam_init = 0.8 - 0.6 * math.exp(-0.3 * 0)
    o = _diff_attention(z.reshape(b, s, -1), zt, a_lambda[0].astype(F32), a_subln[0].astype(F32),
                        bias_causal, lam_init)
    h = _proj_residual(o.reshape(n, d), a_w_out[0].astype(BF16), h)
    h = _ffn(h, norm_gains[0, 1], ffn_w_gate[0].astype(BF16), ffn_w_up[0].astype(BF16),
             ffn_w_down[0].astype(BF16))

    dils = [dil for _, dil in B_PAIRS]
    views = _norm_matmul_views(h, norm_gains[1, 0], _scaled_qkv(b_w_in[0]), dils)
    outs, lses = [], []
    for zv, dil in zip(views, dils):
        bias = _build_bias_tiles(tab, _dilated_bucket_tiles(dil))
        o, lse = _dilated_group(zv.reshape(b, s // dil, -1), bias, dil)
        outs.append(o)
        lses.append(lse)
    h = _dil_combine_proj(outs, lses, dils, b_w_out[0].astype(BF16), h)
    h = _moe(h, norm_gains[1, 1], moe_router[0], moe_w_gate[0].astype(BF16), moe_w_up[0].astype(BF16),
             moe_w_down[0].astype(BF16), final_gain, False)

    w = _scaled_qkv(c_w_in[0])
    z, zt = _norm_matmul(h, norm_gains[2, 0], w[:, :2 * d], w[:, 2 * d:].T)
    o = _stick_attention(z.reshape(b, s, -1), zt)
    h = _proj_residual(o.reshape(n, d), c_w_out[0].astype(BF16), h)
    h = _ffn(h, norm_gains[2, 1], ffn_w_gate[1].astype(BF16), ffn_w_up[1].astype(BF16),
             ffn_w_down[1].astype(BF16))

    o = _nsa_mixer(h, norm_gains[3, 0], b, s, d_w_in[0], d_cmp_pos[0], d_cmp_w1[0], d_cmp_w2[0], tab_rel,
                   bias_causal)
    h = _proj_residual(o.reshape(n, d), d_w_out[0].astype(BF16), h)
    h = _moe(h, norm_gains[3, 1], moe_router[1], moe_w_gate[1].astype(BF16), moe_w_up[1].astype(BF16),
             moe_w_down[1].astype(BF16), final_gain, True)
    return h.reshape(b, s, d)
```

```python
import functools
import math

import numpy as np
import jax
import jax.numpy as jnp
from jax import lax
from jax.experimental import pallas as pl
from jax.experimental.pallas import tpu as pltpu

F32 = jnp.float32
BF16 = jnp.bfloat16

D_MODEL = 1024
HEAD_DIM = 64
N_HEADS = 16
LANES = 128
B_PAIRS = ((128, 1), (512, 4), (2048, 16))
DIL_W = 128
NSA_KV = 4
NSA_GROUP = 4
NSA_CMP_LEN = 32
NSA_CMP_STRIDE = 16
NSA_CMP_HID = 256
NSA_SLC_LEN = 64
NSA_TOP_N = 16
NSA_WINDOW = 512
NUM_BUCKETS = 32
MAX_EXACT = 16
REL_MAX_DIST = 2048
N_EXPERTS = 8
EPS = 1e-6
SCALE = HEAD_DIM ** -0.5
NEG = -1e30
MASKED_BUCKET = NUM_BUCKETS

ATT_T = 256
N_NEAR = 7
ATT_Q_TILES = 2
ATT_GROUP = 2
NSA_SWEEP_GROUP = 2
CMP_ROW_VARIANTS = 4
DEN_ROWS = 16
LOG2E = math.log2(math.e)
SB_T = 256
SB_Q_TILES = 2
SB_CUTOFF = -104.0
VMEM_LIMIT = 56 << 20


def _cparams(sem):
    return pltpu.CompilerParams(dimension_semantics=sem, vmem_limit_bytes=VMEM_LIMIT)


def _resident(block_shape, index_map):
    return pl.BlockSpec(block_shape, index_map, pipeline_mode=pl.Buffered(1))


def _nt_dot(a, b):
    return lax.dot_general(a, b, (((1,), (1,)), ((), ())), preferred_element_type=F32)


def _dot(a, b):
    return jnp.dot(a, b, preferred_element_type=F32)


def _split2(x):
    hi = x.astype(BF16)
    lo = (x - hi.astype(F32)).astype(BF16)
    return hi, lo


def _split3(x):
    hi = x.astype(BF16)
    rem = x - hi.astype(F32)
    mid = rem.astype(BF16)
    lo = (rem - mid.astype(F32)).astype(BF16)
    return hi, mid, lo


def _rms(x, g):
    ms = jnp.mean(x * x, axis=-1, keepdims=True)
    return x * lax.rsqrt(ms + EPS) * g


def _sigmoid(x):
    return 1.0 / (1.0 + jnp.exp(-x))


def _eye(n):
    r = lax.broadcasted_iota(jnp.int32, (n, n), 0)
    c = lax.broadcasted_iota(jnp.int32, (n, n), 1)
    return jnp.where(r == c, 1.0, 0.0).astype(BF16)


def _flip(eye, x):
    return _nt_dot(eye, x)


def _pad_rows(x, top):
    zero = jnp.zeros_like(x)
    return jnp.concatenate([x, zero] if top else [zero, x], axis=0)


def _t5_bucket_np(dist):
    n = np.maximum(dist, 0)
    ratio = np.log(np.maximum(n, 1).astype(np.float32) / np.float32(MAX_EXACT)) / np.float32(
        math.log(REL_MAX_DIST / MAX_EXACT))
    large = np.minimum(MAX_EXACT + (ratio * np.float32(NUM_BUCKETS - MAX_EXACT)).astype(np.int32),
                       NUM_BUCKETS - 1)
    return np.where(n < MAX_EXACT, n, large).astype(np.int32)


def _causal_bucket_tiles(t, n_tiles, max_dist=None):
    i = np.arange(t)[None, None, :]
    j = np.arange(t)[None, :, None]
    dist = np.arange(n_tiles)[:, None, None] * t + i - j
    ok = dist >= 0
    if max_dist is not None:
        ok = ok & (dist < max_dist)
    return np.where(ok, _t5_bucket_np(dist), MASKED_BUCKET).astype(np.int32)


def _sweep_bucket_tiles(t):
    far = np.full((1, t, t), NUM_BUCKETS - 1, np.int32)
    future = np.full((1, t, t), MASKED_BUCKET, np.int32)
    return np.concatenate([_causal_bucket_tiles(t, N_NEAR), far, future], axis=0)


def _bias_tile_index(delta):
    return jnp.where(delta < 0, N_NEAR + 1, jnp.minimum(delta, N_NEAR))


def _sweep_bias(bias_ref, col, first_q, nq, first_key, count):
    return jnp.concatenate(
        [jnp.concatenate([bias_ref[col, _bias_tile_index(first_q + a - (first_key + u))] for a in range(nq)], axis=1)
         for u in range(count)], axis=0)


def _dilated_bucket_tiles(dilation):
    w = DIL_W
    steps = w + np.arange(w)[None, :] - np.arange(2 * w)[:, None]
    ok = (steps >= 0) & (steps <= w)
    bkt = np.where(ok, _t5_bucket_np(dilation * steps), MASKED_BUCKET)
    first = np.where(np.arange(2 * w)[:, None] < w, MASKED_BUCKET, bkt)
    return np.stack([first, bkt]).astype(np.int32)


def _bias_build_kernel(tab_ref, bkt_ref, o_ref):
    c = pl.program_id(0)
    bkt = bkt_ref[0]
    out = jnp.full(bkt.shape, NEG, F32)
    for b in range(NUM_BUCKETS):
        out = jnp.where(bkt == b, tab_ref[c, b], out)
    o_ref[0, 0] = out


def _build_bias_tiles(tab, buckets):
    n, r, c = buckets.shape
    return pl.pallas_call(
        _bias_build_kernel,
        out_shape=jax.ShapeDtypeStruct((N_HEADS, n, r, c), F32),
        grid=(N_HEADS, n),
        in_specs=[pl.BlockSpec(memory_space=pltpu.SMEM),
                  pl.BlockSpec((1, r, c), lambda h, i: (i, 0, 0))],
        out_specs=pl.BlockSpec((1, 1, r, c), lambda h, i: (h, i, 0, 0)),
        compiler_params=_cparams(("arbitrary", "arbitrary")),
        name="bias_tiles",
    )(tab, jnp.asarray(buckets))


def _norm_matmul_kernel(x_ref, g_ref, w_ref, *refs, chunk, feature_major):
    xn = _rms(x_ref[...], g_ref[...]).astype(BF16)
    o_ref = refs[1] if feature_major else refs[0]
    nout = o_ref.shape[1]
    for c0 in range(0, nout, chunk):
        c1 = min(c0 + chunk, nout)
        o_ref[:, c0:c1] = _dot(xn, w_ref[:, c0:c1]).astype(o_ref.dtype)
    if feature_major:
        wt_ref, ot_ref = refs[0], refs[2]
        nt = ot_ref.shape[0]
        for c0 in range(0, nt, chunk):
            c1 = min(c0 + chunk, nt)
            ot_ref[c0:c1, :] = _nt_dot(wt_ref[c0:c1, :], xn).astype(ot_ref.dtype)


def _norm_matmul(x, g, w, wt=None, tm=512):
    n, d = x.shape
    nout = w.shape[1]
    in_specs = [pl.BlockSpec((tm, d), lambda i: (i, 0)),
                pl.BlockSpec((1, d), lambda i: (0, 0)),
                pl.BlockSpec((d, nout), lambda i: (0, 0))]
    out_shape = [jax.ShapeDtypeStruct((n, nout), BF16)]
    out_specs = [pl.BlockSpec((tm, nout), lambda i: (i, 0))]
    args = [x, g.reshape(1, d), w]
    if wt is not None:
        nt = wt.shape[0]
        in_specs.append(pl.BlockSpec((nt, d), lambda i: (0, 0)))
        out_shape.append(jax.ShapeDtypeStruct((nt, n), BF16))
        out_specs.append(pl.BlockSpec((nt, tm), lambda i: (0, i)))
        args.append(wt)
    out = pl.pallas_call(
        functools.partial(_norm_matmul_kernel, chunk=512, feature_major=wt is not None),
        out_shape=tuple(out_shape),
        grid=(n // tm,),
        in_specs=in_specs,
        out_specs=tuple(out_specs),
        compiler_params=_cparams(("parallel",)),
        name="norm_matmul",
    )(*args)
    return out if wt is not None else out[0]


def _norm_matmul_views_kernel(x_ref, g_ref, w_ref, *refs, chunk, dils):
    out_refs, z_ref = refs[:len(dils)], refs[len(dils)]
    xn = _rms(x_ref[...], g_ref[...]).astype(BF16)
    n_col, tm, _ = z_ref.shape
    nout = n_col * LANES
    for c0 in range(0, nout, chunk):
        c1 = min(c0 + chunk, nout)
        zc = _dot(xn, w_ref[:, c0:c1])
        for k in range((c1 - c0) // LANES):
            z_ref[c0 // LANES + k] = zc[:, k * LANES:(k + 1) * LANES]
    for o_ref, d in zip(out_refs, dils):
        for r in range(d):
            for c in range(n_col):
                col = r * nout + c * LANES
                o_ref[:, col:col + LANES] = z_ref[c, pl.ds(r, tm // d, stride=d), :].astype(o_ref.dtype)


def _norm_matmul_views(x, g, w, dils, tm=512):
    n, d_in = x.shape
    nout = w.shape[1]
    return pl.pallas_call(
        functools.partial(_norm_matmul_views_kernel, chunk=512, dils=tuple(dils)),
        out_shape=tuple(jax.ShapeDtypeStruct((n // d, d * nout), BF16) for d in dils),
        grid=(n // tm,),
        in_specs=[pl.BlockSpec((tm, d_in), lambda i: (i, 0)),
                  pl.BlockSpec((1, d_in), lambda i: (0, 0)),
                  pl.BlockSpec((d_in, nout), lambda i: (0, 0))],
        out_specs=tuple(pl.BlockSpec((tm // d, d * nout), lambda i: (i, 0)) for d in dils),
        scratch_shapes=[pltpu.VMEM((nout // LANES, tm, LANES), F32)],
        compiler_params=_cparams(("parallel",)),
        name="norm_matmul_views",
    )(x, g.reshape(1, d_in), w)


def _proj_res_kernel(a_ref, w_ref, h_ref, o_ref):
    o_ref[...] = h_ref[...] + _dot(a_ref[...], w_ref[...])


def _proj_residual(a, w, h, tm=512):
    n, d = h.shape
    k = a.shape[1]
    return pl.pallas_call(
        _proj_res_kernel,
        out_shape=jax.ShapeDtypeStruct((n, d), F32),
        grid=(n // tm,),
        in_specs=[pl.BlockSpec((tm, k), lambda i: (i, 0)),
                  pl.BlockSpec((k, d), lambda i: (0, 0)),
                  pl.BlockSpec((tm, d), lambda i: (i, 0))],
        out_specs=pl.BlockSpec((tm, d), lambda i: (i, 0)),
        compiler_params=_cparams(("parallel",)),
        name="proj_residual",
    )(a, w, h)


def _dil_combine_proj_kernel(o1, o2, o3, l1, l2, l3, w_ref, h_ref, o_ref, nat_ref, *, dils):
    tm = h_ref.shape[0]
    d_model = h_ref.shape[1]

    def in_position_order(view_ref, slot, d):
        if d == 1:
            return view_ref[...].astype(F32)
        for r in range(d):
            for c in range(d_model // LANES):
                col = r * d_model + c * LANES
                nat_ref[slot, c, pl.ds(r, tm // d, stride=d), :] = view_ref[:, col:col + LANES].astype(F32)
        return jnp.concatenate([nat_ref[slot, c] for c in range(d_model // LANES)], axis=1)

    outs = [in_position_order(ref, 2 * gi, d) for gi, (ref, d) in enumerate(zip((o1, o2, o3), dils))]
    ls = [in_position_order(ref, 2 * gi + 1, d) for gi, (ref, d) in enumerate(zip((l1, l2, l3), dils))]
    m = jnp.maximum(jnp.maximum(ls[0], ls[1]), ls[2])
    es = [jnp.exp(l - m) for l in ls]
    den = es[0] + es[1] + es[2]
    a = (es[0] * outs[0] + es[1] * outs[1] + es[2] * outs[2]) / den
    o_ref[...] = h_ref[...] + _dot(a.astype(BF16), w_ref[...])


def _dil_combine_proj(outs, lses, dils, w, h, tm=512):
    n, d = h.shape
    row = pl.BlockSpec((tm, d), lambda i: (i, 0))
    views = [pl.BlockSpec((tm // dil, dil * d), lambda i: (i, 0)) for dil in dils]
    return pl.pallas_call(
        functools.partial(_dil_combine_proj_kernel, dils=tuple(dils)),
        out_shape=jax.ShapeDtypeStruct((n, d), F32),
        grid=(n // tm,),
        in_specs=views + views + [pl.BlockSpec((d, d), lambda i: (0, 0)), row],
        out_specs=row,
        scratch_shapes=[pltpu.VMEM((2 * len(dils), d // LANES, tm, LANES), F32)],
        compiler_params=_cparams(("parallel",)),
        name="dilated_combine_proj",
    )(*outs, *lses, w, h)


def _ffn_kernel(h_ref, g_ref, wg_ref, wu_ref, wd_ref, o_ref, xn_ref, acc_ref):
    f = pl.program_id(1)

    @pl.when(f == 0)
    def _():
        xn_ref[...] = _rms(h_ref[...], g_ref[...]).astype(BF16)
        acc_ref[...] = jnp.zeros_like(acc_ref)

    xn = xn_ref[...]
    gate = _dot(xn, wg_ref[...])
    up = _dot(xn, wu_ref[...])
    act = (gate * _sigmoid(gate) * up).astype(BF16)
    acc_ref[...] += _dot(act, wd_ref[...])

    @pl.when(f == pl.num_programs(1) - 1)
    def _():
        o_ref[...] = h_ref[...] + acc_ref[...]


def _ffn(h, g, wg, wu, wd, tm=1024, n_f=2):
    n, d = h.shape
    dff = wg.shape[1]
    tf = dff // n_f
    return pl.pallas_call(
        _ffn_kernel,
        out_shape=jax.ShapeDtypeStruct((n, d), F32),
        grid=(n // tm, n_f),
        in_specs=[pl.BlockSpec((tm, d), lambda i, f: (i, 0)),
                  pl.BlockSpec((1, d), lambda i, f: (0, 0)),
                  pl.BlockSpec((d, tf), lambda i, f: (0, f)),
                  pl.BlockSpec((d, tf), lambda i, f: (0, f)),
                  pl.BlockSpec((tf, d), lambda i, f: (f, 0))],
        out_specs=pl.BlockSpec((tm, d), lambda i, f: (i, 0)),
        scratch_shapes=[pltpu.VMEM((tm, d), BF16), pltpu.VMEM((tm, d), F32)],
        compiler_params=_cparams(("parallel", "arbitrary")),
        name="ffn",
    )(h, g.reshape(1, d), wg, wu, wd)


def _moe_kernel(h_ref, g_ref, rh_ref, rl_ref, wg_ref, wu_ref, wd_ref, fg_ref, o_ref,
                xn_ref, comb_ref, acc_ref, *, final_norm):
    e = pl.program_id(1)
    col = lax.broadcasted_iota(jnp.int32, comb_ref.shape, 1)

    @pl.when(e == 0)
    def _():
        xn = _rms(h_ref[...], g_ref[...])
        xh, xl = _split2(xn)
        xn_ref[...] = xh
        logits = _dot(xh, rh_ref[...]) + _dot(xh, rl_ref[...]) + _dot(xl, rh_ref[...])
        colf = col.astype(F32)
        lg = jnp.where(col < N_EXPERTS, logits, NEG)
        m1 = jnp.max(lg, axis=-1, keepdims=True)
        i1 = jnp.min(jnp.where(lg == m1, colf, float(LANES)), axis=-1, keepdims=True)
        lg2 = jnp.where(colf == i1, NEG, lg)
        m2 = jnp.max(lg2, axis=-1, keepdims=True)
        i2 = jnp.min(jnp.where(lg2 == m2, colf, float(LANES)), axis=-1, keepdims=True)
        e2 = jnp.exp(m2 - m1)
        g1 = 1.0 / (1.0 + e2)
        g2 = e2 / (1.0 + e2)
        comb_ref[...] = jnp.where(colf == i1, g1, 0.0) + jnp.where(colf == i2, g2, 0.0)
        acc_ref[...] = jnp.zeros_like(acc_ref)

    xn = xn_ref[...]
    gate = _dot(xn, wg_ref[0])
    up = _dot(xn, wu_ref[0])
    ce = jnp.sum(jnp.where(col == e, comb_ref[...], 0.0), axis=-1, keepdims=True)
    act = (gate * _sigmoid(gate) * up * ce).astype(BF16)
    acc_ref[...] += _dot(act, wd_ref[0])

    @pl.when(e == pl.num_programs(1) - 1)
    def _():
        y = h_ref[...] + acc_ref[...]
        if final_norm:
            y = _rms(y, fg_ref[...])
        o_ref[...] = y


def _moe(h, g, router, wg, wu, wd, final_gain, final_norm, tm=1024):
    n, d = h.shape
    ne, _, dfe = wg.shape
    rpad = jnp.zeros((d, LANES), F32).at[:, :ne].set(router)
    rh, rl = _split2(rpad)
    return pl.pallas_call(
        functools.partial(_moe_kernel, final_norm=final_norm),
        out_shape=jax.ShapeDtypeStruct((n, d), F32),
        grid=(n // tm, ne),
        in_specs=[pl.BlockSpec((tm, d), lambda i, e: (i, 0)),
                  pl.BlockSpec((1, d), lambda i, e: (0, 0)),
                  pl.BlockSpec((d, LANES), lambda i, e: (0, 0)),
                  pl.BlockSpec((d, LANES), lambda i, e: (0, 0)),
                  pl.BlockSpec((1, d, dfe), lambda i, e: (e, 0, 0)),
                  pl.BlockSpec((1, d, dfe), lambda i, e: (e, 0, 0)),
                  pl.BlockSpec((1, dfe, d), lambda i, e: (e, 0, 0)),
                  pl.BlockSpec((1, d), lambda i, e: (0, 0))],
        out_specs=pl.BlockSpec((tm, d), lambda i, e: (i, 0)),
        scratch_shapes=[pltpu.VMEM((tm, d), BF16), pltpu.VMEM((tm, LANES), F32),
                        pltpu.VMEM((tm, d), F32)],
        compiler_params=_cparams(("parallel", "arbitrary")),
        name="moe",
    )(h, g.reshape(1, d), rh, rl, wg, wu, wd, final_gain.reshape(1, d))


def _with_ones(vt):
    return jnp.concatenate([vt, jnp.ones((DEN_ROWS, vt.shape[1]), vt.dtype)], axis=0)


def _flash_step(s, vt1, m_ref, acc_ref, idx):
    m_old = m_ref[idx]
    m_new = jnp.maximum(m_old, jnp.max(s, axis=0, keepdims=True))
    p = jnp.exp2(s - m_new).astype(BF16)
    acc_ref[idx] = jnp.exp2(m_old - m_new) * acc_ref[idx] + _dot(vt1, p)
    m_ref[idx] = m_new


def _flash_result(acc_ref, idx):
    acc = acc_ref[idx]
    nf = acc.shape[0] - DEN_ROWS
    return acc[:nf] / acc[nf:nf + 1]


def _far_groups(qi, group):
    return lax.shift_right_logical(jnp.maximum(qi + 1 - N_NEAR, 0), int(math.log2(group)))


def _pipelined_sweep(n_far, n_groups, scores, softmax, sa_ref, sb_ref):
    scores(0, sa_ref)

    def pairs(near):
        def pair(k, carry):
            scores(2 * k + 1, sb_ref)
            softmax(2 * k, sa_ref, near)
            scores(2 * k + 2, sa_ref)
            softmax(2 * k + 1, sb_ref, near)
            return carry
        return pair

    far_pairs = lax.shift_right_logical(n_far, 1)
    full_pairs = lax.shift_right_logical(n_groups, 1)
    lax.fori_loop(0, far_pairs, pairs(False), 0)
    lax.fori_loop(far_pairs, full_pairs, pairs(True), 0)

    @pl.when((n_groups & 1) == 1)
    def _():
        softmax(n_groups - 1, sa_ref, True)


def _flash_init(m_ref, acc_ref):
    m_ref[...] = jnp.full(m_ref.shape, NEG, F32)
    acc_ref[...] = jnp.zeros_like(acc_ref)


def _diff_attn_kernel(lam_ref, q_ref, k_ref, vt_ref, bias_ref, g_ref, o_ref, m_ref, acc_ref, sa_ref, sb_ref,
                      *, t, nq, lam_init):
    qi = pl.program_id(2)
    tq = nq * t
    qt = _flip(_eye(LANES), q_ref[...]).astype(BF16)
    rhs = jnp.concatenate([_pad_rows(qt[:HEAD_DIM], True), _pad_rows(qt[HEAD_DIM:], False)], axis=1)
    _flash_init(m_ref, acc_ref)
    gt = ATT_GROUP * t
    first_q = qi * nq
    n_groups = lax.shift_right_logical(first_q + nq - 1 + ATT_GROUP, int(math.log2(ATT_GROUP)))

    def key_start(gi):
        return pl.multiple_of(jnp.minimum(gi, n_groups - 1) * gt, gt)

    def scores(gi, s_ref):
        s_ref[...] = _dot(k_ref[pl.ds(key_start(gi), gt), :], rhs)

    def softmax(gi, s_ref, near):
        vt1 = _with_ones(vt_ref[:, pl.ds(key_start(gi), gt)])
        for mi in range(2):
            s = s_ref[:, mi * tq:(mi + 1) * tq]
            if near:
                s = s + _sweep_bias(bias_ref, mi, first_q, nq, gi * ATT_GROUP, ATT_GROUP)
            _flash_step(s, vt1, m_ref, acc_ref, mi)

    _pipelined_sweep(_far_groups(first_q, ATT_GROUP), n_groups, scores, softmax, sa_ref, sb_ref)

    lam = lam_ref[...]
    lam_full = (jnp.exp(jnp.sum(lam[0:1] * lam[1:2], axis=-1, keepdims=True))
                - jnp.exp(jnp.sum(lam[2:3] * lam[3:4], axis=-1, keepdims=True)) + lam_init)
    o = _flash_result(acc_ref, 0) - lam_full * _flash_result(acc_ref, 1)
    ms = jnp.mean(o * o, axis=0, keepdims=True)
    y = (o * lax.rsqrt(ms + EPS) * g_ref[...] * (1.0 - lam_init)).astype(BF16)
    o_ref[...] = _flip(_eye(tq), y).astype(o_ref.dtype)


def _diff_attention(z, zt, lam, subln, bias_tiles, lam_init):
    b, s, _ = z.shape
    t = ATT_T
    tq = ATT_Q_TILES * t
    nh = N_HEADS // 2
    n_tiles = bias_tiles.shape[1]
    return pl.pallas_call(
        functools.partial(_diff_attn_kernel, t=t, nq=ATT_Q_TILES, lam_init=lam_init),
        out_shape=jax.ShapeDtypeStruct((b, s, D_MODEL), BF16),
        grid=(b, nh, s // tq),
        in_specs=[pl.BlockSpec((4, HEAD_DIM), lambda bi, h, qi: (0, 0)),
                  pl.BlockSpec((None, tq, LANES), lambda bi, h, qi: (bi, qi, h)),
                  pl.BlockSpec((None, s, LANES), lambda bi, h, qi: (bi, 0, nh + h)),
                  pl.BlockSpec((LANES, s), lambda bi, h, qi: (h, bi)),
                  pl.BlockSpec((2, n_tiles, t, t), lambda bi, h, qi: (h, 0, 0, 0)),
                  pl.BlockSpec((LANES, 1), lambda bi, h, qi: (0, 0))],
        out_specs=pl.BlockSpec((None, tq, LANES), lambda bi, h, qi: (bi, qi, h)),
        scratch_shapes=[pltpu.VMEM((2, 1, tq), F32), pltpu.VMEM((2, LANES + DEN_ROWS, tq), F32),
                        pltpu.VMEM((ATT_GROUP * t, 2 * tq), F32), pltpu.VMEM((ATT_GROUP * t, 2 * tq), F32)],
        compiler_params=_cparams(("parallel", "parallel", "arbitrary")),
        name="diff_attention",
    )(lam, z, z, zt, bias_tiles, subln.reshape(LANES, 1))


def _dilated_kernel(q_ref, kp_ref, kc_ref, vp_ref, vc_ref, bias_ref, o_ref, lse_ref):
    n = pl.program_id(2)
    variant = jnp.minimum(n, 1)
    w = q_ref.shape[0]
    eye = _eye(LANES)
    eye_w = _eye(w)
    pairs = range(N_HEADS // 2)
    cols = [slice(hp * LANES, (hp + 1) * LANES) for hp in pairs]
    qts = [_flip(eye, q_ref[:, c]).astype(BF16) for c in cols]
    kcats = [jnp.concatenate([kp_ref[:, c], kc_ref[:, c]], axis=0) for c in cols]
    vts = [_flip(eye, jnp.concatenate([vp_ref[:, c], vc_ref[:, c]], axis=0)).astype(BF16) for c in cols]
    s2s = [_dot(kcats[hp], jnp.concatenate([_pad_rows(qts[hp][:HEAD_DIM], True),
                                            _pad_rows(qts[hp][HEAD_DIM:], False)], axis=1))
           for hp in pairs]
    ps, lses = [], []
    for hp in pairs:
        for sub in range(2):
            s = s2s[hp][:, sub * w:(sub + 1) * w] + bias_ref[2 * hp + sub, variant]
            m = jnp.max(s, axis=0, keepdims=True)
            e = jnp.exp(s - m)
            den = jnp.sum(e, axis=0, keepdims=True)
            ps.append((e * (1.0 / den)).astype(BF16))
            lses.append(jnp.broadcast_to(m + jnp.log(den), (HEAD_DIM, w)))
    outs = [_dot(vts[hp][sub * HEAD_DIM:(sub + 1) * HEAD_DIM], ps[2 * hp + sub])
            for hp in pairs for sub in range(2)]
    for hp in pairs:
        pair_out = jnp.concatenate(outs[2 * hp:2 * hp + 2], axis=0).astype(BF16)
        o_ref[:, cols[hp]] = _flip(eye_w, pair_out).astype(o_ref.dtype)
        pair_lse = jnp.concatenate(lses[2 * hp:2 * hp + 2], axis=0)
        lse_ref[:, cols[hp]] = sum(_flip(eye_w, term) for term in _split3(pair_lse))


def _dilated_group(zv, bias, dilation):
    b, m_len, _ = zv.shape
    w = DIL_W
    nb = m_len // w
    blk = lambda sect, prev: pl.BlockSpec(
        (None, w, D_MODEL),
        (lambda bi, r, n: (bi, jnp.maximum(n - 1, 0), 3 * r + sect)) if prev
        else (lambda bi, r, n: (bi, n, 3 * r + sect)))
    out_blk = pl.BlockSpec((None, w, D_MODEL), lambda bi, r, n: (bi, n, r))
    o, lse = pl.pallas_call(
        _dilated_kernel,
        out_shape=(jax.ShapeDtypeStruct((b, m_len, dilation * D_MODEL), BF16),
                   jax.ShapeDtypeStruct((b, m_len, dilation * D_MODEL), F32)),
        grid=(b, dilation, nb),
        in_specs=[blk(0, False), blk(1, True), blk(1, False), blk(2, True), blk(2, False),
                  pl.BlockSpec((N_HEADS, 2, 2 * w, w), lambda bi, r, n: (0, 0, 0, 0))],
        out_specs=(out_blk, out_blk),
        compiler_params=_cparams(("parallel", "parallel", "arbitrary")),
        name=f"dilated_d{dilation}",
    )(zv, zv, zv, zv, zv, bias)
    return o.reshape(b * m_len, dilation * D_MODEL), lse.reshape(b * m_len, dilation * D_MODEL)


def _stick_kernel(q_ref, k_ref, vt_ref, o_ref, acc_ref, carry_ref, *, t, nq):
    qi = pl.program_id(2)
    tq = nq * t
    first_q = qi * nq
    qt = _flip(_eye(LANES), q_ref[...]).astype(BF16)
    rhs = jnp.concatenate([_pad_rows(qt[:HEAD_DIM], True), _pad_rows(qt[HEAD_DIM:], False)], axis=1)
    row = lax.broadcasted_iota(jnp.int32, (t, t), 0)
    col = lax.broadcasted_iota(jnp.int32, (t, t), 1)
    after = jnp.where(col > row, 1.0, 0.0).astype(BF16)
    key_off = lax.broadcasted_iota(jnp.int32, (t, 2 * tq), 0)
    q_off = lax.broadcasted_iota(jnp.int32, (t, 2 * tq), 1) % tq
    acc_ref[...] = jnp.zeros_like(acc_ref)
    carry_ref[...] = jnp.zeros_like(carry_ref)

    def block(j, masked):
        start = pl.multiple_of(j * t, t)
        z = _dot(k_ref[pl.ds(start, t), :], rhs)
        log_keep = -(jnp.maximum(z, 0.0) + jnp.log(1.0 + jnp.exp(-jnp.abs(z))))
        if masked:
            before = key_off + (j - first_q) * t < q_off
            log_keep = jnp.where(before, log_keep, 0.0)
        hi, lo = _split2(log_keep)
        later = _dot(after, hi) + _dot(after, lo) + carry_ref[...]
        a = jnp.exp(z + log_keep + later)
        if masked:
            a = jnp.where(before, a, 0.0)
        a = a.astype(BF16)
        for sub in range(2):
            vt = vt_ref[sub * HEAD_DIM:(sub + 1) * HEAD_DIM, pl.ds(start, t)]
            acc_ref[sub] += _dot(vt, a[:, sub * tq:(sub + 1) * tq])
        carry_ref[...] += jnp.sum(log_keep, axis=0, keepdims=True)

    for a_tile in reversed(range(nq)):
        block(first_q + a_tile, True)

    def cond(c):
        return jnp.logical_and(c[0] >= 0, c[1] >= SB_CUTOFF)

    def body(c):
        block(c[0], False)
        return c[0] - 1, jnp.max(carry_ref[...])

    lax.while_loop(cond, body, (first_q - 1, jnp.max(carry_ref[...])))
    out = jnp.concatenate([acc_ref[0], acc_ref[1]], axis=0).astype(BF16)
    o_ref[...] = _flip(_eye(tq), out).astype(o_ref.dtype)


def _stick_attention(z, zt):
    b, s, _ = z.shape
    t = SB_T
    tq = SB_Q_TILES * t
    nh = N_HEADS // 2
    return pl.pallas_call(
        functools.partial(_stick_kernel, t=t, nq=SB_Q_TILES),
        out_shape=jax.ShapeDtypeStruct((b, s, D_MODEL), BF16),
        grid=(b, nh, s // tq),
        in_specs=[pl.BlockSpec((None, tq, LANES), lambda bi, h, qi: (bi, qi, h)),
                  pl.BlockSpec((None, s, LANES), lambda bi, h, qi: (bi, 0, nh + h)),
                  pl.BlockSpec((LANES, s), lambda bi, h, qi: (h, bi))],
        out_specs=pl.BlockSpec((None, tq, LANES), lambda bi, h, qi: (bi, qi, h)),
        scratch_shapes=[pltpu.VMEM((2, HEAD_DIM, tq), F32), pltpu.VMEM((1, 2 * tq), F32)],
        compiler_params=_cparams(("parallel", "parallel", "arbitrary")),
        name="stick_breaking",
    )(z, z, zt)


def _compress_kernel(a_ref, as_ref, w1_ref, pos_ref, w2_ref, w2t_ref, o_ref, ot_ref):
    half = a_ref.shape[1]
    w1 = w1_ref[...]
    const = _dot(jnp.broadcast_to(pos_ref[...], (8, 2 * half)), w1)[0:1]
    pre = _dot(a_ref[...], w1[:half]) + _dot(as_ref[...], w1[half:]) + const
    hid = 0.5 * pre * (1.0 + jnp.tanh(math.sqrt(2.0 / math.pi) * (pre + 0.044715 * pre * pre * pre)))
    hid = hid.astype(BF16)
    o_ref[...] = _dot(hid, w2_ref[...]).astype(o_ref.dtype)
    ot_ref[...] = _nt_dot(w2t_ref[...], hid).astype(ot_ref.dtype)


def _compress(blocks, blocks_next, w1, pos, w2, tr=256):
    _, nkv, b, r, half = blocks.shape
    tr = min(tr, r)
    w2pad = jnp.concatenate([w2, jnp.zeros_like(w2)], axis=-1)
    w2t = jnp.swapaxes(w2, 1, 2)
    spec = pl.BlockSpec((None, None, None, tr, half), lambda j, hk, bi, ri: (j, hk, bi, ri, 0))
    return pl.pallas_call(
        _compress_kernel,
        out_shape=(jax.ShapeDtypeStruct((2, nkv, b, r, LANES), BF16),
                   jax.ShapeDtypeStruct((2, nkv, b, HEAD_DIM, r), BF16)),
        grid=(2, nkv, b, r // tr),
        in_specs=[spec, spec,
                  pl.BlockSpec((None, 2 * half, NSA_CMP_HID), lambda j, hk, bi, ri: (j, 0, 0)),
                  pl.BlockSpec((None, 1, 2 * half), lambda j, hk, bi, ri: (j, 0, 0)),
                  pl.BlockSpec((None, NSA_CMP_HID, LANES), lambda j, hk, bi, ri: (j, 0, 0)),
                  pl.BlockSpec((None, HEAD_DIM, NSA_CMP_HID), lambda j, hk, bi, ri: (j, 0, 0))],
        out_specs=(pl.BlockSpec((None, None, None, tr, LANES), lambda j, hk, bi, ri: (j, hk, bi, ri, 0)),
                   pl.BlockSpec((None, None, None, HEAD_DIM, tr), lambda j, hk, bi, ri: (j, hk, bi, 0, ri))),
        compiler_params=_cparams(("parallel", "parallel", "parallel", "arbitrary")),
        name="nsa_compress",
    )(blocks, blocks_next, w1, pos, w2pad, w2t)


def _nsa_select_kernel(q_ref, kc_ref, vct_ref, cmat_ref, o_ref, sel_ref, *, t, n_sel):
    qi = pl.program_id(2)
    r = kc_ref.shape[0]
    n_slc = sel_ref.shape[0]
    qt = _flip(_eye(NSA_GROUP * HEAD_DIM), q_ref[...]).astype(BF16)
    tpos = qi * t + lax.broadcasted_iota(jnp.int32, (1, t), 1)
    live = (tpos >= NSA_CMP_LEN - 1).astype(F32)

    def compressed_branch(rows):
        kc = kc_ref[:rows, :]
        vct = vct_ref[:, :rows]
        cidx = lax.broadcasted_iota(jnp.int32, (rows, t), 0)
        seen = NSA_CMP_STRIDE * cidx + (NSA_CMP_LEN - 1) <= tpos
        imp = jnp.zeros((rows, t), F32)
        for g in range(NSA_GROUP):
            rhs = _pad_rows(qt[g * HEAD_DIM:(g + 1) * HEAD_DIM], True)
            sc = jnp.where(seen, _dot(kc, rhs), NEG)
            e = jnp.exp2(sc - jnp.max(sc, axis=0, keepdims=True))
            pc = e * (live / jnp.sum(e, axis=0, keepdims=True))
            o_ref[g * HEAD_DIM:(g + 1) * HEAD_DIM, :] = _dot(vct, pc.astype(BF16)).astype(o_ref.dtype)
            imp = imp + pc
        n_blk = rows // (NSA_SLC_LEN // NSA_CMP_STRIDE)
        cmat = cmat_ref[:n_blk, :rows]
        imp_slc = sum(_dot(cmat, term) for term in _split3(imp))
        blk = lax.broadcasted_iota(jnp.int32, (n_blk, t), 0)
        cur = tpos // NSA_SLC_LEN
        forced = (blk == 0) | (blk == cur) | (blk == cur - 1)
        valid = blk * NSA_SLC_LEN <= tpos
        ranked = jnp.logical_and(valid, jnp.logical_not(forced))
        score = jnp.where(ranked, imp_slc, -1.0)
        blkf = blk.astype(F32)
        for _ in range(n_sel - 3):
            top = jnp.max(score, axis=0, keepdims=True)
            first = jnp.min(jnp.where(score == top, blkf, float(n_blk)), axis=0, keepdims=True)
            score = jnp.where(blkf == first, -jnp.inf, score)
        chosen = jnp.logical_or(jnp.logical_and(valid, forced), jnp.logical_and(ranked, score == -jnp.inf))
        sel_ref[:n_blk, :] = jnp.where(chosen, 1.0, 0.0).astype(sel_ref.dtype)
        if n_blk < n_slc:
            sel_ref[n_blk:, :] = jnp.zeros((n_slc - n_blk, t), sel_ref.dtype)

    quarter = r // CMP_ROW_VARIANTS
    need = (qi * t + t - NSA_CMP_LEN) // NSA_CMP_STRIDE
    variant = jnp.clip(need // quarter, 0, CMP_ROW_VARIANTS - 1)
    for v in range(CMP_ROW_VARIANTS):
        @pl.when(variant == v)
        def _(v=v):
            compressed_branch((v + 1) * quarter)


def _nsa_select(z, kc, vct, cmat, n_sel):
    b, s, _ = z.shape
    t = ATT_T
    r = kc.shape[2]
    n_slc = cmat.shape[0]
    gw = NSA_GROUP * HEAD_DIM
    assert r % CMP_ROW_VARIANTS == 0 and (r // CMP_ROW_VARIANTS) % 16 == 0 and t % 16 == 0
    return pl.pallas_call(
        functools.partial(_nsa_select_kernel, t=t, n_sel=n_sel),
        out_shape=(jax.ShapeDtypeStruct((b, NSA_KV, gw, s), BF16),
                   jax.ShapeDtypeStruct((b, NSA_KV, n_slc, s), BF16)),
        grid=(b, NSA_KV, s // t),
        in_specs=[pl.BlockSpec((None, t, gw), lambda bi, hk, qi: (bi, qi, hk)),
                  pl.BlockSpec((None, None, r, LANES), lambda bi, hk, qi: (hk, bi, 0, 0)),
                  pl.BlockSpec((None, None, HEAD_DIM, r), lambda bi, hk, qi: (hk, bi, 0, 0)),
                  pl.BlockSpec((n_slc, r), lambda bi, hk, qi: (0, 0))],
        out_specs=(pl.BlockSpec((None, None, gw, t), lambda bi, hk, qi: (bi, hk, 0, qi)),
                   pl.BlockSpec((None, None, n_slc, t), lambda bi, hk, qi: (bi, hk, 0, qi))),
        compiler_params=_cparams(("parallel", "parallel", "arbitrary")),
        name="nsa_select",
    )(z, kc, vct, cmat)


def _nsa_attend_kernel(q_ref, ks_ref, kw_ref, vst_ref, vwt_ref, sel_ref, ocmp_ref, zg_ref, bs_ref, bw_ref,
                       o_ref, m_ref, acc_ref, res_ref, gate_ref, sel32_ref, sa_ref, sb_ref, *, t, nq, n_win):
    hk = pl.program_id(1)
    qi = pl.program_id(2)
    tq = nq * t
    first_q = qi * nq
    per_tile = t // NSA_SLC_LEN
    gw = NSA_GROUP * HEAD_DIM
    qt = _flip(_eye(gw), q_ref[...]).astype(BF16)
    low_half = (hk % 2) == 0
    q_heads = [qt[g * HEAD_DIM:(g + 1) * HEAD_DIM] for g in range(NSA_GROUP)]
    sel32_ref[...] = sel_ref[...].astype(F32)

    _flash_init(m_ref, acc_ref)
    grp = NSA_SWEEP_GROUP
    gt = grp * t
    n_groups = lax.shift_right_logical(first_q + nq - 1 + grp, int(math.log2(grp)))

    def first_tile(gi):
        return jnp.minimum(gi, n_groups - 1) * grp

    n_blk = grp * per_tile
    lane = lax.broadcasted_iota(jnp.int32, (gt, LANES), 1)
    blk_of_key = lax.broadcasted_iota(jnp.int32, (gt, LANES), 0) // NSA_SLC_LEN
    key_lanes = lax.shift_right_logical(lane, int(math.log2(HEAD_DIM))) == hk % 2
    one_hot = jnp.where(lane == blk_of_key + jnp.where(low_half, HEAD_DIM, 0), 1.0, 0.0).astype(BF16)

    def with_extra_rows(extra):
        return jnp.concatenate(
            [jnp.where(low_half, jnp.concatenate([qg, extra], axis=0), jnp.concatenate([extra, qg], axis=0))
             for qg in q_heads], axis=1)

    def scores(gi, s_ref):
        j0 = first_tile(gi)
        keys = jnp.where(key_lanes, ks_ref[pl.ds(pl.multiple_of(j0 * t, t), gt), :], one_hot)
        drop = (sel32_ref[pl.ds(pl.multiple_of(j0 * per_tile, n_blk), n_blk), :] - 1.0) * (-NEG)
        drop = jnp.concatenate([drop, jnp.zeros((HEAD_DIM - n_blk, tq), F32)], axis=0).astype(BF16)
        s_ref[...] = _dot(keys, with_extra_rows(drop))

    def softmax(gi, s_ref, near):
        j0 = first_tile(gi)
        vt1 = _with_ones(vst_ref[:, pl.ds(pl.multiple_of(j0 * t, t), gt)])
        for g in range(NSA_GROUP):
            s = s_ref[:, g * tq:(g + 1) * tq]
            if near:
                s = s + _sweep_bias(bs_ref, g, first_q, nq, gi * grp, grp)
            _flash_step(s, vt1, m_ref, acc_ref, g)

    _pipelined_sweep(_far_groups(first_q, grp), n_groups, scores, softmax, sa_ref, sb_ref)
    for g in range(NSA_GROUP):
        res_ref[g] = _flash_result(acc_ref, g)

    _flash_init(m_ref, acc_ref)
    rhs_win = with_extra_rows(jnp.zeros((HEAD_DIM, tq), BF16))
    w_first = first_q - (n_win - 1)
    for wg in range(-(-(nq + n_win - 1) // grp)):
        j0 = w_first + wg * grp
        start = pl.multiple_of(jnp.maximum(j0, 0) * t, t)
        sa_ref[...] = _dot(kw_ref[pl.ds(start, gt), :], rhs_win)
        vt1 = _with_ones(vwt_ref[:, pl.ds(start, gt)])
        for g in range(NSA_GROUP):
            rows = []
            for u in range(grp):
                tiles = []
                for a in range(nq):
                    delta = first_q + a - (j0 + u)
                    inside = jnp.logical_and(jnp.logical_and(delta >= 0, delta < n_win), j0 + u >= 0)
                    tiles.append(bw_ref[g, jnp.where(inside, delta, n_win)])
                rows.append(jnp.concatenate(tiles, axis=1))
            _flash_step(sa_ref[:, g * tq:(g + 1) * tq] + jnp.concatenate(rows, axis=0), vt1, m_ref, acc_ref, g)

    gate_ref[...] = _sigmoid(_flip(_eye(LANES), zg_ref[...]))
    mixed = []
    for g in range(NSA_GROUP):
        base = 3 * (NSA_GROUP * hk + g)
        gates = [gate_ref[pl.ds(base + c, 1), :] for c in range(3)]
        o_cmp = ocmp_ref[g * HEAD_DIM:(g + 1) * HEAD_DIM, :].astype(F32)
        mixed.append(gates[0] * o_cmp + gates[1] * res_ref[g] + gates[2] * _flash_result(acc_ref, g))
    o_ref[...] = _flip(_eye(tq), jnp.concatenate(mixed, axis=0).astype(BF16)).astype(o_ref.dtype)


def _nsa_attend(z, zt, sel, o_cmp, bias_slc, bias_win, cols):
    b, s, _ = z.shape
    t = ATT_T
    tq = ATT_Q_TILES * t
    n_slc = sel.shape[2]
    gw = NSA_GROUP * HEAD_DIM
    n_win = bias_win.shape[1] - 1
    assert (n_win - 1) % NSA_SWEEP_GROUP == 0 and ATT_Q_TILES % NSA_SWEEP_GROUP == 0
    n_tiles = bias_slc.shape[1]
    c_slc, c_win, c_gate = cols["k_slc"], cols["k_win"], cols["gate"]
    return pl.pallas_call(
        functools.partial(_nsa_attend_kernel, t=t, nq=ATT_Q_TILES, n_win=n_win),
        out_shape=jax.ShapeDtypeStruct((b, s, D_MODEL), BF16),
        grid=(b, NSA_KV, s // tq),
        in_specs=[pl.BlockSpec((None, tq, gw), lambda bi, hk, qi: (bi, qi, hk)),
                  pl.BlockSpec((None, s, LANES), lambda bi, hk, qi: (bi, 0, c_slc + hk // 2)),
                  pl.BlockSpec((None, s, LANES), lambda bi, hk, qi: (bi, 0, c_win + hk // 2)),
                  pl.BlockSpec((HEAD_DIM, s), lambda bi, hk, qi: (hk, bi)),
                  pl.BlockSpec((HEAD_DIM, s), lambda bi, hk, qi: (NSA_KV + hk, bi)),
                  pl.BlockSpec((None, None, n_slc, tq), lambda bi, hk, qi: (bi, hk, 0, qi)),
                  pl.BlockSpec((None, None, gw, tq), lambda bi, hk, qi: (bi, hk, 0, qi)),
                  pl.BlockSpec((None, tq, LANES), lambda bi, hk, qi: (bi, qi, c_gate)),
                  _resident((NSA_GROUP, n_tiles, t, t), lambda bi, hk, qi: (hk, 0, 0, 0)),
                  _resident((NSA_GROUP, n_win + 1, t, t), lambda bi, hk, qi: (hk, 0, 0, 0))],
        out_specs=pl.BlockSpec((None, tq, gw), lambda bi, hk, qi: (bi, qi, hk)),
        scratch_shapes=[pltpu.VMEM((NSA_GROUP, 1, tq), F32),
                        pltpu.VMEM((NSA_GROUP, HEAD_DIM + DEN_ROWS, tq), F32),
                        pltpu.VMEM((NSA_GROUP, HEAD_DIM, tq), F32),
                        pltpu.VMEM((LANES, tq), F32), pltpu.VMEM((n_slc, tq), F32),
                        pltpu.VMEM((NSA_SWEEP_GROUP * t, NSA_GROUP * tq), F32),
                        pltpu.VMEM((NSA_SWEEP_GROUP * t, NSA_GROUP * tq), F32)],
        compiler_params=_cparams(("parallel", "parallel", "arbitrary")),
        name="nsa_attend",
    )(z, z, z, zt, zt, sel, o_cmp, z, bias_slc, bias_win)


def _nsa_layout(w_in):
    d = w_in.shape[0]
    nq = N_HEADS * HEAD_DIM
    kvw = NSA_KV * HEAD_DIM
    kv = [w_in[:, nq + j * kvw: nq + (j + 1) * kvw] for j in range(6)]
    wg = w_in[:, nq + 6 * kvw:]
    wg = jnp.concatenate([wg, jnp.zeros((d, LANES - wg.shape[1]), w_in.dtype)], axis=-1)
    w = jnp.concatenate([w_in[:, :nq] * (SCALE * LOG2E), kv[2], kv[4], kv[0], kv[1], wg], axis=-1)
    wt = jnp.concatenate([kv[3], kv[5]], axis=-1).T
    cols = {"k_slc": nq // LANES, "k_win": (nq + kvw) // LANES, "k_cmp": nq + 2 * kvw,
            "v_cmp": nq + 3 * kvw, "gate": (nq + 4 * kvw) // LANES}
    return w.astype(BF16), wt.astype(BF16), cols


def _importance_matrix_np(n_slc, r):
    ratio = NSA_SLC_LEN // NSA_CMP_STRIDE
    span = NSA_CMP_LEN // NSA_CMP_STRIDE
    coef = np.convolve(np.ones(ratio), np.ones(span))
    cmat = np.zeros((n_slc, r), np.float32)
    for j in range(n_slc):
        for o, cf in enumerate(coef):
            c = ratio * j - o
            if c >= 0:
                cmat[j, c] = cf
    return cmat


def _nsa_mixer(h, g, b, s, w_in, cmp_pos, cmp_w1, cmp_w2, tab, bias_causal):
    w, wt, cols = _nsa_layout(w_in)
    z, zt = _norm_matmul(h, g, w, wt)
    z = z.reshape(b, s, -1)
    kvw = NSA_KV * HEAD_DIM
    r = s // NSA_CMP_STRIDE

    def rows_of_16(x):
        x = x.reshape(b, -1, NSA_KV, HEAD_DIM).transpose(2, 0, 1, 3)
        return x.reshape(NSA_KV, b, -1, NSA_CMP_STRIDE * HEAD_DIM)

    kv_c = jnp.stack([z[..., cols["k_cmp"]:cols["k_cmp"] + kvw], z[..., cols["v_cmp"]:cols["v_cmp"] + kvw]])
    nxt = jnp.concatenate([kv_c[:, :, NSA_CMP_STRIDE:], jnp.zeros_like(kv_c[:, :, :NSA_CMP_STRIDE])], axis=2)
    blocks = jnp.stack([rows_of_16(kv_c[0]), rows_of_16(kv_c[1])])
    blocks_next = jnp.stack([rows_of_16(nxt[0]), rows_of_16(nxt[1])])
    flat = NSA_CMP_LEN * HEAD_DIM
    cmp_tok, cmp_feat = _compress(blocks, blocks_next, cmp_w1.astype(BF16),
                                  cmp_pos.reshape(2, 1, flat).astype(BF16), cmp_w2.astype(BF16))

    n_slc = s // NSA_SLC_LEN
    cmat = jnp.asarray(_importance_matrix_np(n_slc, r), BF16)
    o_cmp, sel = _nsa_select(z, cmp_tok[0], cmp_feat[1], cmat, min(NSA_TOP_N, n_slc))
    win_tiles = _causal_bucket_tiles(ATT_T, -(-NSA_WINDOW // ATT_T) + 1, NSA_WINDOW)
    win_tiles = np.concatenate([win_tiles, np.full((1, ATT_T, ATT_T), MASKED_BUCKET, np.int32)], axis=0)
    bias_win = _build_bias_tiles(tab, win_tiles)
    return _nsa_attend(z, zt, sel, o_cmp, bias_causal, bias_win, cols)


def _scaled_qkv(w_in, scale=SCALE):
    return jnp.concatenate([w_in[:, :D_MODEL] * scale, w_in[:, D_MODEL:]], axis=1).astype(BF16)


def kernel(x, rel_bias, norm_gains, final_gain, a_w_in, a_lambda, a_subln, a_w_out, b_w_in, b_w_out,
           c_w_in, c_w_out, d_w_in, d_cmp_pos, d_cmp_w1, d_cmp_w2, d_w_out, ffn_w_gate, ffn_w_up,
           ffn_w_down, moe_router, moe_w_gate, moe_w_up, moe_w_down):
    b, s, d = x.shape
    assert d == D_MODEL and s % (B_PAIRS[-1][0]) == 0 and s >= (N_NEAR + 1) * ATT_T
    assert _t5_bucket_np(np.array([(N_NEAR - 1) * ATT_T + 1]))[0] == NUM_BUCKETS - 1
    n = b * s
    h = x.reshape(n, d)
    tab = rel_bias.T.astype(F32)
    tab_rel = (tab - tab[:, NUM_BUCKETS - 1:]) * LOG2E
    bias_causal = _build_bias_tiles(tab_rel, _sweep_bucket_tiles(ATT_T))

    w = _scaled_qkv(a_w_in[0], SCALE * LOG2E)
    z, zt = _norm_matmul(h, norm_gains[0, 0], w[:, :2 * d], w[:, 2 * d:].T)
    lam_init = 0.8 - 0.6 * math.exp(-0.3 * 0)
    o = _diff_attention(z.reshape(b, s, -1), zt, a_lambda[0].astype(F32), a_subln[0].astype(F32),
                        bias_causal, lam_init)
    h = _proj_residual(o.reshape(n, d), a_w_out[0].astype(BF16), h)
    h = _ffn(h, norm_gains[0, 1], ffn_w_gate[0].astype(BF16), ffn_w_up[0].astype(BF16),
             ffn_w_down[0].astype(BF16))

    dils = [dil for _, dil in B_PAIRS]
    views = _norm_matmul_views(h, norm_gains[1, 0], _scaled_qkv(b_w_in[0]), dils)
    outs, lses = [], []
    for zv, dil in zip(views, dils):
        bias = _build_bias_tiles(tab, _dilated_bucket_tiles(dil))
        o, lse = _dilated_group(zv.reshape(b, s // dil, -1), bias, dil)
        outs.append(o)
        lses.append(lse)
    h = _dil_combine_proj(outs, lses, dils, b_w_out[0].astype(BF16), h)
    h = _moe(h, norm_gains[1, 1], moe_router[0], moe_w_gate[0].astype(BF16), moe_w_up[0].astype(BF16),
             moe_w_down[0].astype(BF16), final_gain, False)

    w = _scaled_qkv(c_w_in[0])
    z, zt = _norm_matmul(h, norm_gains[2, 0], w[:, :2 * d], w[:, 2 * d:].T)
    o = _stick_attention(z.reshape(b, s, -1), zt)
    h = _proj_residual(o.reshape(n, d), c_w_out[0].astype(BF16), h)
    h = _ffn(h, norm_gains[2, 1], ffn_w_gate[1].astype(BF16), ffn_w_up[1].astype(BF16),
             ffn_w_down[1].astype(BF16))

    o = _nsa_mixer(h, norm_gains[3, 0], b, s, d_w_in[0], d_cmp_pos[0], d_cmp_w1[0], d_cmp_w2[0], tab_rel,
                   bias_causal)
    h = _proj_residual(o.reshape(n, d), d_w_out[0].astype(BF16), h)
    h = _moe(h, norm_gains[3, 1], moe_router[1], moe_w_gate[1].astype(BF16), moe_w_up[1].astype(BF16),
             moe_w_down[1].astype(BF16), final_gain, True)
    return h.reshape(b, s, d)
```
